```python
import math
import jax, jax.numpy as jnp
from jax import lax
import numpy as np

D_MODEL = 1024
BATCH = 8
SEQ = 2048
DEPTH = 4

GRID_W = 64
CTX_LEN = 256
N_EVEN = (DEPTH + 1) // 2
N_ODD = DEPTH // 2
MIX_WIDTH = D_MODEL
FFN_HIDDEN = ((8 * D_MODEL + 3 * 256 - 1) // (3 * 256)) * 256
S5_WIDTH = D_MODEL // 2
S5_GROUP = 16
S5_GROUPS = S5_WIDTH // S5_GROUP
S5_STATE = 64
HGRN_HEADS = 4
HGRN_HEAD_DIM = D_MODEL // 8
HGRN_WIDTH = HGRN_HEADS * HGRN_HEAD_DIM
HGRN_CHUNK = 64
MAX_EXP_ARG = 60.0
DIFF_HEADS = 4
DIFF_HEAD_DIM = D_MODEL // 16
DIFF_WIDTH = DIFF_HEADS * 2 * DIFF_HEAD_DIM
MLA_HEADS = 4
MLA_NOPE_DIM = D_MODEL // 8
MLA_ROPE_DIM = D_MODEL // 16
MLA_V_DIM = D_MODEL // 8
MLA_Q_RANK = 3 * D_MODEL // 8
MLA_KV_RANK = D_MODEL // 4
MLA_WIDTH = MLA_HEADS * MLA_V_DIM
ROPE_DIM = D_MODEL // 16
ROPE_BASE = 10000.0
Q_BLOCK = 128
EPS = 1e-6
AB_SPLITS = (S5_WIDTH, S5_WIDTH + HGRN_WIDTH, S5_WIDTH + 2 * HGRN_WIDTH, S5_WIDTH + 3 * HGRN_WIDTH, S5_WIDTH + 4 * HGRN_WIDTH)
AB_IN = S5_WIDTH + 5 * HGRN_WIDTH
CD_SPLITS = (DIFF_WIDTH, 2 * DIFF_WIDTH, 3 * DIFF_WIDTH, 3 * DIFF_WIDTH + MLA_Q_RANK, 3 * DIFF_WIDTH + MLA_Q_RANK + MLA_KV_RANK)
CD_IN = 3 * DIFF_WIDTH + MLA_Q_RANK + MLA_KV_RANK + MLA_ROPE_DIM

kernel_name = "hybrid_s5_hgrn2_diffattn_mla_dit"


def _rmsnorm(x, g):
    xf = x.astype(jnp.float32)
    y = xf * lax.rsqrt(jnp.mean(xf * xf, axis=-1, keepdims=True) + EPS)
    return (y * g.astype(jnp.float32)).astype(x.dtype)


def _modulate(x, shift, scale):
    return x * (1 + scale) + shift


def _swiglu(a, w_in, w_out):
    gate, up = jnp.split(a @ w_in, 2, axis=-1)
    return (jax.nn.silu(gate) * up) @ w_out


def _rope_tables(rows, cols):
    n_freq = ROPE_DIM // 4
    inv = jnp.power(ROPE_BASE, -jnp.arange(n_freq, dtype=jnp.float32) / n_freq)
    ang_r = rows.astype(jnp.float32)[:, None] * inv
    ang_c = cols.astype(jnp.float32)[:, None] * inv
    ang = jnp.concatenate([ang_r, ang_r, ang_c, ang_c], axis=-1)
    return jnp.cos(ang), jnp.sin(ang)


def _apply_rope(x, cos, sin):
    xr = x.reshape(x.shape[:-1] + (2, 2, ROPE_DIM // 4))
    rot = jnp.concatenate([-xr[..., 1:2, :], xr[..., 0:1, :]], axis=-2).reshape(x.shape)
    out = x.astype(jnp.float32) * cos[None, :, None, :] + rot.astype(jnp.float32) * sin[None, :, None, :]
    return out.astype(x.dtype)


def _sweep_query_blocks(fn, *qs):
    b, l = qs[0].shape[:2]
    nb = l // Q_BLOCK
    blocks = tuple(jnp.moveaxis(q.reshape((b, nb, Q_BLOCK) + q.shape[2:]), 1, 0) for q in qs)
    out = lax.map(lambda blk: fn(*blk), blocks)
    return jnp.moveaxis(out, 0, 1).reshape((b, l) + out.shape[3:])


def _softmax_attention(q, k, v, scale):
    def blk(bq):
        s = jnp.einsum('bqhd,bkhd->bhqk', bq, k).astype(jnp.float32) * scale
        p = jax.nn.softmax(s, axis=-1)
        return jnp.einsum('bhqk,bkhe->bqhe', p.astype(v.dtype), v)
    return _sweep_query_blocks(blk, q)


def _diff_attention(q1, q2, k1, k2, v, lam):
    scale = DIFF_HEAD_DIM ** -0.5
    def blk(b1, b2):
        s1 = jnp.einsum('bqhd,bkhd->bhqk', b1, k1).astype(jnp.float32) * scale
        s2 = jnp.einsum('bqhd,bkhd->bhqk', b2, k2).astype(jnp.float32) * scale
        p = jax.nn.softmax(s1, axis=-1) - lam * jax.nn.softmax(s2, axis=-1)
        return jnp.einsum('bhqk,bkhe->bqhe', p.astype(v.dtype), v)
    return _sweep_query_blocks(blk, q1, q2)


def _s5_discretize(lam_re, lam_im, log_step, b_re, b_im):
    lr = jnp.minimum(lam_re.astype(jnp.float32), -1e-4)
    li = lam_im.astype(jnp.float32)
    step = jnp.exp(log_step.astype(jnp.float32))[:, None]
    mag = jnp.exp(lr * step)
    a_r = mag * jnp.cos(li * step)
    a_i = mag * jnp.sin(li * step)
    den = lr * lr + li * li
    coef_r = ((a_r - 1) * lr + a_i * li) / den
    coef_i = (a_i * lr - (a_r - 1) * li) / den
    br = b_re.astype(jnp.float32)
    bi = b_im.astype(jnp.float32)
    bb_r = coef_r[..., None] * br - coef_i[..., None] * bi
    bb_i = coef_r[..., None] * bi + coef_i[..., None] * br
    return a_r, a_i, bb_r, bb_i


def _complex_affine_combine(e1, e2):
    a1r, a1i, b1r, b1i = e1
    a2r, a2i, b2r, b2i = e2
    return (a2r * a1r - a2i * a1i, a2r * a1i + a2i * a1r,
            a2r * b1r - a2i * b1i + b2r, a2r * b1i + a2i * b1r + b2i)


def _s5_scan(u, a_r, a_i, bb_r, bb_i, h0_r, h0_i):
    bu_r = jnp.einsum('blgc,gpc->blgp', u, bb_r)
    bu_i = jnp.einsum('blgc,gpc->blgp', u, bb_i)
    bu_r = bu_r.at[:, 0].add(a_r * h0_r - a_i * h0_i)
    bu_i = bu_i.at[:, 0].add(a_r * h0_i + a_i * h0_r)
    l = u.shape[1]
    ar = jnp.broadcast_to(a_r, (1, l) + a_r.shape)
    ai = jnp.broadcast_to(a_i, (1, l) + a_i.shape)
    _, _, h_r, h_i = lax.associative_scan(_complex_affine_combine, (ar, ai, bu_r, bu_i), axis=1)
    return h_r, h_i


def _s5_readout(h_r, h_i, c_re, c_im):
    return jnp.einsum('blgp,gcp->blgc', h_r, c_re) - jnp.einsum('blgp,gcp->blgc', h_i, c_im)


def _s5_mixer(ux, uc, lam_re, lam_im, log_step, b_re, b_im, c_re, c_im, d_skip, glu_w, glu_b, need_ctx):
    def groups(u):
        return u.astype(jnp.float32).reshape(u.shape[:2] + (S5_GROUPS, S5_GROUP))
    gx, gc = groups(ux), groups(uc)
    d_g = d_skip.astype(jnp.float32).reshape(S5_GROUPS, S5_GROUP)
    zeros = jnp.zeros((ux.shape[0], S5_GROUPS, S5_STATE), jnp.float32)
    y_x = gx * d_g
    y_c = gc * d_g
    for d in range(2):
        a_r, a_i, bb_r, bb_i = _s5_discretize(lam_re[d], lam_im[d], log_step[d], b_re[d], b_im[d])
        cr, ci = c_re[d].astype(jnp.float32), c_im[d].astype(jnp.float32)
        flip = d == 1
        def orient(t):
            return jnp.flip(t, axis=1) if flip else t
        hc_r, hc_i = _s5_scan(orient(gc), a_r, a_i, bb_r, bb_i, zeros, zeros)
        hx_r, hx_i = _s5_scan(orient(gx), a_r, a_i, bb_r, bb_i, hc_r[:, -1], hc_i[:, -1])
        y_x = y_x + orient(_s5_readout(hx_r, hx_i, cr, ci))
        if need_ctx:
            y_c = y_c + orient(_s5_readout(hc_r, hc_i, cr, ci))
    def glu(y):
        z = jax.nn.gelu(y.reshape(y.shape[:2] + (S5_WIDTH,)))
        return z * jax.nn.sigmoid(z @ glu_w.astype(jnp.float32) + glu_b.astype(jnp.float32))
    out_x = glu(y_x).astype(ux.dtype)
    out_c = glu(y_c).astype(uc.dtype) if need_ctx else None
    return out_x, out_c


def _gla_chunk_scan(q, k, log_f, v, s0):
    b, l, h, _ = q.shape
    dv = v.shape[-1]
    n = l // HGRN_CHUNK
    def chunks(t):
        return t.reshape(b, n, HGRN_CHUNK, h, t.shape[-1]).transpose(1, 0, 3, 2, 4)
    seen = jnp.tril(jnp.ones((HGRN_CHUNK, HGRN_CHUNK), dtype=bool))[None, None, :, :, None]
    def step(state, inp):
        qc, kc, gc, vc = inp
        cum = jnp.cumsum(gc, axis=2)
        inter = jnp.einsum('bhtk,bhkv->bhtv', qc * jnp.exp(cum), state)
        rel = cum[:, :, :, None, :] - cum[:, :, None, :, :]
        decay = jnp.where(seen, jnp.exp(jnp.where(seen, rel, 0.0)), 0.0)
        scores = jnp.einsum('bhtk,bhsk,bhtsk->bhts', qc, kc, decay)
        intra = jnp.einsum('bhts,bhsv->bhtv', scores, vc)
        last = cum[:, :, -1:, :]
        new_state = (jnp.exp(last[:, :, 0, :])[..., None] * state
                     + jnp.einsum('bhsk,bhsv->bhkv', kc * jnp.exp(last - cum), vc))
        return new_state, inter + intra
    s_fin, o = lax.scan(step, s0, (chunks(q), chunks(k), chunks(log_f), chunks(v)))
    o = o.transpose(1, 0, 3, 2, 4).reshape(b, l, h, dv)
    return o, s_fin


def _hgrn2_mixer(qx, qc, ffx, ffc, fbx, fbc, ix, ic, gx, gc, lb, out_norm, need_ctx):
    def heads(t):
        return t.astype(jnp.float32).reshape(t.shape[:2] + (HGRN_HEADS, HGRN_HEAD_DIM))
    lb_h = lb.reshape(HGRN_HEADS, HGRN_HEAD_DIM)
    def gates(fl):
        fl = heads(fl)
        k = (1 - lb_h) * jax.nn.sigmoid(-fl)
        log_f = jax.nn.log_sigmoid(fl) + jnp.log1p(lb_h * jnp.exp(jnp.minimum(-fl, MAX_EXP_ARG)))
        return k, log_f
    q_x, q_c = jax.nn.silu(heads(qx)), jax.nn.silu(heads(qc))
    i_x, i_c = heads(ix), heads(ic)
    s0 = jnp.zeros((qx.shape[0], HGRN_HEADS, HGRN_HEAD_DIM, HGRN_HEAD_DIM), jnp.float32)
    o_x_parts, o_c_parts = [], []
    for d, (f_x, f_c) in enumerate(((ffx, ffc), (fbx, fbc))):
        flip = d == 1
        def orient(t):
            return jnp.flip(t, axis=1) if flip else t
        k_x, lf_x = gates(f_x)
        k_c, lf_c = gates(f_c)
        o_c, s_ctx = _gla_chunk_scan(orient(q_c), orient(k_c), orient(lf_c), orient(i_c), s0)
        o_x, _ = _gla_chunk_scan(orient(q_x), orient(k_x), orient(lf_x), orient(i_x), s_ctx)
        o_x_parts.append(orient(o_x))
        o_c_parts.append(orient(o_c))
    def readout(o, g):
        y = _rmsnorm(o, out_norm) * jax.nn.silu(heads(g))
        return y.reshape(o.shape[:2] + (HGRN_WIDTH,)).astype(g.dtype)
    out_x = readout(o_x_parts[0] + o_x_parts[1], gx)
    out_c = readout(o_c_parts[0] + o_c_parts[1], gc) if need_ctx else None
    return out_x, out_c


def _even_mixer(ax, ac, w_in, lam_re, lam_im, log_step, b_re, b_im, c_re, c_im, d_skip, glu_w, glu_b,
                lb, out_norm, need_ctx):
    ux, qx, ffx, fbx, ix, gx = jnp.split(ax @ w_in, AB_SPLITS, axis=-1)
    uc, qc, ffc, fbc, ic, gc = jnp.split(ac @ w_in, AB_SPLITS, axis=-1)
    s5x, s5c = _s5_mixer(ux, uc, lam_re, lam_im, log_step, b_re, b_im, c_re, c_im, d_skip, glu_w, glu_b, need_ctx)
    hx, hc = _hgrn2_mixer(qx, qc, ffx, ffc, fbx, fbc, ix, ic, gx, gc, lb, out_norm, need_ctx)
    out_x = jnp.concatenate([s5x, hx], axis=-1)
    out_c = jnp.concatenate([s5c, hc], axis=-1) if need_ctx else None
    return out_x, out_c


def _odd_mixer(ax, ac, w_in, lam_vec, lam_init, qk_norm, subln, q_a_norm, kv_a_norm, w_uq, w_ukv,
               nope_norm, rope_norm, cos, sin, need_ctx):
    dqx, dkx, dvx, cqx, ckvx, krx = jnp.split(ax @ w_in, CD_SPLITS, axis=-1)
    dqc, dkc, dvc, cqc, ckvc, krc = jnp.split(ac @ w_in, CD_SPLITS, axis=-1)
    lv = lam_vec.astype(jnp.float32)
    lam = jnp.exp(jnp.sum(lv[0] * lv[1])) - jnp.exp(jnp.sum(lv[2] * lv[3])) + lam_init

    def diff_q(p, rope):
        q = _rmsnorm(p.reshape(p.shape[:2] + (DIFF_HEADS, 2, DIFF_HEAD_DIM)), qk_norm[0])
        q1, q2 = q[..., 0, :], q[..., 1, :]
        if rope:
            q1, q2 = _apply_rope(q1, cos, sin), _apply_rope(q2, cos, sin)
        return q1, q2

    def diff_kv(pk, pv, rope):
        k = _rmsnorm(pk.reshape(pk.shape[:2] + (DIFF_HEADS, 2, DIFF_HEAD_DIM)), qk_norm[1])
        k1, k2 = k[..., 0, :], k[..., 1, :]
        if rope:
            k1, k2 = _apply_rope(k1, cos, sin), _apply_rope(k2, cos, sin)
        return k1, k2, pv.reshape(pv.shape[:2] + (DIFF_HEADS, 2 * DIFF_HEAD_DIM))

    def mla_q(cq, rope):
        q = (_rmsnorm(cq, q_a_norm) @ w_uq).reshape(cq.shape[:2] + (MLA_HEADS, MLA_NOPE_DIM + MLA_ROPE_DIM))
        q_nope = _rmsnorm(q[..., :MLA_NOPE_DIM], nope_norm[0])
        q_rope = _rmsnorm(q[..., MLA_NOPE_DIM:], rope_norm[0])
        if rope:
            q_rope = _apply_rope(q_rope, cos, sin)
        return jnp.concatenate([q_nope, q_rope], axis=-1)

    def mla_kv(ckv, kr, rope):
        shp = ckv.shape[:2]
        kv = (_rmsnorm(ckv, kv_a_norm) @ w_ukv).reshape(shp + (MLA_HEADS, MLA_NOPE_DIM + MLA_V_DIM))
        k_nope = _rmsnorm(kv[..., :MLA_NOPE_DIM], nope_norm[1])
        k_rope = _rmsnorm(kr[:, :, None, :], rope_norm[1])
        if rope:
            k_rope = _apply_rope(k_rope, cos, sin)
        k = jnp.concatenate([k_nope, jnp.broadcast_to(k_rope, shp + (MLA_HEADS, MLA_ROPE_DIM))], axis=-1)
        return k, kv[..., MLA_NOPE_DIM:]

    def cat(a, b):
        return jnp.concatenate([a, b], axis=1)

    def diff_out(o):
        return (_rmsnorm(o, subln) * (1.0 - lam_init)).reshape(o.shape[:2] + (DIFF_WIDTH,))

    mla_scale = (MLA_NOPE_DIM + MLA_ROPE_DIM) ** -0.5
    k1c, k2c, vc = diff_kv(dkc, dvc, False)
    k1x, k2x, vx = diff_kv(dkx, dvx, True)
    mkc, mvc = mla_kv(ckvc, krc, False)
    mkx, mvx = mla_kv(ckvx, krx, True)
    q1x, q2x = diff_q(dqx, True)
    c_out_x = diff_out(_diff_attention(q1x, q2x, cat(k1c, k1x), cat(k2c, k2x), cat(vc, vx), lam))
    d_out_x = _softmax_attention(mla_q(cqx, True), cat(mkc, mkx), cat(mvc, mvx), mla_scale)
    out_x = jnp.concatenate([c_out_x, d_out_x.reshape(cqx.shape[:2] + (MLA_WIDTH,))], axis=-1)
    out_c = None
    if need_ctx:
        q1c, q2c = diff_q(dqc, False)
        c_out_c = diff_out(_diff_attention(q1c, q2c, k1c, k2c, vc, lam))
        d_out_c = _softmax_attention(mla_q(cqc, False), mkc, mvc, mla_scale)
        out_c = jnp.concatenate([c_out_c, d_out_c.reshape(cqc.shape[:2] + (MLA_WIDTH,))], axis=-1)
    return out_x, out_c


def setup_inputs(seed: int = 0) -> dict:
    key = jax.random.key(seed)
    ks = iter(jax.random.split(key, 34))
    f32 = jnp.float32

    def nrm(shape, std):
        return std * jax.random.normal(next(ks), shape, f32)

    def gain(shape):
        return 1.0 + nrm(shape, 0.1)

    inputs = {
        "x": nrm((BATCH, SEQ, D_MODEL), 1.0),
        "c": nrm((BATCH, D_MODEL), 1.0),
        "ctx": nrm((BATCH, CTX_LEN, D_MODEL), 1.0),
        "c_ctx": nrm((D_MODEL,), 1.0),
        "ada_w": nrm((DEPTH, D_MODEL, 6 * D_MODEL), 0.5 * D_MODEL ** -0.5),
        "ada_b": nrm((DEPTH, 6 * D_MODEL), 0.02),
        "norm_mix": gain((DEPTH, D_MODEL)),
        "norm_ffn": gain((DEPTH, D_MODEL)),
        "w_out": nrm((DEPTH, MIX_WIDTH, D_MODEL), MIX_WIDTH ** -0.5),
        "ffn_w_in": nrm((DEPTH, D_MODEL, 2 * FFN_HIDDEN), D_MODEL ** -0.5),
        "ffn_w_out": nrm((DEPTH, FFN_HIDDEN, D_MODEL), FFN_HIDDEN ** -0.5),
        "ab_w_in": nrm((N_EVEN, D_MODEL, AB_IN), D_MODEL ** -0.5),
        "s5_lambda_re": -0.5 + nrm((N_EVEN, 2, S5_GROUPS, S5_STATE), 0.01),
        "s5_lambda_im": jnp.pi * jnp.arange(S5_STATE, dtype=f32) + nrm((N_EVEN, 2, S5_GROUPS, S5_STATE), 0.01),
        "s5_log_step": jax.random.uniform(next(ks), (N_EVEN, 2, S5_GROUPS), f32, math.log(1e-3), math.log(1e-1)),
        "s5_b_re": nrm((N_EVEN, 2, S5_GROUPS, S5_STATE, S5_GROUP), (2 * S5_GROUP) ** -0.5),
        "s5_b_im": nrm((N_EVEN, 2, S5_GROUPS, S5_STATE, S5_GROUP), (2 * S5_GROUP) ** -0.5),
        "s5_c_re": nrm((N_EVEN, 2, S5_GROUPS, S5_GROUP, S5_STATE), (2 * S5_STATE) ** -0.5),
        "s5_c_im": nrm((N_EVEN, 2, S5_GROUPS, S5_GROUP, S5_STATE), (2 * S5_STATE) ** -0.5),
        "s5_d": nrm((N_EVEN, S5_WIDTH), 1.0),
        "s5_glu_w": nrm((N_EVEN, S5_WIDTH, S5_WIDTH), S5_WIDTH ** -0.5),
        "s5_glu_b": nrm((N_EVEN, S5_WIDTH), 0.02),
        "hgrn_lb_logits": nrm((N_EVEN, HGRN_WIDTH), 0.5),
        "hgrn_out_norm": gain((N_EVEN, HGRN_HEAD_DIM)),
        "cd_w_in": nrm((N_ODD, D_MODEL, CD_IN), D_MODEL ** -0.5),
        "diff_lambda": nrm((N_ODD, 4, DIFF_HEAD_DIM), 0.1),
        "diff_qk_norm": gain((N_ODD, 2, DIFF_HEAD_DIM)),
        "diff_subln": gain((N_ODD, 2 * DIFF_HEAD_DIM)),
        "mla_q_a_norm": gain((N_ODD, MLA_Q_RANK)),
        "mla_kv_a_norm": gain((N_ODD, MLA_KV_RANK)),
        "mla_w_uq": nrm((N_ODD, MLA_Q_RANK, MLA_HEADS * (MLA_NOPE_DIM + MLA_ROPE_DIM)), MLA_Q_RANK ** -0.5),
        "mla_w_ukv": nrm((N_ODD, MLA_KV_RANK, MLA_HEADS * (MLA_NOPE_DIM + MLA_V_DIM)), MLA_KV_RANK ** -0.5),
        "mla_nope_norm": gain((N_ODD, 2, MLA_NOPE_DIM)),
        "mla_rope_norm": gain((N_ODD, 2, MLA_ROPE_DIM)),
    }
    return inputs


def reference(x, c, ctx, c_ctx, ada_w, ada_b, norm_mix, norm_ffn, w_out, ffn_w_in, ffn_w_out, ab_w_in,
              s5_lambda_re, s5_lambda_im, s5_log_step, s5_b_re, s5_b_im, s5_c_re, s5_c_im, s5_d, s5_glu_w,
              s5_glu_b, hgrn_lb_logits, hgrn_out_norm, cd_w_in, diff_lambda, diff_qk_norm, diff_subln,
              mla_q_a_norm, mla_kv_a_norm, mla_w_uq, mla_w_ukv, mla_nope_norm, mla_rope_norm):
    n_tok = x.shape[1]
    ROWS = n_tok // GRID_W
    rows = jnp.repeat(jnp.arange(ROWS, dtype=jnp.int32), GRID_W)
    cols = jnp.tile(jnp.arange(GRID_W, dtype=jnp.int32), ROWS)
    cos, sin = _rope_tables(rows, cols)

    lb_p = jax.nn.softmax(hgrn_lb_logits.astype(jnp.float32), axis=0)
    lower_bounds = jnp.cumsum(lb_p, axis=0) - lb_p[0:1]

    sc = jax.nn.silu(c)
    scc = jax.nn.silu(c_ctx)
    h, hc = x, ctx
    for l in range(DEPTH):
        need_ctx = l < DEPTH - 1
        mod_x = jnp.split((sc @ ada_w[l] + ada_b[l])[:, None, :], 6, axis=-1)
        mod_c = jnp.split((scc @ ada_w[l] + ada_b[l])[None, None, :], 6, axis=-1)
        ax = _modulate(_rmsnorm(h, norm_mix[l]), mod_x[0], mod_x[1])
        ac = _modulate(_rmsnorm(hc, norm_mix[l]), mod_c[0], mod_c[1])
        if l % 2 == 0:
            e = l // 2
            mx, mc = _even_mixer(ax, ac, ab_w_in[e], s5_lambda_re[e], s5_lambda_im[e], s5_log_step[e],
                                 s5_b_re[e], s5_b_im[e], s5_c_re[e], s5_c_im[e], s5_d[e], s5_glu_w[e],
                                 s5_glu_b[e], lower_bounds[e], hgrn_out_norm[e], need_ctx)
        else:
            o = l // 2
            lam_init = 0.8 - 0.6 * math.exp(-0.3 * l)
            mx, mc = _odd_mixer(ax, ac, cd_w_in[o], diff_lambda[o], lam_init, diff_qk_norm[o], diff_subln[o],
                                mla_q_a_norm[o], mla_kv_a_norm[o], mla_w_uq[o], mla_w_ukv[o],
                                mla_nope_norm[o], mla_rope_norm[o], cos, sin, need_ctx)
        h = h + mod_x[2] * (mx @ w_out[l])
        h = h + mod_x[5] * _swiglu(_modulate(_rmsnorm(h, norm_ffn[l]), mod_x[3], mod_x[4]), ffn_w_in[l], ffn_w_out[l])
        if need_ctx:
            hc = hc + mod_c[2] * (mc @ w_out[l])
            hc = hc + mod_c[5] * _swiglu(_modulate(_rmsnorm(hc, norm_ffn[l]), mod_c[3], mod_c[4]), ffn_w_in[l], ffn_w_out[l])
    return h
```

```python
import functools
import math

import numpy as np
import jax
import jax.numpy as jnp
from jax import lax
from jax.experimental import pallas as pl
from jax.experimental.pallas import tpu as pltpu

F32 = jnp.float32
BF16 = jnp.bfloat16

D = 1024
B = 8
SEQ = 2048
CTX = 256
T = CTX + SEQ
DEPTH = 4
GRID_W = 64
FFN_H = ((8 * D + 3 * 256 - 1) // (3 * 256)) * 256
S5_W = D // 2
S5_GROUP = 16
S5_GROUPS = S5_W // S5_GROUP
S5_STATE = 64
HG_HEADS = 4
HG_DIM = D // 8
HG_W = HG_HEADS * HG_DIM
MAX_EXP_ARG = 60.0
DIFF_HEADS = 4
DIFF_DIM = D // 16
DIFF_W = DIFF_HEADS * 2 * DIFF_DIM
MLA_HEADS = 4
MLA_NOPE = D // 8
MLA_ROPE = D // 16
MLA_V = D // 8
MLA_Q_RANK = 3 * D // 8
MLA_KV_RANK = D // 4
ROPE_DIM = D // 16
ROPE_BASE = 10000.0
EPS = 1e-6
AB_IN = S5_W + 5 * HG_W
CD_IN = 3 * DIFF_W + MLA_Q_RANK + MLA_KV_RANK + MLA_ROPE
CD_PAD = CD_IN + 64

LANE = 128
SUBLANE = 8
TM = 256
NT = T // TM
MOD_ROWS = 16
CTX_ROW = B
S5_LC = 512
S5_NC = S5_GROUPS * S5_STATE // S5_LC
S5_UC = S5_LC // S5_STATE * S5_GROUP
HG_C = TM
VMEM_LIMIT = 56 * 1024 * 1024


def _cparams(sem):
    return pltpu.CompilerParams(dimension_semantics=sem, vmem_limit_bytes=VMEM_LIMIT)


def _const_spec(shape):
    n = len(shape)
    return pl.BlockSpec(shape, lambda *_: (0,) * n, pipeline_mode=pl.Buffered(1))


def _silu(x):
    return x * jax.nn.sigmoid(x)


def _norm_mod(x, g, shift, scale):
    ms = jnp.mean(x * x, axis=-1, keepdims=True)
    return x * lax.rsqrt(ms + EPS) * g * (1.0 + scale) + shift


def _dot(a, b):
    return jnp.dot(a, b, preferred_element_type=F32)


def _dot_nt(a, b):
    return lax.dot_general(a, b, (((1,), (1,)), ((), ())), preferred_element_type=F32)


def _mod_kernel(s_ref, w_ref, b_ref, o_ref):
    s = _silu(s_ref[...])
    o_ref[...] = _dot(s.astype(BF16), w_ref[...].astype(BF16)) + b_ref[...]


def _modulation(c, c_ctx, ada_w, ada_b):
    s = jnp.zeros((MOD_ROWS, D), F32).at[:B].set(c).at[CTX_ROW].set(c_ctx)
    nb = 1536
    out = pl.pallas_call(
        _mod_kernel,
        grid=(DEPTH, 6 * D // nb),
        in_specs=[
            pl.BlockSpec((MOD_ROWS, D), lambda l, n: (0, 0)),
            pl.BlockSpec((None, D, nb), lambda l, n: (l, 0, n)),
            pl.BlockSpec((None, 1, nb), lambda l, n: (l, 0, n)),
        ],
        out_specs=pl.BlockSpec((None, MOD_ROWS, nb), lambda l, n: (l, 0, n)),
        out_shape=jax.ShapeDtypeStruct((DEPTH, MOD_ROWS, 6 * D), F32),
        compiler_params=_cparams(("arbitrary", "arbitrary")),
        name="adaln_mod",
    )(s, ada_w, ada_b.reshape(DEPTH, 1, 6 * D))
    return out.reshape(DEPTH, MOD_ROWS, 6, D)


def _mod_spec(t0):
    return pl.BlockSpec((None, 6, D), lambda b, t: (jnp.where(t + t0 == 0, CTX_ROW, b), 0, 0))


def _even_in_kernel(x_ref, mod_ref, g_ref, w_ref, u_ref, hg_ref):
    a = _norm_mod(x_ref[...], g_ref[...], mod_ref[0:1, :], mod_ref[1:2, :])
    p = _dot(a.astype(BF16), w_ref[...])
    u_ref[...] = p[:, :S5_W]
    hg_ref[...] = p[:, S5_W:]


def _even_in(hs, mod_l, g, w):
    return pl.pallas_call(
        _even_in_kernel,
        grid=(B, NT),
        in_specs=[
            pl.BlockSpec((None, TM, D), lambda b, t: (b, t, 0)),
            _mod_spec(0),
            _const_spec((1, D)),
            _const_spec((D, AB_IN)),
        ],
        out_specs=[
            pl.BlockSpec((TM, S5_W), lambda b, t: (t, b)),
            pl.BlockSpec((None, TM, 5 * HG_W), lambda b, t: (b, t, 0)),
        ],
        out_shape=[
            jax.ShapeDtypeStruct((T, B * S5_W), F32),
            jax.ShapeDtypeStruct((B, T, 5 * HG_W), F32),
        ],
        compiler_params=_cparams(("arbitrary", "arbitrary")),
        name="even_in_proj",
    )(hs, mod_l, g, w)


def _s5_kernel(u_ref, w_ref, c_ref, ar_ref, ai_ref, y_ref, bu_ref, h_ref):
    d = pl.program_id(0)
    i = pl.program_id(2)

    @pl.when(i == 0)
    def _():
        h_ref[...] = jnp.zeros_like(h_ref)

    u = u_ref[...].reshape(TM * B, S5_UC).astype(BF16)
    bu_ref[...] = _dot(u, w_ref[...])
    ar = ar_ref[...]
    ai = ai_ref[...]

    def step(j, carry):
        hr, hi = carry
        t = jnp.where(d == 0, j, TM - 1 - j)
        r0 = pl.multiple_of(t * B, SUBLANE)
        br = bu_ref[pl.ds(r0, B), 0:S5_LC]
        bi = bu_ref[pl.ds(r0, B), S5_LC:2 * S5_LC]
        nr = ar * hr - ai * hi + br
        ni = ar * hi + ai * hr + bi
        bu_ref[pl.ds(r0, B), 0:S5_LC] = nr
        bu_ref[pl.ds(r0, B), S5_LC:2 * S5_LC] = ni
        return nr, ni

    hr, hi = lax.fori_loop(0, TM, step, (h_ref[0], h_ref[1]), unroll=8)
    h_ref[0] = hr
    h_ref[1] = hi
    y = _dot(bu_ref[...].astype(BF16), c_ref[...])
    y_ref[...] = y.reshape(TM, B, S5_UC)


def _seq_tile(d, i):
    return jnp.where(d == 0, i, jnp.where(i == 0, 0, NT - i))


def _s5_scan(u_tm, w_in, w_out, a_r, a_i):
    u3 = u_tm.reshape(T, B, S5_W)
    return pl.pallas_call(
        _s5_kernel,
        grid=(2, S5_NC, NT),
        in_specs=[
            pl.BlockSpec((TM, B, S5_UC), lambda d, c, i: (_seq_tile(d, i), 0, c)),
            pl.BlockSpec((None, None, S5_UC, 2 * S5_LC), lambda d, c, i: (d, c, 0, 0)),
            pl.BlockSpec((None, None, 2 * S5_LC, S5_UC), lambda d, c, i: (d, c, 0, 0)),
            pl.BlockSpec((None, None, B, S5_LC), lambda d, c, i: (d, c, 0, 0)),
            pl.BlockSpec((None, None, B, S5_LC), lambda d, c, i: (d, c, 0, 0)),
        ],
        out_specs=pl.BlockSpec((None, TM, B, S5_UC), lambda d, c, i: (d, _seq_tile(d, i), 0, c)),
        out_shape=jax.ShapeDtypeStruct((2, T, B, S5_W), F32),
        scratch_shapes=[
            pltpu.VMEM((TM * B, 2 * S5_LC), F32),
            pltpu.VMEM((2, B, S5_LC), F32),
        ],
        compiler_params=_cparams(("arbitrary", "arbitrary", "arbitrary")),
        name="s5_scan",
    )(u3, w_in, w_out, a_r, a_i)


def _s5_params(lam_re, lam_im, log_step, b_re, b_im, c_re, c_im):
    lr = jnp.minimum(lam_re.astype(F32), -1e-4)
    li = lam_im.astype(F32)
    step = jnp.exp(log_step.astype(F32))[..., None]
    mag = jnp.exp(lr * step)
    a_r = mag * jnp.cos(li * step)
    a_i = mag * jnp.sin(li * step)
    den = lr * lr + li * li
    coef_r = ((a_r - 1) * lr + a_i * li) / den
    coef_i = (a_i * lr - (a_r - 1) * li) / den
    br = b_re.astype(F32)
    bi = b_im.astype(F32)
    bb_r = coef_r[..., None] * br - coef_i[..., None] * bi
    bb_i = coef_r[..., None] * bi + coef_i[..., None] * br
    gpc = S5_LC // S5_STATE
    eye = jnp.eye(gpc, dtype=F32)

    def in_blocks(bb):
        bb = bb.reshape(2, S5_NC, gpc, S5_STATE, S5_GROUP)
        return jnp.einsum('dngpc,gh->dngchp', bb, eye).reshape(2, S5_NC, S5_UC, S5_LC)

    def out_blocks(cc):
        cc = cc.reshape(2, S5_NC, gpc, S5_GROUP, S5_STATE)
        return jnp.einsum('dngcp,gh->dngphc', cc, eye).reshape(2, S5_NC, S5_LC, S5_UC)

    w_in = jnp.concatenate([in_blocks(bb_r), in_blocks(bb_i)], axis=-1).astype(BF16)
    w_out = jnp.concatenate([out_blocks(c_re.astype(F32)), -out_blocks(c_im.astype(F32))], axis=-2).astype(BF16)

    def lanes(a):
        return jnp.broadcast_to(a.reshape(2, S5_NC, 1, S5_LC), (2, S5_NC, B, S5_LC))

    return w_in, w_out, lanes(a_r), lanes(a_i)


HG_LEVELS = tuple(2 ** e for e in range(int(math.log2(HG_C))))


def _hgrn_level_map(rev):
    pos = np.arange(HG_C)
    if rev:
        pos = HG_C - 1 - pos
    pt, ps = pos[:, None], pos[None, :]
    lv = np.full((HG_C, HG_C), -1, np.int32)
    lv[pt == ps] = len(HG_LEVELS)
    for e, m in enumerate(HG_LEVELS):
        x, y = pt // m, ps // m
        lv[(x == y + 1) & (x % 2 == 1)] = e
    return lv


def _hgrn_kernel(q_ref, f_ref, v_ref, lb_ref, lv_ref, o_ref, s_ref, *, rev):
    i = pl.program_id(1)

    @pl.when(i == 0)
    def _():
        s_ref[...] = jnp.zeros_like(s_ref)

    c = HG_C
    fl = f_ref[...]
    lb = lb_ref[...]
    q = _silu(q_ref[...])
    v = v_ref[...]
    k = (1.0 - lb) * jax.nn.sigmoid(-fl)
    log_sig = jnp.minimum(fl, 0.0) - jnp.log1p(jnp.exp(-jnp.abs(fl)))
    lf = log_sig + jnp.log1p(lb * jnp.exp(jnp.minimum(-fl, MAX_EXP_ARG)))

    row = lax.broadcasted_iota(jnp.int32, (c, 1), 0)
    pos = (c - 1 - row) if rev else row

    def from_earlier(x, j):
        return pltpu.roll(x, (c - j) if rev else j, 0)

    def from_later(x, j):
        return pltpu.roll(x, j if rev else (c - j), 0)

    cum = lf
    for j in HG_LEVELS:
        cum = cum + jnp.where(pos >= j, from_earlier(cum, j), 0.0)

    lv = lv_ref[...]
    scores = [jnp.zeros((c, c), F32) for _ in range(HG_HEADS)]
    qb = q.astype(BF16)
    kb = k.astype(BF16)
    for h in range(HG_HEADS):
        sl = slice(h * HG_DIM, (h + 1) * HG_DIM)
        scores[h] = jnp.where(lv == len(HG_LEVELS), _dot_nt(qb[:, sl], kb[:, sl]), scores[h])

    g_end = cum
    for e, m in enumerate(HG_LEVELS):
        e_k = g_end - cum
        e_q = cum - from_earlier(g_end, m)
        qm = (q * jnp.exp(jnp.minimum(e_q, 0.0))).astype(BF16)
        km = (k * jnp.exp(jnp.minimum(e_k, 0.0))).astype(BF16)
        for h in range(HG_HEADS):
            sl = slice(h * HG_DIM, (h + 1) * HG_DIM)
            scores[h] = jnp.where(lv == e, _dot_nt(qm[:, sl], km[:, sl]), scores[h])
        g_end = jnp.where((pos & m) != 0, g_end, from_later(g_end, m))

    q_in = (q * jnp.exp(cum)).astype(BF16)
    k_out = (k * jnp.exp(g_end - cum)).astype(BF16)
    decay = jnp.exp(g_end[0:1, :])
    vb = v.astype(BF16)
    for h in range(HG_HEADS):
        sl = slice(h * HG_DIM, (h + 1) * HG_DIM)
        st = s_ref[h]
        inter = _dot_nt(q_in[:, sl], st.astype(BF16))
        intra = _dot(scores[h].astype(BF16), vb[:, sl])
        o_ref[:, sl] = inter + intra
        s_ref[h] = decay[:, sl] * st + _dot(v[:, sl].T.astype(BF16), k_out[:, sl])


def _hgrn_scan(hg, lb, rev):
    d = 1 if rev else 0
    lvl = jnp.asarray(_hgrn_level_map(rev))
    return pl.pallas_call(
        functools.partial(_hgrn_kernel, rev=rev),
        grid=(B, NT),
        in_specs=[
            pl.BlockSpec((None, HG_C, HG_W), lambda b, i: (b, _seq_tile(d, i), 0)),
            pl.BlockSpec((None, HG_C, HG_W), lambda b, i: (b, _seq_tile(d, i), 1 + d)),
            pl.BlockSpec((None, HG_C, HG_W), lambda b, i: (b, _seq_tile(d, i), 3)),
            _const_spec((1, HG_W)),
            _const_spec((HG_C, HG_C)),
        ],
        out_specs=pl.BlockSpec((None, HG_C, HG_W), lambda b, i: (b, _seq_tile(d, i), 0)),
        out_shape=jax.ShapeDtypeStruct((B, T, HG_W), F32),
        scratch_shapes=[pltpu.VMEM((HG_HEADS, HG_DIM, HG_DIM), F32)],
        compiler_params=_cparams(("arbitrary", "arbitrary")),
        name="hgrn_bwd" if rev else "hgrn_fwd",
    )(hg, hg, hg, lb, lvl)


def _ffn_tail(mix_out, h_ref, mod_ref, g_ref, w_in_ref, w_out_ref, o_ref):
    h1 = h_ref[...] + mod_ref[2:3, :] * mix_out
    a = _norm_mod(h1, g_ref[...], mod_ref[3:4, :], mod_ref[4:5, :])
    gu = _dot(a.astype(BF16), w_in_ref[...])
    act = _silu(gu[:, :FFN_H]) * gu[:, FFN_H:]
    o_ref[...] = h1 + mod_ref[5:6, :] * _dot(act.astype(BF16), w_out_ref[...])


def _gelu_tanh(x):
    return 0.5 * x * (1.0 + jnp.tanh(math.sqrt(2.0 / math.pi) * (x + 0.044715 * (x * x * x))))


def _post_even_kernel(yf_ref, yb_ref, u_ref, of_ref, ob_ref, gate_ref, h_ref, mod_ref, g_ref,
                      dsk_ref, gw_ref, gb_ref, on_ref, wo_ref, w_in_ref, w_out_ref, o_ref):
    y = yf_ref[...] + yb_ref[...] + u_ref[...] * dsk_ref[...]
    z = _gelu_tanh(y)
    s5 = z * jax.nn.sigmoid(_dot(z.astype(BF16), gw_ref[...]) + gb_ref[...])
    mix = _dot(s5.astype(BF16), wo_ref[0:S5_W, :])
    o = of_ref[...] + ob_ref[...]
    gate = _silu(gate_ref[...])
    for h in range(HG_HEADS):
        sl = slice(h * HG_DIM, (h + 1) * HG_DIM)
        oh = o[:, sl]
        ms = jnp.mean(oh * oh, axis=-1, keepdims=True)
        hn = oh * lax.rsqrt(ms + EPS) * on_ref[...] * gate[:, sl]
        mix = mix + _dot(hn.astype(BF16), wo_ref[S5_W + h * HG_DIM:S5_W + (h + 1) * HG_DIM, :])
    _ffn_tail(mix, h_ref, mod_ref, g_ref, w_in_ref, w_out_ref, o_ref)


def _post_even(y, u_tm, o_f, o_b, hg, hs, mod_l, g, dsk, gw, gb, on, wo, w_in, w_out):
    y2 = y.reshape(2, T, B * S5_W)
    tok = lambda b, t: (b, t, 0)
    return pl.pallas_call(
        _post_even_kernel,
        grid=(B, NT),
        in_specs=[
            pl.BlockSpec((None, TM, S5_W), lambda b, t: (0, t, b)),
            pl.BlockSpec((None, TM, S5_W), lambda b, t: (1, t, b)),
            pl.BlockSpec((TM, S5_W), lambda b, t: (t, b)),
            pl.BlockSpec((None, TM, HG_W), tok),
            pl.BlockSpec((None, TM, HG_W), tok),
            pl.BlockSpec((None, TM, HG_W), lambda b, t: (b, t, 4)),
            pl.BlockSpec((None, TM, D), tok),
            _mod_spec(0),
            _const_spec((1, D)),
            _const_spec((1, S5_W)),
            _const_spec((S5_W, S5_W)),
            _const_spec((1, S5_W)),
            _const_spec((1, HG_DIM)),
            _const_spec((D, D)),
            _const_spec((D, 2 * FFN_H)),
            _const_spec((FFN_H, D)),
        ],
        out_specs=pl.BlockSpec((None, TM, D), tok),
        out_shape=jax.ShapeDtypeStruct((B, T, D), F32),
        compiler_params=_cparams(("arbitrary", "arbitrary")),
        name="post_even",
    )(y2, y2, u_tm, o_f, o_b, hg, hs, mod_l, g, dsk, gw, gb, on, wo, w_in, w_out)


def _post_odd_kernel(mx_ref, h_ref, mod_ref, g_ref, wo_ref, w_in_ref, w_out_ref, o_ref):
    _ffn_tail(_dot(mx_ref[...], wo_ref[...]), h_ref, mod_ref, g_ref, w_in_ref, w_out_ref, o_ref)


def _post_odd(mx, hs, mod_l, g, wo, w_in, w_out, latent_only):
    t0 = 1 if latent_only else 0
    nt = NT - t0
    return pl.pallas_call(
        _post_odd_kernel,
        grid=(B, nt),
        in_specs=[
            pl.BlockSpec((None, TM, D), lambda b, t: (b, t, 0)),
            pl.BlockSpec((None, TM, D), lambda b, t: (b, t + t0, 0)),
            _mod_spec(t0),
            _const_spec((1, D)),
            _const_spec((D, D)),
            _const_spec((D, 2 * FFN_H)),
            _const_spec((FFN_H, D)),
        ],
        out_specs=pl.BlockSpec((None, TM, D), lambda b, t: (b, t, 0)),
        out_shape=jax.ShapeDtypeStruct((B, nt * TM, D), F32),
        compiler_params=_cparams(("arbitrary", "arbitrary")),
        name="post_odd",
    )(mx, hs, mod_l, g, wo, w_in, w_out)


def _group_rms(xs, r_ref):
    ms = _dot((xs * xs).astype(BF16), r_ref)
    return xs * lax.rsqrt(ms + EPS)


def _rope(xs, cos, sin_signed, first_half):
    rot = jnp.where(first_half, pltpu.roll(xs, LANE - ROPE_DIM // 4, 1), pltpu.roll(xs, ROPE_DIM // 4, 1))
    return xs * cos + rot * sin_signed


def _odd_in_kernel(x_ref, mod_ref, g_ref, w_ref, rope_ref, gv_ref, qan_ref, kvan_ref, wuq_ref, wukv_ref,
                   rbd_ref, rfull_ref, rhalf_ref,
                   dq_ref, k1_ref, k2_ref, dv_ref, mq_ref, mk_ref, mv_ref, *, mla_scale):
    a = _norm_mod(x_ref[...], g_ref[...], mod_ref[0:1, :], mod_ref[1:2, :])
    p = _dot(a.astype(BF16), w_ref[...])
    cos_d, sin_d, cos_m, sin_m = rope_ref[0], rope_ref[1], rope_ref[2], rope_ref[3]
    lane = lax.broadcasted_iota(jnp.int32, (1, LANE), 1)
    first_half = (lane % (ROPE_DIM // 2)) < (ROPE_DIM // 4)
    low = lane < DIFF_DIM
    rbd, rfull, rhalf = rbd_ref[...], rfull_ref[...], rhalf_ref[...]
    g_dq, g_dk = gv_ref[0:1, :], gv_ref[1:2, :]
    g_qn, g_qr, g_kn, g_kr = gv_ref[2:3, :], gv_ref[3:4, :], gv_ref[4:5, :], gv_ref[5:6, :]
    diff_scale = DIFF_DIM ** -0.5

    for h in range(DIFF_HEADS):
        sl = slice(h * LANE, (h + 1) * LANE)
        qh = _rope(_group_rms(p[:, sl], rbd) * g_dq, cos_d, sin_d, first_half)
        dq_ref[:, sl] = (qh * diff_scale).astype(BF16)
        kh = _rope(_group_rms(p[:, DIFF_W + h * LANE:DIFF_W + (h + 1) * LANE], rbd) * g_dk, cos_d, sin_d, first_half)
        k1_ref[:, sl] = jnp.where(low, kh, 0.0).astype(BF16)
        k2_ref[:, sl] = jnp.where(low, 0.0, kh).astype(BF16)
    dv_ref[...] = p[:, 2 * DIFF_W:3 * DIFF_W].astype(BF16)

    o_cq = 3 * DIFF_W
    o_ckv = o_cq + MLA_Q_RANK
    o_kr = o_ckv + MLA_KV_RANK
    cq = p[:, o_cq:o_ckv]
    cqn = cq * lax.rsqrt(jnp.mean(cq * cq, axis=-1, keepdims=True) + EPS) * qan_ref[...]
    q = _dot(cqn.astype(BF16), wuq_ref[...])
    ckv = p[:, o_ckv:o_kr]
    ckvn = ckv * lax.rsqrt(jnp.mean(ckv * ckv, axis=-1, keepdims=True) + EPS) * kvan_ref[...]
    kv = _dot(ckvn.astype(BF16), wukv_ref[...])
    k_rope = _rope(_group_rms(p[:, o_kr:o_kr + LANE], rhalf) * g_kr, cos_m, sin_m, first_half).astype(BF16)
    for h in range(MLA_HEADS):
        o = 2 * LANE * h
        q_nope = _group_rms(q[:, o:o + LANE], rfull) * g_qn
        q_rope = _rope(_group_rms(q[:, o + LANE:o + 2 * LANE], rhalf) * g_qr, cos_m, sin_m, first_half)
        mq_ref[:, o:o + LANE] = (q_nope * mla_scale).astype(BF16)
        mq_ref[:, o + LANE:o + 2 * LANE] = (q_rope * mla_scale).astype(BF16)
        mk_ref[:, o:o + LANE] = (_group_rms(kv[:, o:o + LANE], rfull) * g_kn).astype(BF16)
        mk_ref[:, o + LANE:o + 2 * LANE] = k_rope
        mv_ref[:, h * LANE:(h + 1) * LANE] = kv[:, o + LANE:o + 2 * LANE].astype(BF16)


def _odd_in(hs, mod_l, g, w, rope, gv, qan, kvan, wuq, wukv, rbd, rfull, rhalf):
    tok = lambda b, t: (b, t, 0)
    sds = lambda n: jax.ShapeDtypeStruct((B, T, n), BF16)
    widths = (DIFF_W, DIFF_W, DIFF_W, DIFF_W, 2 * LANE * MLA_HEADS, 2 * LANE * MLA_HEADS, MLA_HEADS * MLA_V)
    return pl.pallas_call(
        functools.partial(_odd_in_kernel, mla_scale=(MLA_NOPE + MLA_ROPE) ** -0.5),
        grid=(B, NT),
        in_specs=[
            pl.BlockSpec((None, TM, D), tok),
            _mod_spec(0),
            _const_spec((1, D)),
            _const_spec((D, CD_PAD)),
            pl.BlockSpec((4, TM, LANE), lambda b, t: (0, t, 0)),
            _const_spec((SUBLANE, LANE)),
            _const_spec((1, MLA_Q_RANK)),
            _const_spec((1, MLA_KV_RANK)),
            _const_spec((MLA_Q_RANK, 2 * LANE * MLA_HEADS)),
            _const_spec((MLA_KV_RANK, 2 * LANE * MLA_HEADS)),
            _const_spec((LANE, LANE)),
            _const_spec((LANE, LANE)),
            _const_spec((LANE, LANE)),
        ],
        out_specs=[pl.BlockSpec((None, TM, n), tok) for n in widths],
        out_shape=[sds(n) for n in widths],
        compiler_params=_cparams(("arbitrary", "arbitrary")),
        name="odd_in_proj",
    )(hs, mod_l, g, w, rope, gv, qan, kvan, wuq, wukv, rbd, rfull, rhalf)


def _softmax_parts(s):
    e = jnp.exp(s - jnp.max(s, axis=-1, keepdims=True))
    return e, jnp.sum(e, axis=-1, keepdims=True)


def _attn_kernel(lam_ref, dq_ref, k1_ref, k2_ref, dv_ref, mq_ref, mk_ref, mv_ref, sub_ref, o_ref,
                 *, out_scale, t0):
    lam = lam_ref[0, 0]

    def body(kv):
        for h in range(DIFF_HEADS):
            sl = slice(h * LANE, (h + 1) * LANE)
            qh = dq_ref[:, sl]
            e1, l1 = _softmax_parts(_dot_nt(qh, k1_ref[0:kv, sl]))
            e2, l2 = _softmax_parts(_dot_nt(qh, k2_ref[0:kv, sl]))
            pr = e1 * (1.0 / l1) - e2 * (lam / l2)
            o = _dot(pr.astype(BF16), dv_ref[0:kv, sl])
            o = o * lax.rsqrt(jnp.mean(o * o, axis=-1, keepdims=True) + EPS) * sub_ref[...] * out_scale
            o_ref[:, sl] = o.astype(BF16)
        for h in range(MLA_HEADS):
            sq = slice(2 * LANE * h, 2 * LANE * (h + 1))
            e, l = _softmax_parts(_dot_nt(mq_ref[:, sq], mk_ref[0:kv, sq]))
            o = _dot(e.astype(BF16), mv_ref[0:kv, h * LANE:(h + 1) * LANE]) * (1.0 / l)
            o_ref[:, DIFF_W + h * LANE:DIFF_W + (h + 1) * LANE] = o.astype(BF16)

    if t0 == 0:
        t = pl.program_id(1)
        pl.when(t == 0)(lambda: body(CTX))
        pl.when(t > 0)(lambda: body(T))
    else:
        body(T)


def _attention(lam, dq, k1, k2, dv, mq, mk, mv, sub, out_scale, latent_only):
    t0 = 1 if latent_only else 0
    nt = NT - t0
    qspec = lambda n: pl.BlockSpec((None, TM, n), lambda b, t: (b, t + t0, 0))
    kspec = lambda n: pl.BlockSpec((None, T, n), lambda b, t: (b, 0, 0))
    return pl.pallas_call(
        functools.partial(_attn_kernel, out_scale=out_scale, t0=t0),
        grid=(B, nt),
        in_specs=[
            pl.BlockSpec(memory_space=pltpu.SMEM),
            qspec(DIFF_W), kspec(DIFF_W), kspec(DIFF_W), kspec(DIFF_W),
            qspec(2 * LANE * MLA_HEADS), kspec(2 * LANE * MLA_HEADS), kspec(MLA_HEADS * MLA_V),
            _const_spec((1, LANE)),
        ],
        out_specs=pl.BlockSpec((None, TM, D), lambda b, t: (b, t, 0)),
        out_shape=jax.ShapeDtypeStruct((B, nt * TM, D), BF16),
        compiler_params=_cparams(("arbitrary", "arbitrary")),
        name="attention",
    )(lam, dq, k1, k2, dv, mq, mk, mv, sub)


def _rope_tables():
    n_tok = SEQ
    rows = jnp.repeat(jnp.arange(n_tok // GRID_W, dtype=jnp.int32), GRID_W)
    cols = jnp.tile(jnp.arange(GRID_W, dtype=jnp.int32), n_tok // GRID_W)
    n_freq = ROPE_DIM // 4
    inv = jnp.power(ROPE_BASE, -jnp.arange(n_freq, dtype=F32) / n_freq)
    ang_r = rows.astype(F32)[:, None] * inv
    ang_c = cols.astype(F32)[:, None] * inv
    ang = jnp.concatenate([ang_r, ang_r, ang_c, ang_c], axis=-1)
    sign = jnp.where((jnp.arange(ROPE_DIM) % (ROPE_DIM // 2)) < n_freq, -1.0, 1.0).astype(F32)
    cos = jnp.concatenate([jnp.ones((CTX, ROPE_DIM), F32), jnp.cos(ang)], axis=0)
    sin = jnp.concatenate([jnp.zeros((CTX, ROPE_DIM), F32), jnp.sin(ang) * sign], axis=0)
    one, zero = jnp.ones_like(cos), jnp.zeros_like(sin)
    cat = lambda a, b: jnp.concatenate([a, b], axis=-1)
    return jnp.stack([cat(cos, cos), cat(sin, sin), cat(cos, one), cat(sin, zero)])


def _averaging_mats():
    idx = np.arange(LANE)
    same64 = (idx[:, None] // DIFF_DIM) == (idx[None, :] // DIFF_DIM)
    rbd = np.where(same64, 1.0 / DIFF_DIM, 0.0)
    rfull = np.full((LANE, LANE), 1.0 / LANE)
    rhalf = np.where((idx[:, None] < MLA_ROPE) & (idx[None, :] < MLA_ROPE), 1.0 / MLA_ROPE, 0.0)
    return tuple(jnp.asarray(m, BF16) for m in (rbd, rfull, rhalf))


def kernel(x, c, ctx, c_ctx, ada_w, ada_b, norm_mix, norm_ffn, w_out, ffn_w_in, ffn_w_out, ab_w_in, s5_lambda_re, s5_lambda_im, s5_log_step, s5_b_re, s5_b_im, s5_c_re, s5_c_im, s5_d, s5_glu_w, s5_glu_b, hgrn_lb_logits, hgrn_out_norm, cd_w_in, diff_lambda, diff_qk_norm, diff_subln, mla_q_a_norm, mla_kv_a_norm, mla_w_uq, mla_w_ukv, mla_nope_norm, mla_rope_norm):
    assert x.shape == (B, SEQ, D) and ctx.shape == (B, CTX, D)
    mod = _modulation(c, c_ctx, ada_w, ada_b)
    hs = jnp.concatenate([ctx, x], axis=1)
    rope = _rope_tables()
    rbd, rfull, rhalf = _averaging_mats()

    lb_p = jax.nn.softmax(hgrn_lb_logits.astype(F32), axis=0)
    lower_bounds = jnp.cumsum(lb_p, axis=0) - lb_p[0:1]

    for l in range(DEPTH):
        last = l == DEPTH - 1
        g_mix = norm_mix[l].reshape(1, D)
        g_ffn = norm_ffn[l].reshape(1, D)
        wo = w_out[l].astype(BF16)
        w1 = ffn_w_in[l].astype(BF16)
        w2 = ffn_w_out[l].astype(BF16)
        if l % 2 == 0:
            e = l // 2
            u_tm, hg = _even_in(hs, mod[l], g_mix, ab_w_in[e].astype(BF16))
            s5w_in, s5w_out, a_r, a_i = _s5_params(s5_lambda_re[e], s5_lambda_im[e], s5_log_step[e],
                                                   s5_b_re[e], s5_b_im[e], s5_c_re[e], s5_c_im[e])
            y = _s5_scan(u_tm, s5w_in, s5w_out, a_r, a_i)
            lb = lower_bounds[e].reshape(1, HG_W)
            o_f = _hgrn_scan(hg, lb, rev=False)
            o_b = _hgrn_scan(hg, lb, rev=True)
            hs = _post_even(y, u_tm, o_f, o_b, hg, hs, mod[l], g_ffn,
                            s5_d[e].reshape(1, S5_W), s5_glu_w[e].astype(BF16), s5_glu_b[e].reshape(1, S5_W),
                            hgrn_out_norm[e].reshape(1, HG_DIM), wo, w1, w2)
        else:
            o = l // 2
            lam_init = 0.8 - 0.6 * math.exp(-0.3 * l)
            lv = diff_lambda[o].astype(F32)
            lam = (jnp.exp(jnp.sum(lv[0] * lv[1])) - jnp.exp(jnp.sum(lv[2] * lv[3])) + lam_init).reshape(1, 1)
            w_cd = jnp.pad(cd_w_in[o], ((0, 0), (0, CD_PAD - CD_IN))).astype(BF16)
            wuq = mla_w_uq[o].reshape(MLA_Q_RANK, MLA_HEADS, MLA_NOPE + MLA_ROPE)
            wuq = jnp.pad(wuq, ((0, 0), (0, 0), (0, 2 * LANE - MLA_NOPE - MLA_ROPE)))
            wuq = wuq.reshape(MLA_Q_RANK, 2 * LANE * MLA_HEADS).astype(BF16)
            pad_r = lambda v: jnp.pad(v, (0, LANE - MLA_ROPE))
            gv = jnp.zeros((SUBLANE, LANE), F32)
            gv = gv.at[0].set(jnp.tile(diff_qk_norm[o, 0], 2)).at[1].set(jnp.tile(diff_qk_norm[o, 1], 2))
            gv = gv.at[2].set(mla_nope_norm[o, 0]).at[3].set(pad_r(mla_rope_norm[o, 0]))
            gv = gv.at[4].set(mla_nope_norm[o, 1]).at[5].set(pad_r(mla_rope_norm[o, 1]))
            parts = _odd_in(hs, mod[l], g_mix, w_cd, rope, gv, mla_q_a_norm[o].reshape(1, MLA_Q_RANK),
                            mla_kv_a_norm[o].reshape(1, MLA_KV_RANK), wuq, mla_w_ukv[o].astype(BF16),
                            rbd, rfull, rhalf)
            mx = _attention(lam, *parts, diff_subln[o].reshape(1, LANE), 1.0 - lam_init, latent_only=last)
            hs = _post_odd(mx, hs, mod[l], g_ffn, wo, w1, w2, latent_only=last)
    return hs
```

```python
import functools
import math

import numpy as np
import jax
import jax.numpy as jnp
from jax import lax
from jax.experimental import pallas as pl
from jax.experimental.pallas import tpu as pltpu

F32 = jnp.float32
BF16 = jnp.bfloat16

D = 1024
B = 8
SEQ = 2048
CTX = 256
T = CTX + SEQ
DEPTH = 4
GRID_W = 64
FFN_H = ((8 * D + 3 * 256 - 1) // (3 * 256)) * 256
S5_W = D // 2
S5_GROUP = 16
S5_GROUPS = S5_W // S5_GROUP
S5_STATE = 64
HG_HEADS = 4
HG_DIM = D // 8
HG_W = HG_HEADS * HG_DIM
MAX_EXP_ARG = 60.0
DIFF_HEADS = 4
DIFF_DIM = D // 16
DIFF_W = DIFF_HEADS * 2 * DIFF_DIM
MLA_HEADS = 4
MLA_NOPE = D // 8
MLA_ROPE = D // 16
MLA_V = D // 8
MLA_Q_RANK = 3 * D // 8
MLA_KV_RANK = D // 4
ROPE_DIM = D // 16
ROPE_BASE = 10000.0
EPS = 1e-6
AB_IN = S5_W + 5 * HG_W
CD_IN = 3 * DIFF_W + MLA_Q_RANK + MLA_KV_RANK + MLA_ROPE
CD_PAD = CD_IN + 64

LANE = 128
SUBLANE = 8
TM = 256
NT = T // TM
MOD_ROWS = 16
CTX_ROW = B
S5_LC = 512
S5_NC = S5_GROUPS * S5_STATE // S5_LC
S5_UC = S5_LC // S5_STATE * S5_GROUP
S5_SB = 32
HG_C = TM
VMEM_LIMIT = 56 * 1024 * 1024
LOG2E = math.log2(math.e)


def _cparams(sem):
    return pltpu.CompilerParams(dimension_semantics=sem, vmem_limit_bytes=VMEM_LIMIT)


def _const_spec(shape):
    n = len(shape)
    return pl.BlockSpec(shape, lambda *_: (0,) * n, pipeline_mode=pl.Buffered(1))


def _silu(x):
    return x * jax.nn.sigmoid(x)


def _norm_mod(x, g, shift, scale):
    ms = jnp.mean(x * x, axis=-1, keepdims=True)
    return x * lax.rsqrt(ms + EPS) * g * (1.0 + scale) + shift


def _dot(a, b):
    return jnp.dot(a, b, preferred_element_type=F32)


def _dot_nt(a, b):
    return lax.dot_general(a, b, (((1,), (1,)), ((), ())), preferred_element_type=F32)


def _mod_kernel(s_ref, w_ref, b_ref, o_ref):
    s = _silu(s_ref[...])
    o_ref[...] = _dot(s.astype(BF16), w_ref[...].astype(BF16)) + b_ref[...]


def _modulation(c, c_ctx, ada_w, ada_b):
    s = jnp.zeros((MOD_ROWS, D), F32).at[:B].set(c).at[CTX_ROW].set(c_ctx)
    nb = 1536
    out = pl.pallas_call(
        _mod_kernel,
        grid=(DEPTH, 6 * D // nb),
        in_specs=[
            pl.BlockSpec((MOD_ROWS, D), lambda l, n: (0, 0)),
            pl.BlockSpec((None, D, nb), lambda l, n: (l, 0, n)),
            pl.BlockSpec((None, 1, nb), lambda l, n: (l, 0, n)),
        ],
        out_specs=pl.BlockSpec((None, MOD_ROWS, nb), lambda l, n: (l, 0, n)),
        out_shape=jax.ShapeDtypeStruct((DEPTH, MOD_ROWS, 6 * D), F32),
        compiler_params=_cparams(("arbitrary", "arbitrary")),
        name="adaln_mod",
    )(s, ada_w, ada_b.reshape(DEPTH, 1, 6 * D))
    return out.reshape(DEPTH, MOD_ROWS, 6, D)


def _mod_spec(t0):
    return pl.BlockSpec((None, 6, D), lambda b, t: (jnp.where(t + t0 == 0, CTX_ROW, b), 0, 0))


def _even_in_kernel(x_ref, mod_ref, g_ref, w_ref, p_ref):
    a = _norm_mod(x_ref[...], g_ref[...], mod_ref[0:1, :], mod_ref[1:2, :])
    p_ref[...] = _dot(a.astype(BF16), w_ref[...])


def _even_in(hs, mod_l, g, w):
    return pl.pallas_call(
        _even_in_kernel,
        grid=(B, NT),
        in_specs=[
            pl.BlockSpec((None, TM, D), lambda b, t: (b, t, 0)),
            _mod_spec(0),
            _const_spec((1, D)),
            _const_spec((D, AB_IN)),
        ],
        out_specs=pl.BlockSpec((None, TM, AB_IN), lambda b, t: (b, t, 0)),
        out_shape=jax.ShapeDtypeStruct((B, T, AB_IN), F32),
        compiler_params=_cparams(("arbitrary", "arbitrary")),
        name="even_in_proj",
    )(hs, mod_l, g, w)


def _s5_kernel(u_ref, w_ref, c_ref, ar_ref, ai_ref, y_ref, uy_ref, bu_ref, h_ref, *, rev):
    i = pl.program_id(1)

    @pl.when(i == 0)
    def _():
        h_ref[...] = jnp.zeros_like(h_ref)

    ar = ar_ref[...]
    ai = ai_ref[...]
    hr, hi = h_ref[0], h_ref[1]
    n_sb = TM // S5_SB
    for sb in (reversed(range(n_sb)) if rev else range(n_sb)):
        tsl = slice(sb * S5_SB, (sb + 1) * S5_SB)
        rows = slice(sb * S5_SB * B, (sb + 1) * S5_SB * B)
        for b in range(B):
            uy_ref[pl.ds(rows.start + b, S5_SB, stride=B), :] = u_ref[b, tsl, :]
        bu_ref[rows, :] = _dot(uy_ref[rows, :].astype(BF16), w_ref[...])
        for t in (reversed(range(S5_SB)) if rev else range(S5_SB)):
            r = slice(rows.start + t * B, rows.start + (t + 1) * B)
            nr = ar * hr - ai * hi + bu_ref[r, 0:S5_LC]
            ni = ar * hi + ai * hr + bu_ref[r, S5_LC:2 * S5_LC]
            bu_ref[r, 0:S5_LC] = nr
            bu_ref[r, S5_LC:2 * S5_LC] = ni
            hr, hi = nr, ni
        uy_ref[rows, :] = _dot(bu_ref[rows, :].astype(BF16), c_ref[...])
        for b in range(B):
            y_ref[b, tsl, :] = uy_ref[pl.ds(rows.start + b, S5_SB, stride=B), :]
    h_ref[0] = hr
    h_ref[1] = hi


def _seq_tile(d, i):
    return jnp.where(d == 0, i, jnp.where(i == 0, 0, NT - i))


def _s5_scan(proj, w_in, w_out, a_r, a_i, rev):
    d = 1 if rev else 0
    return pl.pallas_call(
        functools.partial(_s5_kernel, rev=rev),
        grid=(S5_NC, NT),
        in_specs=[
            pl.BlockSpec((B, TM, S5_UC), lambda c, i: (0, _seq_tile(d, i), c)),
            pl.BlockSpec((None, None, S5_UC, 2 * S5_LC), lambda c, i: (d, c, 0, 0)),
            pl.BlockSpec((None, None, 2 * S5_LC, S5_UC), lambda c, i: (d, c, 0, 0)),
            pl.BlockSpec((None, None, B, S5_LC), lambda c, i: (d, c, 0, 0)),
            pl.BlockSpec((None, None, B, S5_LC), lambda c, i: (d, c, 0, 0)),
        ],
        out_specs=pl.BlockSpec((B, TM, S5_UC), lambda c, i: (0, _seq_tile(d, i), c)),
        out_shape=jax.ShapeDtypeStruct((B, T, S5_W), F32),
        scratch_shapes=[
            pltpu.VMEM((TM * B, S5_UC), F32),
            pltpu.VMEM((TM * B, 2 * S5_LC), F32),
            pltpu.VMEM((2, B, S5_LC), F32),
        ],
        compiler_params=_cparams(("arbitrary", "arbitrary")),
        name="s5_bwd" if rev else "s5_fwd",
    )(proj, w_in, w_out, a_r, a_i)


def _s5_params(lam_re, lam_im, log_step, b_re, b_im, c_re, c_im):
    lr = jnp.minimum(lam_re.astype(F32), -1e-4)
    li = lam_im.astype(F32)
    step = jnp.exp(log_step.astype(F32))[..., None]
    mag = jnp.exp(lr * step)
    a_r = mag * jnp.cos(li * step)
    a_i = mag * jnp.sin(li * step)
    den = lr * lr + li * li
    coef_r = ((a_r - 1) * lr + a_i * li) / den
    coef_i = (a_i * lr - (a_r - 1) * li) / den
    br = b_re.astype(F32)
    bi = b_im.astype(F32)
    bb_r = coef_r[..., None] * br - coef_i[..., None] * bi
    bb_i = coef_r[..., None] * bi + coef_i[..., None] * br
    gpc = S5_LC // S5_STATE
    eye = jnp.eye(gpc, dtype=F32)

    def in_blocks(bb):
        bb = bb.reshape(2, S5_NC, gpc, S5_STATE, S5_GROUP)
        return jnp.einsum('dngpc,gh->dngchp', bb, eye).reshape(2, S5_NC, S5_UC, S5_LC)

    def out_blocks(cc):
        cc = cc.reshape(2, S5_NC, gpc, S5_GROUP, S5_STATE)
        return jnp.einsum('dngcp,gh->dngphc', cc, eye).reshape(2, S5_NC, S5_LC, S5_UC)

    w_in = jnp.concatenate([in_blocks(bb_r), in_blocks(bb_i)], axis=-1).astype(BF16)
    w_out = jnp.concatenate([out_blocks(c_re.astype(F32)), -out_blocks(c_im.astype(F32))], axis=-2).astype(BF16)

    def lanes(a):
        return jnp.broadcast_to(a.reshape(2, S5_NC, 1, S5_LC), (2, S5_NC, B, S5_LC))

    return w_in, w_out, lanes(a_r), lanes(a_i)


HG_LEVELS = tuple(2 ** e for e in range(int(math.log2(HG_C))))


def _hgrn_level_map(rev):
    pos = np.arange(HG_C)
    if rev:
        pos = HG_C - 1 - pos
    pt, ps = pos[:, None], pos[None, :]
    lv = np.full((HG_C, HG_C), -1, np.int32)
    lv[pt == ps] = len(HG_LEVELS)
    for e, m in enumerate(HG_LEVELS):
        x, y = pt // m, ps // m
        lv[(x == y + 1) & (x % 2 == 1)] = e
    return lv


def _hgrn_kernel(q_ref, f_ref, v_ref, lb_ref, lv_ref, o_ref, s_ref, *, rev):
    i = pl.program_id(1)

    @pl.when(i == 0)
    def _():
        s_ref[...] = jnp.zeros_like(s_ref)

    c = HG_C
    fl = f_ref[...]
    lb = lb_ref[...]
    q = _silu(q_ref[...])
    v = v_ref[...]
    k = (1.0 - lb) * jax.nn.sigmoid(-fl)
    log_sig = jnp.minimum(fl, 0.0) - jnp.log(1.0 + jnp.exp(-jnp.abs(fl)))
    lf = (log_sig + jnp.log(1.0 + lb * jnp.exp(jnp.minimum(-fl, MAX_EXP_ARG)))) * LOG2E

    row = lax.broadcasted_iota(jnp.int32, (c, 1), 0)
    pos = (c - 1 - row) if rev else row

    def from_earlier(x, j):
        return pltpu.roll(x, (c - j) if rev else j, 0)

    def from_later(x, j):
        return pltpu.roll(x, j if rev else (c - j), 0)

    cum = lf
    for j in HG_LEVELS:
        cum = cum + jnp.where(pos >= j, from_earlier(cum, j), 0.0)

    lv = lv_ref[...]
    scores = [jnp.zeros((c, c), F32) for _ in range(HG_HEADS)]
    qb = q.astype(BF16)
    kb = k.astype(BF16)
    for h in range(HG_HEADS):
        sl = slice(h * HG_DIM, (h + 1) * HG_DIM)
        scores[h] = jnp.where(lv == len(HG_LEVELS), _dot_nt(qb[:, sl], kb[:, sl]), scores[h])

    g_end = cum
    for e, m in enumerate(HG_LEVELS):
        e_k = g_end - cum
        e_q = cum - from_earlier(g_end, m)
        qm = (q * jnp.exp2(jnp.minimum(e_q, 0.0))).astype(BF16)
        km = (k * jnp.exp2(e_k)).astype(BF16)
        for h in range(HG_HEADS):
            sl = slice(h * HG_DIM, (h + 1) * HG_DIM)
            scores[h] = jnp.where(lv == e, _dot_nt(qm[:, sl], km[:, sl]), scores[h])
        g_end = jnp.where((pos & m) != 0, g_end, from_later(g_end, m))

    q_in = (q * jnp.exp2(cum)).astype(BF16)
    k_out = (k * jnp.exp2(g_end - cum)).astype(BF16)
    decay = jnp.exp2(g_end[0:1, :])
    vb = v.astype(BF16)
    for h in range(HG_HEADS):
        sl = slice(h * HG_DIM, (h + 1) * HG_DIM)
        st = s_ref[h]
        inter = _dot_nt(q_in[:, sl], st.astype(BF16))
        intra = _dot(scores[h].astype(BF16), vb[:, sl])
        o_ref[:, sl] = inter + intra
        s_ref[h] = decay[:, sl] * st + _dot(v[:, sl].T.astype(BF16), k_out[:, sl])


def _hgrn_scan(hg, lb, rev):
    d = 1 if rev else 0
    lvl = jnp.asarray(_hgrn_level_map(rev))
    return pl.pallas_call(
        functools.partial(_hgrn_kernel, rev=rev),
        grid=(B, NT),
        in_specs=[
            pl.BlockSpec((None, HG_C, HG_W), lambda b, i: (b, _seq_tile(d, i), 1)),
            pl.BlockSpec((None, HG_C, HG_W), lambda b, i: (b, _seq_tile(d, i), 2 + d)),
            pl.BlockSpec((None, HG_C, HG_W), lambda b, i: (b, _seq_tile(d, i), 4)),
            _const_spec((1, HG_W)),
            _const_spec((HG_C, HG_C)),
        ],
        out_specs=pl.BlockSpec((None, HG_C, HG_W), lambda b, i: (b, _seq_tile(d, i), 0)),
        out_shape=jax.ShapeDtypeStruct((B, T, HG_W), F32),
        scratch_shapes=[pltpu.VMEM((HG_HEADS, HG_DIM, HG_DIM), F32)],
        compiler_params=_cparams(("arbitrary", "arbitrary")),
        name="hgrn_bwd" if rev else "hgrn_fwd",
    )(hg, hg, hg, lb, lvl)


def _ffn_tail(mix_out, h_ref, mod_ref, g_ref, w_in_ref, w_out_ref, o_ref):
    h1 = h_ref[...] + mod_ref[2:3, :] * mix_out
    a = _norm_mod(h1, g_ref[...], mod_ref[3:4, :], mod_ref[4:5, :])
    gu = _dot(a.astype(BF16), w_in_ref[...])
    act = _silu(gu[:, :FFN_H]) * gu[:, FFN_H:]
    o_ref[...] = h1 + mod_ref[5:6, :] * _dot(act.astype(BF16), w_out_ref[...])


def _gelu_tanh(x):
    return 0.5 * x * (1.0 + jnp.tanh(math.sqrt(2.0 / math.pi) * (x + 0.044715 * (x * x * x))))


def _post_even_kernel(yf_ref, yb_ref, u_ref, of_ref, ob_ref, gate_ref, h_ref, mod_ref, g_ref,
                      dsk_ref, gw_ref, gb_ref, on_ref, wo_ref, w_in_ref, w_out_ref, o_ref):
    y = yf_ref[...] + yb_ref[...] + u_ref[...] * dsk_ref[...]
    z = _gelu_tanh(y)
    s5 = z * jax.nn.sigmoid(_dot(z.astype(BF16), gw_ref[...]) + gb_ref[...])
    mix = _dot(s5.astype(BF16), wo_ref[0:S5_W, :])
    o = of_ref[...] + ob_ref[...]
    gate = _silu(gate_ref[...])
    for h in range(HG_HEADS):
        sl = slice(h * HG_DIM, (h + 1) * HG_DIM)
        oh = o[:, sl]
        ms = jnp.mean(oh * oh, axis=-1, keepdims=True)
        hn = oh * lax.rsqrt(ms + EPS) * on_ref[...] * gate[:, sl]
        mix = mix + _dot(hn.astype(BF16), wo_ref[S5_W + h * HG_DIM:S5_W + (h + 1) * HG_DIM, :])
    _ffn_tail(mix, h_ref, mod_ref, g_ref, w_in_ref, w_out_ref, o_ref)


def _post_even(y_f, y_b, o_f, o_b, proj, hs, mod_l, g, dsk, gw, gb, on, wo, w_in, w_out):
    tok = lambda b, t: (b, t, 0)
    return pl.pallas_call(
        _post_even_kernel,
        grid=(B, NT),
        in_specs=[
            pl.BlockSpec((None, TM, S5_W), tok),
            pl.BlockSpec((None, TM, S5_W), tok),
            pl.BlockSpec((None, TM, S5_W), tok),
            pl.BlockSpec((None, TM, HG_W), tok),
            pl.BlockSpec((None, TM, HG_W), tok),
            pl.BlockSpec((None, TM, HG_W), lambda b, t: (b, t, 5)),
            pl.BlockSpec((None, TM, D), tok),
            _mod_spec(0),
            _const_spec((1, D)),
            _const_spec((1, S5_W)),
            _const_spec((S5_W, S5_W)),
            _const_spec((1, S5_W)),
            _const_spec((1, HG_DIM)),
            _const_spec((D, D)),
            _const_spec((D, 2 * FFN_H)),
            _const_spec((FFN_H, D)),
        ],
        out_specs=pl.BlockSpec((None, TM, D), tok),
        out_shape=jax.ShapeDtypeStruct((B, T, D), F32),
        compiler_params=_cparams(("arbitrary", "arbitrary")),
        name="post_even",
    )(y_f, y_b, proj, o_f, o_b, proj, hs, mod_l, g, dsk, gw, gb, on, wo, w_in, w_out)


def _post_odd_kernel(mx_ref, h_ref, mod_ref, g_ref, wo_ref, w_in_ref, w_out_ref, o_ref):
    _ffn_tail(_dot(mx_ref[...], wo_ref[...]), h_ref, mod_ref, g_ref, w_in_ref, w_out_ref, o_ref)


def _post_odd(mx, hs, mod_l, g, wo, w_in, w_out, latent_only):
    t0 = 1 if latent_only else 0
    nt = NT - t0
    return pl.pallas_call(
        _post_odd_kernel,
        grid=(B, nt),
        in_specs=[
            pl.BlockSpec((None, TM, D), lambda b, t: (b, t, 0)),
            pl.BlockSpec((None, TM, D), lambda b, t: (b, t + t0, 0)),
            _mod_spec(t0),
            _const_spec((1, D)),
            _const_spec((D, D)),
            _const_spec((D, 2 * FFN_H)),
            _const_spec((FFN_H, D)),
        ],
        out_specs=pl.BlockSpec((None, TM, D), lambda b, t: (b, t, 0)),
        out_shape=jax.ShapeDtypeStruct((B, nt * TM, D), F32),
        compiler_params=_cparams(("arbitrary", "arbitrary")),
        name="post_odd",
    )(mx, hs, mod_l, g, wo, w_in, w_out)


def _group_rms(xs, r_ref):
    ms = _dot((xs * xs).astype(BF16), r_ref)
    return xs * lax.rsqrt(ms + EPS)


def _rope(xs, cos, sin_signed, first_half):
    rot = jnp.where(first_half, pltpu.roll(xs, LANE - ROPE_DIM // 4, 1), pltpu.roll(xs, ROPE_DIM // 4, 1))
    return xs * cos + rot * sin_signed


def _odd_in_kernel(x_ref, mod_ref, g_ref, w_ref, rope_ref, gv_ref, qan_ref, kvan_ref, wuq_ref, wukv_ref,
                   rbd_ref, rfull_ref, rhalf_ref,
                   dq_ref, k1_ref, k2_ref, dv_ref, mq_ref, mk_ref, mv_ref, *, mla_scale):
    a = _norm_mod(x_ref[...], g_ref[...], mod_ref[0:1, :], mod_ref[1:2, :])
    p = _dot(a.astype(BF16), w_ref[...])
    cos_d, sin_d, cos_m, sin_m = rope_ref[0], rope_ref[1], rope_ref[2], rope_ref[3]
    lane = lax.broadcasted_iota(jnp.int32, (1, LANE), 1)
    first_half = (lane % (ROPE_DIM // 2)) < (ROPE_DIM // 4)
    low = lane < DIFF_DIM
    rbd, rfull, rhalf = rbd_ref[...], rfull_ref[...], rhalf_ref[...]
    g_dq, g_dk = gv_ref[0:1, :], gv_ref[1:2, :]
    g_qn, g_qr, g_kn, g_kr = gv_ref[2:3, :], gv_ref[3:4, :], gv_ref[4:5, :], gv_ref[5:6, :]
    diff_scale = DIFF_DIM ** -0.5 * LOG2E
    ones = jnp.ones((TM, LANE), BF16)

    for h in range(DIFF_HEADS):
        sl = slice(h * LANE, (h + 1) * LANE)
        qh = _rope(_group_rms(p[:, sl], rbd) * g_dq, cos_d, sin_d, first_half)
        dq_ref[:, sl] = (qh * diff_scale).astype(BF16)
        kh = _rope(_group_rms(p[:, DIFF_W + h * LANE:DIFF_W + (h + 1) * LANE], rbd) * g_dk, cos_d, sin_d, first_half)
        k1_ref[:, sl] = jnp.where(low, kh, 0.0).astype(BF16)
        k2_ref[:, sl] = jnp.where(low, 0.0, kh).astype(BF16)
        dv_ref[:, 2 * h * LANE:(2 * h + 1) * LANE] = p[:, 2 * DIFF_W + h * LANE:2 * DIFF_W + (h + 1) * LANE].astype(BF16)
        dv_ref[:, (2 * h + 1) * LANE:(2 * h + 2) * LANE] = ones

    o_cq = 3 * DIFF_W
    o_ckv = o_cq + MLA_Q_RANK
    o_kr = o_ckv + MLA_KV_RANK
    cq = p[:, o_cq:o_ckv]
    cqn = cq * lax.rsqrt(jnp.mean(cq * cq, axis=-1, keepdims=True) + EPS) * qan_ref[...]
    q = _dot(cqn.astype(BF16), wuq_ref[...])
    ckv = p[:, o_ckv:o_kr]
    ckvn = ckv * lax.rsqrt(jnp.mean(ckv * ckv, axis=-1, keepdims=True) + EPS) * kvan_ref[...]
    kv = _dot(ckvn.astype(BF16), wukv_ref[...])
    k_rope = _rope(_group_rms(p[:, o_kr:o_kr + LANE], rhalf) * g_kr, cos_m, sin_m, first_half).astype(BF16)
    for h in range(MLA_HEADS):
        o = 2 * LANE * h
        q_nope = _group_rms(q[:, o:o + LANE], rfull) * g_qn
        q_rope = _rope(_group_rms(q[:, o + LANE:o + 2 * LANE], rhalf) * g_qr, cos_m, sin_m, first_half)
        mq_ref[:, o:o + LANE] = (q_nope * mla_scale).astype(BF16)
        mq_ref[:, o + LANE:o + 2 * LANE] = (q_rope * mla_scale).astype(BF16)
        mk_ref[:, o:o + LANE] = (_group_rms(kv[:, o:o + LANE], rfull) * g_kn).astype(BF16)
        mk_ref[:, o + LANE:o + 2 * LANE] = k_rope
        mv_ref[:, o:o + LANE] = kv[:, o + LANE:o + 2 * LANE].astype(BF16)
        mv_ref[:, o + LANE:o + 2 * LANE] = ones


def _odd_in(hs, mod_l, g, w, rope, gv, qan, kvan, wuq, wukv, rbd, rfull, rhalf):
    tok = lambda b, t: (b, t, 0)
    sds = lambda n: jax.ShapeDtypeStruct((B, T, n), BF16)
    wide = 2 * LANE * MLA_HEADS
    widths = (DIFF_W, DIFF_W, DIFF_W, wide, wide, wide, wide)
    return pl.pallas_call(
        functools.partial(_odd_in_kernel, mla_scale=(MLA_NOPE + MLA_ROPE) ** -0.5 * LOG2E),
        grid=(B, NT),
        in_specs=[
            pl.BlockSpec((None, TM, D), tok),
            _mod_spec(0),
            _const_spec((1, D)),
            _const_spec((D, CD_PAD)),
            pl.BlockSpec((4, TM, LANE), lambda b, t: (0, t, 0)),
            _const_spec((SUBLANE, LANE)),
            _const_spec((1, MLA_Q_RANK)),
            _const_spec((1, MLA_KV_RANK)),
            _const_spec((MLA_Q_RANK, 2 * LANE * MLA_HEADS)),
            _const_spec((MLA_KV_RANK, 2 * LANE * MLA_HEADS)),
            _const_spec((LANE, LANE)),
            _const_spec((LANE, LANE)),
            _const_spec((LANE, LANE)),
        ],
        out_specs=[pl.BlockSpec((None, TM, n), tok) for n in widths],
        out_shape=[sds(n) for n in widths],
        compiler_params=_cparams(("arbitrary", "arbitrary")),
        name="odd_in_proj",
    )(hs, mod_l, g, w, rope, gv, qan, kvan, wuq, wukv, rbd, rfull, rhalf)


def _exp2_shifted(s):
    return jnp.exp2(s - jnp.max(s, axis=-1, keepdims=True)).astype(BF16)


def _pv_normalised(e, v_aug):
    o = _dot(e, v_aug)
    return o[:, :LANE] / o[:, LANE:]


def _attn_kernel(lam_ref, dq_ref, k1_ref, k2_ref, dv_ref, mq_ref, mk_ref, mv_ref, sub_ref, o_ref,
                 *, out_scale, t0):
    lam = lam_ref[0, 0]

    def body(kv):
        for h in range(DIFF_HEADS):
            sl = slice(h * LANE, (h + 1) * LANE)
            sv = slice(2 * LANE * h, 2 * LANE * (h + 1))
            qh = dq_ref[:, sl]
            o1 = _pv_normalised(_exp2_shifted(_dot_nt(qh, k1_ref[0:kv, sl])), dv_ref[0:kv, sv])
            o2 = _pv_normalised(_exp2_shifted(_dot_nt(qh, k2_ref[0:kv, sl])), dv_ref[0:kv, sv])
            o = o1 - lam * o2
            o = o * lax.rsqrt(jnp.mean(o * o, axis=-1, keepdims=True) + EPS) * sub_ref[...] * out_scale
            o_ref[:, sl] = o.astype(BF16)
        for h in range(MLA_HEADS):
            sq = slice(2 * LANE * h, 2 * LANE * (h + 1))
            o = _pv_normalised(_exp2_shifted(_dot_nt(mq_ref[:, sq], mk_ref[0:kv, sq])), mv_ref[0:kv, sq])
            o_ref[:, DIFF_W + h * LANE:DIFF_W + (h + 1) * LANE] = o.astype(BF16)

    if t0 == 0:
        t = pl.program_id(1)
        pl.when(t == 0)(lambda: body(CTX))
        pl.when(t > 0)(lambda: body(T))
    else:
        body(T)


def _attention(lam, dq, k1, k2, dv, mq, mk, mv, sub, out_scale, latent_only):
    t0 = 1 if latent_only else 0
    nt = NT - t0
    qspec = lambda n: pl.BlockSpec((None, TM, n), lambda b, t: (b, t + t0, 0))
    kspec = lambda n: pl.BlockSpec((None, T, n), lambda b, t: (b, 0, 0))
    return pl.pallas_call(
        functools.partial(_attn_kernel, out_scale=out_scale, t0=t0),
        grid=(B, nt),
        in_specs=[
            pl.BlockSpec(memory_space=pltpu.SMEM),
            qspec(DIFF_W), kspec(DIFF_W), kspec(DIFF_W), kspec(2 * LANE * DIFF_HEADS),
            qspec(2 * LANE * MLA_HEADS), kspec(2 * LANE * MLA_HEADS), kspec(2 * LANE * MLA_HEADS),
            _const_spec((1, LANE)),
        ],
        out_specs=pl.BlockSpec((None, TM, D), lambda b, t: (b, t, 0)),
        out_shape=jax.ShapeDtypeStruct((B, nt * TM, D), BF16),
        compiler_params=_cparams(("arbitrary", "arbitrary")),
        name="attention",
    )(lam, dq, k1, k2, dv, mq, mk, mv, sub)


def _rope_tables():
    n_tok = SEQ
    rows = jnp.repeat(jnp.arange(n_tok // GRID_W, dtype=jnp.int32), GRID_W)
    cols = jnp.tile(jnp.arange(GRID_W, dtype=jnp.int32), n_tok // GRID_W)
    n_freq = ROPE_DIM // 4
    inv = jnp.power(ROPE_BASE, -jnp.arange(n_freq, dtype=F32) / n_freq)
    ang_r = rows.astype(F32)[:, None] * inv
    ang_c = cols.astype(F32)[:, None] * inv
    ang = jnp.concatenate([ang_r, ang_r, ang_c, ang_c], axis=-1)
    sign = jnp.where((jnp.arange(ROPE_DIM) % (ROPE_DIM // 2)) < n_freq, -1.0, 1.0).astype(F32)
    cos = jnp.concatenate([jnp.ones((CTX, ROPE_DIM), F32), jnp.cos(ang)], axis=0)
    sin = jnp.concatenate([jnp.zeros((CTX, ROPE_DIM), F32), jnp.sin(ang) * sign], axis=0)
    one, zero = jnp.ones_like(cos), jnp.zeros_like(sin)
    cat = lambda a, b: jnp.concatenate([a, b], axis=-1)
    return jnp.stack([cat(cos, cos), cat(sin, sin), cat(cos, one), cat(sin, zero)])


def _averaging_mats():
    idx = np.arange(LANE)
    same64 = (idx[:, None] // DIFF_DIM) == (idx[None, :] // DIFF_DIM)
    rbd = np.where(same64, 1.0 / DIFF_DIM, 0.0)
    rfull = np.full((LANE, LANE), 1.0 / LANE)
    rhalf = np.where((idx[:, None] < MLA_ROPE) & (idx[None, :] < MLA_ROPE), 1.0 / MLA_ROPE, 0.0)
    return tuple(jnp.asarray(m, BF16) for m in (rbd, rfull, rhalf))


def kernel(x, c, ctx, c_ctx, ada_w, ada_b, norm_mix, norm_ffn, w_out, ffn_w_in, ffn_w_out, ab_w_in, s5_lambda_re, s5_lambda_im, s5_log_step, s5_b_re, s5_b_im, s5_c_re, s5_c_im, s5_d, s5_glu_w, s5_glu_b, hgrn_lb_logits, hgrn_out_norm, cd_w_in, diff_lambda, diff_qk_norm, diff_subln, mla_q_a_norm, mla_kv_a_norm, mla_w_uq, mla_w_ukv, mla_nope_norm, mla_rope_norm):
    assert x.shape == (B, SEQ, D) and ctx.shape == (B, CTX, D)
    mod = _modulation(c, c_ctx, ada_w, ada_b)
    hs = jnp.concatenate([ctx, x], axis=1)
    rope = _rope_tables()
    rbd, rfull, rhalf = _averaging_mats()

    lb_p = jax.nn.softmax(hgrn_lb_logits.astype(F32), axis=0)
    lower_bounds = jnp.cumsum(lb_p, axis=0) - lb_p[0:1]

    for l in range(DEPTH):
        last = l == DEPTH - 1
        g_mix = norm_mix[l].reshape(1, D)
        g_ffn = norm_ffn[l].reshape(1, D)
        wo = w_out[l].astype(BF16)
        w1 = ffn_w_in[l].astype(BF16)
        w2 = ffn_w_out[l].astype(BF16)
        if l % 2 == 0:
            e = l // 2
            proj = _even_in(hs, mod[l], g_mix, ab_w_in[e].astype(BF16))
            s5w_in, s5w_out, a_r, a_i = _s5_params(s5_lambda_re[e], s5_lambda_im[e], s5_log_step[e],
                                                   s5_b_re[e], s5_b_im[e], s5_c_re[e], s5_c_im[e])
            y_f = _s5_scan(proj, s5w_in, s5w_out, a_r, a_i, rev=False)
            y_b = _s5_scan(proj, s5w_in, s5w_out, a_r, a_i, rev=True)
            lb = lower_bounds[e].reshape(1, HG_W)
            o_f = _hgrn_scan(proj, lb, rev=False)
            o_b = _hgrn_scan(proj, lb, rev=True)
            hs = _post_even(y_f, y_b, o_f, o_b, proj, hs, mod[l], g_ffn,
                            s5_d[e].reshape(1, S5_W), s5_glu_w[e].astype(BF16), s5_glu_b[e].reshape(1, S5_W),
                            hgrn_out_norm[e].reshape(1, HG_DIM), wo, w1, w2)
        else:
            o = l // 2
            lam_init = 0.8 - 0.6 * math.exp(-0.3 * l)
            lv = diff_lambda[o].astype(F32)
            lam = (jnp.exp(jnp.sum(lv[0] * lv[1])) - jnp.exp(jnp.sum(lv[2] * lv[3])) + lam_init).reshape(1, 1)
            w_cd = jnp.pad(cd_w_in[o], ((0, 0), (0, CD_PAD - CD_IN))).astype(BF16)
            wuq = mla_w_uq[o].reshape(MLA_Q_RANK, MLA_HEADS, MLA_NOPE + MLA_ROPE)
            wuq = jnp.pad(wuq, ((0, 0), (0, 0), (0, 2 * LANE - MLA_NOPE - MLA_ROPE)))
            wuq = wuq.reshape(MLA_Q_RANK, 2 * LANE * MLA_HEADS).astype(BF16)
            pad_r = lambda v: jnp.pad(v, (0, LANE - MLA_ROPE))
            gv = jnp.zeros((SUBLANE, LANE), F32)
            gv = gv.at[0].set(jnp.tile(diff_qk_norm[o, 0], 2)).at[1].set(jnp.tile(diff_qk_norm[o, 1], 2))
            gv = gv.at[2].set(mla_nope_norm[o, 0]).at[3].set(pad_r(mla_rope_norm[o, 0]))
            gv = gv.at[4].set(mla_nope_norm[o, 1]).at[5].set(pad_r(mla_rope_norm[o, 1]))
            parts = _odd_in(hs, mod[l], g_mix, w_cd, rope, gv, mla_q_a_norm[o].reshape(1, MLA_Q_RANK),
                            mla_kv_a_norm[o].reshape(1, MLA_KV_RANK), wuq, mla_w_ukv[o].astype(BF16),
                            rbd, rfull, rhalf)
            mx = _attention(lam, *parts, diff_subln[o].reshape(1, LANE), 1.0 - lam_init, latent_only=last)
            hs = _post_odd(mx, hs, mod[l], g_ffn, wo, w1, w2, latent_only=last)
    return hs
```

```python
import functools
import math

import numpy as np
import jax
import jax.numpy as jnp
from jax import lax
from jax.experimental import pallas as pl
from jax.experimental.pallas import tpu as pltpu

F32 = jnp.float32
BF16 = jnp.bfloat16

D = 1024
B = 8
SEQ = 2048
CTX = 256
T = CTX + SEQ
DEPTH = 4
GRID_W = 64
FFN_H = ((8 * D + 3 * 256 - 1) // (3 * 256)) * 256
S5_W = D // 2
S5_GROUP = 16
S5_GROUPS = S5_W // S5_GROUP
S5_STATE = 64
HG_HEADS = 4
HG_DIM = D // 8
HG_W = HG_HEADS * HG_DIM
MAX_EXP_ARG = 60.0
DIFF_HEADS = 4
DIFF_DIM = D // 16
DIFF_W = DIFF_HEADS * 2 * DIFF_DIM
MLA_HEADS = 4
MLA_NOPE = D // 8
MLA_ROPE = D // 16
MLA_V = D // 8
MLA_Q_RANK = 3 * D // 8
MLA_KV_RANK = D // 4
ROPE_DIM = D // 16
ROPE_BASE = 10000.0
EPS = 1e-6
AB_IN = S5_W + 5 * HG_W
CD_IN = 3 * DIFF_W + MLA_Q_RANK + MLA_KV_RANK + MLA_ROPE
CD_PAD = CD_IN + 64

LANE = 128
SUBLANE = 8
TM = 256
NT = T // TM
MOD_ROWS = 16
CTX_ROW = B
S5_LC = 512
S5_NC = S5_GROUPS * S5_STATE // S5_LC
S5_UC = S5_LC // S5_STATE * S5_GROUP
S5_SB = 32
HG_C = TM
VMEM_LIMIT = 56 * 1024 * 1024
LOG2E = math.log2(math.e)


def _cparams(sem):
    return pltpu.CompilerParams(dimension_semantics=sem, vmem_limit_bytes=VMEM_LIMIT)


def _const_spec(shape):
    n = len(shape)
    return pl.BlockSpec(shape, lambda *_: (0,) * n, pipeline_mode=pl.Buffered(1))


def _silu(x):
    return x * jax.nn.sigmoid(x)


def _norm_mod(x, g, shift, scale):
    ms = jnp.mean(x * x, axis=-1, keepdims=True)
    return x * lax.rsqrt(ms + EPS) * g * (1.0 + scale) + shift


def _dot(a, b):
    return jnp.dot(a, b, preferred_element_type=F32)


def _dot_nt(a, b):
    return lax.dot_general(a, b, (((1,), (1,)), ((), ())), preferred_element_type=F32)


def _mod_kernel(s_ref, w_ref, b_ref, o_ref):
    s = _silu(s_ref[...])
    o_ref[...] = _dot(s.astype(BF16), w_ref[...].astype(BF16)) + b_ref[...]


def _modulation(c, c_ctx, ada_w, ada_b):
    s = jnp.zeros((MOD_ROWS, D), F32).at[:B].set(c).at[CTX_ROW].set(c_ctx)
    nb = 1536
    out = pl.pallas_call(
        _mod_kernel,
        grid=(DEPTH, 6 * D // nb),
        in_specs=[
            pl.BlockSpec((MOD_ROWS, D), lambda l, n: (0, 0)),
            pl.BlockSpec((None, D, nb), lambda l, n: (l, 0, n)),
            pl.BlockSpec((None, 1, nb), lambda l, n: (l, 0, n)),
        ],
        out_specs=pl.BlockSpec((None, MOD_ROWS, nb), lambda l, n: (l, 0, n)),
        out_shape=jax.ShapeDtypeStruct((DEPTH, MOD_ROWS, 6 * D), F32),
        compiler_params=_cparams(("arbitrary", "arbitrary")),
        name="adaln_mod",
    )(s, ada_w, ada_b.reshape(DEPTH, 1, 6 * D))
    return out.reshape(DEPTH, MOD_ROWS, 6, D)


def _mod_spec(t0):
    return pl.BlockSpec((None, 6, D), lambda b, t: (jnp.where(t + t0 == 0, CTX_ROW, b), 0, 0))


def _even_in_kernel(x_ref, mod_ref, g_ref, w_ref, p_ref):
    a = _norm_mod(x_ref[...], g_ref[...], mod_ref[0:1, :], mod_ref[1:2, :])
    p_ref[...] = _dot(a.astype(BF16), w_ref[...])


def _even_in(hs, mod_l, g, w):
    return pl.pallas_call(
        _even_in_kernel,
        grid=(B, NT),
        in_specs=[
            pl.BlockSpec((None, TM, D), lambda b, t: (b, t, 0)),
            _mod_spec(0),
            _const_spec((1, D)),
            _const_spec((D, AB_IN)),
        ],
        out_specs=pl.BlockSpec((None, TM, AB_IN), lambda b, t: (b, t, 0)),
        out_shape=jax.ShapeDtypeStruct((B, T, AB_IN), F32),
        compiler_params=_cparams(("arbitrary", "arbitrary")),
        name="even_in_proj",
    )(hs, mod_l, g, w)


def _s5_kernel(u_ref, w_ref, c_ref, ar_ref, ai_ref, y_ref, ux_ref, yx_ref, bu_ref, h_ref, *, rev):
    i = pl.program_id(1)

    @pl.when(i == 0)
    def _():
        h_ref[...] = jnp.zeros_like(h_ref)

    ar = ar_ref[...]
    ai = ai_ref[...]
    hr, hi = h_ref[0], h_ref[1]
    n_sb = TM // S5_SB
    order = list(reversed(range(n_sb))) if rev else list(range(n_sb))
    for b in range(B):
        ux_ref[pl.ds(b, TM, stride=B), :] = u_ref[b]

    def rows_of(sb):
        return slice(sb * S5_SB * B, (sb + 1) * S5_SB * B)

    def project(sb):
        bu_ref[rows_of(sb), :] = _dot(ux_ref[rows_of(sb), :].astype(BF16), w_ref[...])

    project(order[0])
    for n, sb in enumerate(order):
        if n + 1 < n_sb:
            project(order[n + 1])
        rows = rows_of(sb)
        for t in (reversed(range(S5_SB)) if rev else range(S5_SB)):
            r = slice(rows.start + t * B, rows.start + (t + 1) * B)
            nr = ar * hr - ai * hi + bu_ref[r, 0:S5_LC]
            ni = ar * hi + ai * hr + bu_ref[r, S5_LC:2 * S5_LC]
            bu_ref[r, 0:S5_LC] = nr
            bu_ref[r, S5_LC:2 * S5_LC] = ni
            hr, hi = nr, ni
        yx_ref[rows, :] = _dot(bu_ref[rows, :].astype(BF16), c_ref[...])
    h_ref[0] = hr
    h_ref[1] = hi
    for b in range(B):
        y_ref[b] = yx_ref[pl.ds(b, TM, stride=B), :]


def _seq_tile(d, i):
    return jnp.where(d == 0, i, jnp.where(i == 0, 0, NT - i))


def _s5_scan(proj, w_in, w_out, a_r, a_i, rev):
    d = 1 if rev else 0
    return pl.pallas_call(
        functools.partial(_s5_kernel, rev=rev),
        grid=(S5_NC, NT),
        in_specs=[
            pl.BlockSpec((B, TM, S5_UC), lambda c, i: (0, _seq_tile(d, i), c)),
            pl.BlockSpec((None, None, S5_UC, 2 * S5_LC), lambda c, i: (d, c, 0, 0)),
            pl.BlockSpec((None, None, 2 * S5_LC, S5_UC), lambda c, i: (d, c, 0, 0)),
            pl.BlockSpec((None, None, B, S5_LC), lambda c, i: (d, c, 0, 0)),
            pl.BlockSpec((None, None, B, S5_LC), lambda c, i: (d, c, 0, 0)),
        ],
        out_specs=pl.BlockSpec((B, TM, S5_UC), lambda c, i: (0, _seq_tile(d, i), c)),
        out_shape=jax.ShapeDtypeStruct((B, T, S5_W), F32),
        scratch_shapes=[
            pltpu.VMEM((TM * B, S5_UC), F32),
            pltpu.VMEM((TM * B, S5_UC), F32),
            pltpu.VMEM((TM * B, 2 * S5_LC), F32),
            pltpu.VMEM((2, B, S5_LC), F32),
        ],
        compiler_params=_cparams(("arbitrary", "arbitrary")),
        name="s5_bwd" if rev else "s5_fwd",
    )(proj, w_in, w_out, a_r, a_i)


def _s5_params(lam_re, lam_im, log_step, b_re, b_im, c_re, c_im):
    lr = jnp.minimum(lam_re.astype(F32), -1e-4)
    li = lam_im.astype(F32)
    step = jnp.exp(log_step.astype(F32))[..., None]
    mag = jnp.exp(lr * step)
    a_r = mag * jnp.cos(li * step)
    a_i = mag * jnp.sin(li * step)
    den = lr * lr + li * li
    coef_r = ((a_r - 1) * lr + a_i * li) / den
    coef_i = (a_i * lr - (a_r - 1) * li) / den
    br = b_re.astype(F32)
    bi = b_im.astype(F32)
    bb_r = coef_r[..., None] * br - coef_i[..., None] * bi
    bb_i = coef_r[..., None] * bi + coef_i[..., None] * br
    gpc = S5_LC // S5_STATE
    eye = jnp.eye(gpc, dtype=F32)

    def in_blocks(bb):
        bb = bb.reshape(2, S5_NC, gpc, S5_STATE, S5_GROUP)
        return jnp.einsum('dngpc,gh->dngchp', bb, eye).reshape(2, S5_NC, S5_UC, S5_LC)

    def out_blocks(cc):
        cc = cc.reshape(2, S5_NC, gpc, S5_GROUP, S5_STATE)
        return jnp.einsum('dngcp,gh->dngphc', cc, eye).reshape(2, S5_NC, S5_LC, S5_UC)

    w_in = jnp.concatenate([in_blocks(bb_r), in_blocks(bb_i)], axis=-1).astype(BF16)
    w_out = jnp.concatenate([out_blocks(c_re.astype(F32)), -out_blocks(c_im.astype(F32))], axis=-2).astype(BF16)

    def lanes(a):
        return jnp.broadcast_to(a.reshape(2, S5_NC, 1, S5_LC), (2, S5_NC, B, S5_LC))

    return w_in, w_out, lanes(a_r), lanes(a_i)


HG_HALF = HG_C // 2
HG_LEVELS = tuple(2 ** e for e in range(int(math.log2(HG_HALF))))


def _hgrn_maps(rev):
    pos = np.arange(HG_C)
    ph = np.arange(HG_HALF)
    if rev:
        pos = HG_C - 1 - pos
        ph = HG_HALF - 1 - ph
    tri = (pos[None, :] <= pos[:, None]).astype(np.float32)
    pt, ps = ph[:, None], ph[None, :]
    lv = np.full((HG_HALF, HG_HALF), -1, np.int32)
    lv[pt == ps] = len(HG_LEVELS)
    for e, m in enumerate(HG_LEVELS):
        x, y = pt // m, ps // m
        lv[(x == y + 1) & (x % 2 == 1)] = e
    return jnp.asarray(tri, BF16), jnp.asarray(lv)


def _hgrn_kernel(q_ref, f_ref, v_ref, lb_ref, tri_ref, lv_ref, o_ref, s_ref, *, rev):
    i = pl.program_id(1)

    @pl.when(i == 0)
    def _():
        s_ref[...] = jnp.zeros_like(s_ref)

    c = HG_C
    fl = f_ref[...]
    lb = lb_ref[...]
    qb = _silu(q_ref[...]).astype(BF16)
    v = v_ref[...]
    x2 = fl * LOG2E
    a = jnp.exp2(-jnp.abs(x2))
    s1 = 1.0 + a
    r = 1.0 / s1
    kb = ((1.0 - lb) * jnp.where(x2 >= 0.0, a * r, r)).astype(BF16)
    t2 = jnp.exp2(jnp.minimum(-x2, MAX_EXP_ARG * LOG2E))
    lf = jnp.minimum(x2, 0.0) - jnp.log2(s1) + jnp.log2(1.0 + lb * t2)

    row = lax.broadcasted_iota(jnp.int32, (c, 1), 0)
    pos = (c - 1 - row) if rev else row
    first, second = (slice(HG_HALF, c), slice(0, HG_HALF)) if rev else (slice(0, HG_HALF), slice(HG_HALF, c))
    halves = (first, second)

    def from_earlier(x, j):
        return pltpu.roll(x, (c - j) if rev else j, 0)

    def from_later(x, j):
        return pltpu.roll(x, j if rev else (c - j), 0)

    def head(x, h, rows):
        return x[rows, h * HG_DIM:(h + 1) * HG_DIM]

    hi = lf.astype(BF16)
    r1 = lf - hi.astype(F32)
    mid = r1.astype(BF16)
    lo = (r1 - mid.astype(F32)).astype(BF16)
    tri = tri_ref[...]
    cum = _dot(tri, hi) + _dot(tri, mid) + _dot(tri, lo)

    lv = lv_ref[...]
    diag = len(HG_LEVELS)
    scores = [[jnp.where(lv == diag, _dot_nt(head(qb, h, r), head(kb, h, r)), 0.0) for r in halves]
              for h in range(HG_HEADS)]

    g_end = cum
    for e, m in enumerate(HG_LEVELS):
        qm = qb * jnp.exp2(cum - from_earlier(g_end, m)).astype(BF16)
        km = kb * jnp.exp2(g_end - cum).astype(BF16)
        for h in range(HG_HEADS):
            for j, r in enumerate(halves):
                scores[h][j] = jnp.where(lv == e, _dot_nt(head(qm, h, r), head(km, h, r)), scores[h][j])
        g_end = jnp.where((pos & m) != 0, g_end, from_later(g_end, m))
    qm = qb * jnp.exp2(cum - from_earlier(g_end, HG_HALF)).astype(BF16)
    km = kb * jnp.exp2(g_end - cum).astype(BF16)
    cross = [_dot_nt(head(qm, h, second), head(km, h, first)) for h in range(HG_HEADS)]
    g_end = jnp.where((pos & HG_HALF) != 0, g_end, from_later(g_end, HG_HALF))

    q_in = qb * jnp.exp2(cum).astype(BF16)
    k_out = kb * jnp.exp2(g_end - cum).astype(BF16)
    decay = jnp.exp2(g_end[0:1, :])
    vb = v.astype(BF16)
    for h in range(HG_HEADS):
        sl = slice(h * HG_DIM, (h + 1) * HG_DIM)
        st = s_ref[h]
        inter = _dot_nt(q_in[:, sl], st.astype(BF16))
        v_first, v_second = head(vb, h, first), head(vb, h, second)
        o_ref[first, sl] = inter[first] + _dot(scores[h][0].astype(BF16), v_first)
        o_ref[second, sl] = (inter[second] + _dot(cross[h].astype(BF16), v_first)
                             + _dot(scores[h][1].astype(BF16), v_second))
        s_ref[h] = decay[:, sl] * st + _dot(v[:, sl].T.astype(BF16), k_out[:, sl])


def _hgrn_scan(hg, lb, rev):
    d = 1 if rev else 0
    tri, lvl = _hgrn_maps(rev)
    return pl.pallas_call(
        functools.partial(_hgrn_kernel, rev=rev),
        grid=(B, NT),
        in_specs=[
            pl.BlockSpec((None, HG_C, HG_W), lambda b, i: (b, _seq_tile(d, i), 1)),
            pl.BlockSpec((None, HG_C, HG_W), lambda b, i: (b, _seq_tile(d, i), 2 + d)),
            pl.BlockSpec((None, HG_C, HG_W), lambda b, i: (b, _seq_tile(d, i), 4)),
            _const_spec((1, HG_W)),
            _const_spec((HG_C, HG_C)),
            _const_spec((HG_HALF, HG_HALF)),
        ],
        out_specs=pl.BlockSpec((None, HG_C, HG_W), lambda b, i: (b, _seq_tile(d, i), 0)),
        out_shape=jax.ShapeDtypeStruct((B, T, HG_W), F32),
        scratch_shapes=[pltpu.VMEM((HG_HEADS, HG_DIM, HG_DIM), F32)],
        compiler_params=_cparams(("arbitrary", "arbitrary")),
        name="hgrn_bwd" if rev else "hgrn_fwd",
    )(hg, hg, hg, lb, tri, lvl)


def _ffn_tail(mix_out, h_ref, mod_ref, g_ref, w_in_ref, w_out_ref, o_ref):
    h1 = h_ref[...] + mod_ref[2:3, :] * mix_out
    a = _norm_mod(h1, g_ref[...], mod_ref[3:4, :], mod_ref[4:5, :])
    gu = _dot(a.astype(BF16), w_in_ref[...])
    act = _silu(gu[:, :FFN_H]) * gu[:, FFN_H:]
    o_ref[...] = h1 + mod_ref[5:6, :] * _dot(act.astype(BF16), w_out_ref[...])


def _gelu_tanh(x):
    return 0.5 * x * (1.0 + jnp.tanh(math.sqrt(2.0 / math.pi) * (x + 0.044715 * (x * x * x))))


def _post_even_kernel(yf_ref, yb_ref, u_ref, of_ref, ob_ref, gate_ref, h_ref, mod_ref, g_ref,
                      dsk_ref, gw_ref, gb_ref, on_ref, wo_ref, w_in_ref, w_out_ref, o_ref):
    y = yf_ref[...] + yb_ref[...] + u_ref[...] * dsk_ref[...]
    z = _gelu_tanh(y)
    s5 = z * jax.nn.sigmoid(_dot(z.astype(BF16), gw_ref[...]) + gb_ref[...])
    mix = _dot(s5.astype(BF16), wo_ref[0:S5_W, :])
    o = of_ref[...] + ob_ref[...]
    gate = _silu(gate_ref[...])
    for h in range(HG_HEADS):
        sl = slice(h * HG_DIM, (h + 1) * HG_DIM)
        oh = o[:, sl]
        ms = jnp.mean(oh * oh, axis=-1, keepdims=True)
        hn = oh * lax.rsqrt(ms + EPS) * on_ref[...] * gate[:, sl]
        mix = mix + _dot(hn.astype(BF16), wo_ref[S5_W + h * HG_DIM:S5_W + (h + 1) * HG_DIM, :])
    _ffn_tail(mix, h_ref, mod_ref, g_ref, w_in_ref, w_out_ref, o_ref)


def _post_even(y_f, y_b, o_f, o_b, proj, hs, mod_l, g, dsk, gw, gb, on, wo, w_in, w_out):
    tok = lambda b, t: (b, t, 0)
    return pl.pallas_call(
        _post_even_kernel,
        grid=(B, NT),
        in_specs=[
            pl.BlockSpec((None, TM, S5_W), tok),
            pl.BlockSpec((None, TM, S5_W), tok),
            pl.BlockSpec((None, TM, S5_W), tok),
            pl.BlockSpec((None, TM, HG_W), tok),
            pl.BlockSpec((None, TM, HG_W), tok),
            pl.BlockSpec((None, TM, HG_W), lambda b, t: (b, t, 5)),
            pl.BlockSpec((None, TM, D), tok),
            _mod_spec(0),
            _const_spec((1, D)),
            _const_spec((1, S5_W)),
            _const_spec((S5_W, S5_W)),
            _const_spec((1, S5_W)),
            _const_spec((1, HG_DIM)),
            _const_spec((D, D)),
            _const_spec((D, 2 * FFN_H)),
            _const_spec((FFN_H, D)),
        ],
        out_specs=pl.BlockSpec((None, TM, D), tok),
        out_shape=jax.ShapeDtypeStruct((B, T, D), F32),
        compiler_params=_cparams(("arbitrary", "arbitrary")),
        name="post_even",
    )(y_f, y_b, proj, o_f, o_b, proj, hs, mod_l, g, dsk, gw, gb, on, wo, w_in, w_out)


def _post_odd_kernel(mx_ref, h_ref, mod_ref, g_ref, wo_ref, w_in_ref, w_out_ref, o_ref):
    _ffn_tail(_dot(mx_ref[...], wo_ref[...]), h_ref, mod_ref, g_ref, w_in_ref, w_out_ref, o_ref)


def _post_odd(mx, hs, mod_l, g, wo, w_in, w_out, latent_only):
    t0 = 1 if latent_only else 0
    nt = NT - t0
    return pl.pallas_call(
        _post_odd_kernel,
        grid=(B, nt),
        in_specs=[
            pl.BlockSpec((None, TM, D), lambda b, t: (b, t, 0)),
            pl.BlockSpec((None, TM, D), lambda b, t: (b, t + t0, 0)),
            _mod_spec(t0),
            _const_spec((1, D)),
            _const_spec((D, D)),
            _const_spec((D, 2 * FFN_H)),
            _const_spec((FFN_H, D)),
        ],
        out_specs=pl.BlockSpec((None, TM, D), lambda b, t: (b, t, 0)),
        out_shape=jax.ShapeDtypeStruct((B, nt * TM, D), F32),
        compiler_params=_cparams(("arbitrary", "arbitrary")),
        name="post_odd",
    )(mx, hs, mod_l, g, wo, w_in, w_out)


def _group_rms(xs, r_ref):
    ms = _dot((xs * xs).astype(BF16), r_ref)
    return xs * lax.rsqrt(ms + EPS)


def _rope(xs, cos, sin_signed, first_half):
    rot = jnp.where(first_half, pltpu.roll(xs, LANE - ROPE_DIM // 4, 1), pltpu.roll(xs, ROPE_DIM // 4, 1))
    return xs * cos + rot * sin_signed


def _odd_in_kernel(x_ref, mod_ref, g_ref, w_ref, rope_ref, gv_ref, qan_ref, kvan_ref, wuq_ref, wukv_ref,
                   rbd_ref, rfull_ref, rhalf_ref,
                   dq_ref, k1_ref, k2_ref, dv_ref, mq_ref, mk_ref, mv_ref, *, mla_scale):
    a = _norm_mod(x_ref[...], g_ref[...], mod_ref[0:1, :], mod_ref[1:2, :])
    p = _dot(a.astype(BF16), w_ref[...])
    cos_d, sin_d, cos_m, sin_m = rope_ref[0], rope_ref[1], rope_ref[2], rope_ref[3]
    lane = lax.broadcasted_iota(jnp.int32, (1, LANE), 1)
    first_half = (lane % (ROPE_DIM // 2)) < (ROPE_DIM // 4)
    low = lane < DIFF_DIM
    rbd, rfull, rhalf = rbd_ref[...], rfull_ref[...], rhalf_ref[...]
    g_dq, g_dk = gv_ref[0:1, :], gv_ref[1:2, :]
    g_qn, g_qr, g_kn, g_kr = gv_ref[2:3, :], gv_ref[3:4, :], gv_ref[4:5, :], gv_ref[5:6, :]
    diff_scale = DIFF_DIM ** -0.5 * LOG2E
    ones = jnp.ones((TM, LANE), BF16)

    for h in range(DIFF_HEADS):
        sl = slice(h * LANE, (h + 1) * LANE)
        qh = _rope(_group_rms(p[:, sl], rbd) * g_dq, cos_d, sin_d, first_half)
        dq_ref[:, sl] = (qh * diff_scale).astype(BF16)
        kh = _rope(_group_rms(p[:, DIFF_W + h * LANE:DIFF_W + (h + 1) * LANE], rbd) * g_dk, cos_d, sin_d, first_half)
        k1_ref[:, sl] = jnp.where(low, kh, 0.0).astype(BF16)
        k2_ref[:, sl] = jnp.where(low, 0.0, kh).astype(BF16)
        dv_ref[:, 2 * h * LANE:(2 * h + 1) * LANE] = p[:, 2 * DIFF_W + h * LANE:2 * DIFF_W + (h + 1) * LANE].astype(BF16)
        dv_ref[:, (2 * h + 1) * LANE:(2 * h + 2) * LANE] = ones

    o_cq = 3 * DIFF_W
    o_ckv = o_cq + MLA_Q_RANK
    o_kr = o_ckv + MLA_KV_RANK
    cq = p[:, o_cq:o_ckv]
    cqn = cq * lax.rsqrt(jnp.mean(cq * cq, axis=-1, keepdims=True) + EPS) * qan_ref[...]
    q = _dot(cqn.astype(BF16), wuq_ref[...])
    ckv = p[:, o_ckv:o_kr]
    ckvn = ckv * lax.rsqrt(jnp.mean(ckv * ckv, axis=-1, keepdims=True) + EPS) * kvan_ref[...]
    kv = _dot(ckvn.astype(BF16), wukv_ref[...])
    k_rope = _rope(_group_rms(p[:, o_kr:o_kr + LANE], rhalf) * g_kr, cos_m, sin_m, first_half).astype(BF16)
    for h in range(MLA_HEADS):
        o = 2 * LANE * h
        q_nope = _group_rms(q[:, o:o + LANE], rfull) * g_qn
        q_rope = _rope(_group_rms(q[:, o + LANE:o + 2 * LANE], rhalf) * g_qr, cos_m, sin_m, first_half)
        mq_ref[:, o:o + LANE] = (q_nope * mla_scale).astype(BF16)
        mq_ref[:, o + LANE:o + 2 * LANE] = (q_rope * mla_scale).astype(BF16)
        mk_ref[:, o:o + LANE] = (_group_rms(kv[:, o:o + LANE], rfull) * g_kn).astype(BF16)
        mk_ref[:, o + LANE:o + 2 * LANE] = k_rope
        mv_ref[:, o:o + LANE] = kv[:, o + LANE:o + 2 * LANE].astype(BF16)
        mv_ref[:, o + LANE:o + 2 * LANE] = ones


def _odd_in(hs, mod_l, g, w, rope, gv, qan, kvan, wuq, wukv, rbd, rfull, rhalf):
    tok = lambda b, t: (b, t, 0)
    sds = lambda n: jax.ShapeDtypeStruct((B, T, n), BF16)
    wide = 2 * LANE * MLA_HEADS
    widths = (DIFF_W, DIFF_W, DIFF_W, wide, wide, wide, wide)
    return pl.pallas_call(
        functools.partial(_odd_in_kernel, mla_scale=(MLA_NOPE + MLA_ROPE) ** -0.5 * LOG2E),
        grid=(B, NT),
        in_specs=[
            pl.BlockSpec((None, TM, D), tok),
            _mod_spec(0),
            _const_spec((1, D)),
            _const_spec((D, CD_PAD)),
            pl.BlockSpec((4, TM, LANE), lambda b, t: (0, t, 0)),
            _const_spec((SUBLANE, LANE)),
            _const_spec((1, MLA_Q_RANK)),
            _const_spec((1, MLA_KV_RANK)),
            _const_spec((MLA_Q_RANK, 2 * LANE * MLA_HEADS)),
            _const_spec((MLA_KV_RANK, 2 * LANE * MLA_HEADS)),
            _const_spec((LANE, LANE)),
            _const_spec((LANE, LANE)),
            _const_spec((LANE, LANE)),
        ],
        out_specs=[pl.BlockSpec((None, TM, n), tok) for n in widths],
        out_shape=[sds(n) for n in widths],
        compiler_params=_cparams(("arbitrary", "arbitrary")),
        name="odd_in_proj",
    )(hs, mod_l, g, w, rope, gv, qan, kvan, wuq, wukv, rbd, rfull, rhalf)


def _exp2_shifted(s):
    return jnp.exp2(s - jnp.max(s, axis=-1, keepdims=True)).astype(BF16)


def _pv_normalised(e, v_aug):
    o = _dot(e, v_aug)
    return o[:, :LANE] / o[:, LANE:]


def _attn_kernel(lam_ref, dq_ref, k1_ref, k2_ref, dv_ref, mq_ref, mk_ref, mv_ref, sub_ref, o_ref,
                 *, out_scale, t0):
    lam = lam_ref[0, 0]

    def body(kv):
        for h in range(DIFF_HEADS):
            sl = slice(h * LANE, (h + 1) * LANE)
            sv = slice(2 * LANE * h, 2 * LANE * (h + 1))
            qh = dq_ref[:, sl]
            o1 = _pv_normalised(_exp2_shifted(_dot_nt(qh, k1_ref[0:kv, sl])), dv_ref[0:kv, sv])
            o2 = _pv_normalised(_exp2_shifted(_dot_nt(qh, k2_ref[0:kv, sl])), dv_ref[0:kv, sv])
            o = o1 - lam * o2
            o = o * lax.rsqrt(jnp.mean(o * o, axis=-1, keepdims=True) + EPS) * sub_ref[...] * out_scale
            o_ref[:, sl] = o.astype(BF16)
        for h in range(MLA_HEADS):
            sq = slice(2 * LANE * h, 2 * LANE * (h + 1))
            o = _pv_normalised(_exp2_shifted(_dot_nt(mq_ref[:, sq], mk_ref[0:kv, sq])), mv_ref[0:kv, sq])
            o_ref[:, DIFF_W + h * LANE:DIFF_W + (h + 1) * LANE] = o.astype(BF16)

    if t0 == 0:
        t = pl.program_id(1)
        pl.when(t == 0)(lambda: body(CTX))
        pl.when(t > 0)(lambda: body(T))
    else:
        body(T)


def _attention(lam, dq, k1, k2, dv, mq, mk, mv, sub, out_scale, latent_only):
    t0 = 1 if latent_only else 0
    nt = NT - t0
    qspec = lambda n: pl.BlockSpec((None, TM, n), lambda b, t: (b, t + t0, 0))
    kspec = lambda n: pl.BlockSpec((None, T, n), lambda b, t: (b, 0, 0))
    return pl.pallas_call(
        functools.partial(_attn_kernel, out_scale=out_scale, t0=t0),
        grid=(B, nt),
        in_specs=[
            pl.BlockSpec(memory_space=pltpu.SMEM),
            qspec(DIFF_W), kspec(DIFF_W), kspec(DIFF_W), kspec(2 * LANE * DIFF_HEADS),
            qspec(2 * LANE * MLA_HEADS), kspec(2 * LANE * MLA_HEADS), kspec(2 * LANE * MLA_HEADS),
            _const_spec((1, LANE)),
        ],
        out_specs=pl.BlockSpec((None, TM, D), lambda b, t: (b, t, 0)),
        out_shape=jax.ShapeDtypeStruct((B, nt * TM, D), BF16),
        compiler_params=_cparams(("arbitrary", "arbitrary")),
        name="attention",
    )(lam, dq, k1, k2, dv, mq, mk, mv, sub)


def _rope_tables():
    n_tok = SEQ
    rows = jnp.repeat(jnp.arange(n_tok // GRID_W, dtype=jnp.int32), GRID_W)
    cols = jnp.tile(jnp.arange(GRID_W, dtype=jnp.int32), n_tok // GRID_W)
    n_freq = ROPE_DIM // 4
    inv = jnp.power(ROPE_BASE, -jnp.arange(n_freq, dtype=F32) / n_freq)
    ang_r = rows.astype(F32)[:, None] * inv
    ang_c = cols.astype(F32)[:, None] * inv
    ang = jnp.concatenate([ang_r, ang_r, ang_c, ang_c], axis=-1)
    sign = jnp.where((jnp.arange(ROPE_DIM) % (ROPE_DIM // 2)) < n_freq, -1.0, 1.0).astype(F32)
    cos = jnp.concatenate([jnp.ones((CTX, ROPE_DIM), F32), jnp.cos(ang)], axis=0)
    sin = jnp.concatenate([jnp.zeros((CTX, ROPE_DIM), F32), jnp.sin(ang) * sign], axis=0)
    one, zero = jnp.ones_like(cos), jnp.zeros_like(sin)
    cat = lambda a, b: jnp.concatenate([a, b], axis=-1)
    return jnp.stack([cat(cos, cos), cat(sin, sin), cat(cos, one), cat(sin, zero)])


def _averaging_mats():
    idx = np.arange(LANE)
    same64 = (idx[:, None] // DIFF_DIM) == (idx[None, :] // DIFF_DIM)
    rbd = np.where(same64, 1.0 / DIFF_DIM, 0.0)
    rfull = np.full((LANE, LANE), 1.0 / LANE)
    rhalf = np.where((idx[:, None] < MLA_ROPE) & (idx[None, :] < MLA_ROPE), 1.0 / MLA_ROPE, 0.0)
    return tuple(jnp.asarray(m, BF16) for m in (rbd, rfull, rhalf))


def kernel(x, c, ctx, c_ctx, ada_w, ada_b, norm_mix, norm_ffn, w_out, ffn_w_in, ffn_w_out, ab_w_in, s5_lambda_re, s5_lambda_im, s5_log_step, s5_b_re, s5_b_im, s5_c_re, s5_c_im, s5_d, s5_glu_w, s5_glu_b, hgrn_lb_logits, hgrn_out_norm, cd_w_in, diff_lambda, diff_qk_norm, diff_subln, mla_q_a_norm, mla_kv_a_norm, mla_w_uq, mla_w_ukv, mla_nope_norm, mla_rope_norm):
    assert x.shape == (B, SEQ, D) and ctx.shape == (B, CTX, D)
    mod = _modulation(c, c_ctx, ada_w, ada_b)
    hs = jnp.concatenate([ctx, x], axis=1)
    rope = _rope_tables()
    rbd, rfull, rhalf = _averaging_mats()

    lb_p = jax.nn.softmax(hgrn_lb_logits.astype(F32), axis=0)
    lower_bounds = jnp.cumsum(lb_p, axis=0) - lb_p[0:1]

    for l in range(DEPTH):
        last = l == DEPTH - 1
        g_mix = norm_mix[l].reshape(1, D)
        g_ffn = norm_ffn[l].reshape(1, D)
        wo = w_out[l].astype(BF16)
        w1 = ffn_w_in[l].astype(BF16)
        w2 = ffn_w_out[l].astype(BF16)
        if l % 2 == 0:
            e = l // 2
            proj = _even_in(hs, mod[l], g_mix, ab_w_in[e].astype(BF16))
            s5w_in, s5w_out, a_r, a_i = _s5_params(s5_lambda_re[e], s5_lambda_im[e], s5_log_step[e],
                                                   s5_b_re[e], s5_b_im[e], s5_c_re[e], s5_c_im[e])
            y_f = _s5_scan(proj, s5w_in, s5w_out, a_r, a_i, rev=False)
            y_b = _s5_scan(proj, s5w_in, s5w_out, a_r, a_i, rev=True)
            lb = lower_bounds[e].reshape(1, HG_W)
            o_f = _hgrn_scan(proj, lb, rev=False)
            o_b = _hgrn_scan(proj, lb, rev=True)
            hs = _post_even(y_f, y_b, o_f, o_b, proj, hs, mod[l], g_ffn,
                            s5_d[e].reshape(1, S5_W), s5_glu_w[e].astype(BF16), s5_glu_b[e].reshape(1, S5_W),
                            hgrn_out_norm[e].reshape(1, HG_DIM), wo, w1, w2)
        else:
            o = l // 2
            lam_init = 0.8 - 0.6 * math.exp(-0.3 * l)
            lv = diff_lambda[o].astype(F32)
            lam = (jnp.exp(jnp.sum(lv[0] * lv[1])) - jnp.exp(jnp.sum(lv[2] * lv[3])) + lam_init).reshape(1, 1)
            w_cd = jnp.pad(cd_w_in[o], ((0, 0), (0, CD_PAD - CD_IN))).astype(BF16)
            wuq = mla_w_uq[o].reshape(MLA_Q_RANK, MLA_HEADS, MLA_NOPE + MLA_ROPE)
            wuq = jnp.pad(wuq, ((0, 0), (0, 0), (0, 2 * LANE - MLA_NOPE - MLA_ROPE)))
            wuq = wuq.reshape(MLA_Q_RANK, 2 * LANE * MLA_HEADS).astype(BF16)
            pad_r = lambda v: jnp.pad(v, (0, LANE - MLA_ROPE))
            gv = jnp.zeros((SUBLANE, LANE), F32)
            gv = gv.at[0].set(jnp.tile(diff_qk_norm[o, 0], 2)).at[1].set(jnp.tile(diff_qk_norm[o, 1], 2))
            gv = gv.at[2].set(mla_nope_norm[o, 0]).at[3].set(pad_r(mla_rope_norm[o, 0]))
            gv = gv.at[4].set(mla_nope_norm[o, 1]).at[5].set(pad_r(mla_rope_norm[o, 1]))
            parts = _odd_in(hs, mod[l], g_mix, w_cd, rope, gv, mla_q_a_norm[o].reshape(1, MLA_Q_RANK),
                            mla_kv_a_norm[o].reshape(1, MLA_KV_RANK), wuq, mla_w_ukv[o].astype(BF16),
                            rbd, rfull, rhalf)
            mx = _attention(lam, *parts, diff_subln[o].reshape(1, LANE), 1.0 - lam_init, latent_only=last)
            hs = _post_odd(mx, hs, mod[l], g_ffn, wo, w1, w2, latent_only=last)
    return hs
```

```python
import functools
import math

import numpy as np
import jax
import jax.numpy as jnp
from jax import lax
from jax.experimental import pallas as pl
from jax.experimental.pallas import tpu as pltpu

F32 = jnp.float32
BF16 = jnp.bfloat16

D = 1024
B = 8
SEQ = 2048
CTX = 256
T = CTX + SEQ
DEPTH = 4
GRID_W = 64
FFN_H = ((8 * D + 3 * 256 - 1) // (3 * 256)) * 256
S5_W = D // 2
S5_GROUP = 16
S5_GROUPS = S5_W // S5_GROUP
S5_STATE = 64
HG_HEADS = 4
HG_DIM = D // 8
HG_W = HG_HEADS * HG_DIM
MAX_EXP_ARG = 60.0
DIFF_HEADS = 4
DIFF_DIM = D // 16
DIFF_W = DIFF_HEADS * 2 * DIFF_DIM
MLA_HEADS = 4
MLA_NOPE = D // 8
MLA_ROPE = D // 16
MLA_V = D // 8
MLA_Q_RANK = 3 * D // 8
MLA_KV_RANK = D // 4
ROPE_DIM = D // 16
ROPE_BASE = 10000.0
EPS = 1e-6
AB_IN = S5_W + 5 * HG_W
CD_IN = 3 * DIFF_W + MLA_Q_RANK + MLA_KV_RANK + MLA_ROPE
CD_PAD = CD_IN + 64

LANE = 128
SUBLANE = 8
TS = CTX
NTS = T // TS
TM = 512
NTL = SEQ // TM
MOD_ROWS = 16
CTX_ROW = B
S5_LC = 512
S5_NC = S5_GROUPS * S5_STATE // S5_LC
S5_UC = S5_LC // S5_STATE * S5_GROUP
S5_SB = 32
HG_C = TS
VMEM_LIMIT = 56 * 1024 * 1024
LOG2E = math.log2(math.e)


def _cparams(sem):
    return pltpu.CompilerParams(dimension_semantics=sem, vmem_limit_bytes=VMEM_LIMIT)


def _const_spec(shape):
    n = len(shape)
    return pl.BlockSpec(shape, lambda *_: (0,) * n, pipeline_mode=pl.Buffered(1))


def _silu(x):
    return x * jax.nn.sigmoid(x)


def _norm_mod(x, g, shift, scale):
    ms = jnp.mean(x * x, axis=-1, keepdims=True)
    return x * lax.rsqrt(ms + EPS) * g * (1.0 + scale) + shift


def _dot(a, b):
    return jnp.dot(a, b, preferred_element_type=F32)


def _dot_nt(a, b):
    return lax.dot_general(a, b, (((1,), (1,)), ((), ())), preferred_element_type=F32)


def _mod_kernel(s_ref, w_ref, b_ref, o_ref):
    s = _silu(s_ref[...])
    o_ref[...] = _dot(s.astype(BF16), w_ref[...].astype(BF16)) + b_ref[...]


def _modulation(c, c_ctx, ada_w, ada_b):
    s = jnp.zeros((MOD_ROWS, D), F32).at[:B].set(c).at[CTX_ROW].set(c_ctx)
    nb = 1536
    out = pl.pallas_call(
        _mod_kernel,
        grid=(DEPTH, 6 * D // nb),
        in_specs=[
            pl.BlockSpec((MOD_ROWS, D), lambda l, n: (0, 0)),
            pl.BlockSpec((None, D, nb), lambda l, n: (l, 0, n)),
            pl.BlockSpec((None, 1, nb), lambda l, n: (l, 0, n)),
        ],
        out_specs=pl.BlockSpec((None, MOD_ROWS, nb), lambda l, n: (l, 0, n)),
        out_shape=jax.ShapeDtypeStruct((DEPTH, MOD_ROWS, 6 * D), F32),
        compiler_params=_cparams(("arbitrary", "arbitrary")),
        name="adaln_mod",
    )(s, ada_w, ada_b.reshape(DEPTH, 1, 6 * D))
    return out.reshape(DEPTH, MOD_ROWS, 6, D)


def _mod_spec(ctx_tile):
    return pl.BlockSpec((None, 6, D), lambda b, t: (jnp.where(t == ctx_tile, CTX_ROW, b), 0, 0))


def _for_tile_rows(has_ctx, body):
    if has_ctx:
        t = pl.program_id(1)
        pl.when(t < NTL)(lambda: body(TM))
        pl.when(t == NTL)(lambda: body(CTX))
    else:
        body(TM)


def _even_in_kernel(x_ref, mod_ref, g_ref, w_ref, p_ref):
    def body(n):
        a = _norm_mod(x_ref[0:n], g_ref[...], mod_ref[0:1, :], mod_ref[1:2, :])
        p_ref[0:n] = _dot(a.astype(BF16), w_ref[...])

    _for_tile_rows(True, body)


def _even_in(hs, mod_l, g, w):
    return pl.pallas_call(
        _even_in_kernel,
        grid=(B, NTL + 1),
        in_specs=[
            pl.BlockSpec((None, TM, D), lambda b, t: (b, t, 0)),
            _mod_spec(NTL),
            _const_spec((1, D)),
            _const_spec((D, AB_IN)),
        ],
        out_specs=pl.BlockSpec((None, TM, AB_IN), lambda b, t: (b, t, 0)),
        out_shape=jax.ShapeDtypeStruct((B, T, AB_IN), F32),
        compiler_params=_cparams(("arbitrary", "arbitrary")),
        name="even_in_proj",
    )(hs, mod_l, g, w)


def _s5_kernel(u_ref, w_ref, c_ref, ar_ref, ai_ref, y_ref, ux_ref, yx_ref, bu_ref, h_ref, *, rev):
    i = pl.program_id(1)

    @pl.when(i == 0)
    def _():
        h_ref[...] = jnp.zeros_like(h_ref)

    ar = ar_ref[...]
    ai = ai_ref[...]
    hr, hi = h_ref[0], h_ref[1]
    n_sb = TS // S5_SB
    order = list(reversed(range(n_sb))) if rev else list(range(n_sb))
    for b in range(B):
        ux_ref[pl.ds(b, TS, stride=B), :] = u_ref[b]

    def rows_of(sb):
        return slice(sb * S5_SB * B, (sb + 1) * S5_SB * B)

    def project(sb):
        bu_ref[rows_of(sb), :] = _dot(ux_ref[rows_of(sb), :].astype(BF16), w_ref[...])

    project(order[0])
    for n, sb in enumerate(order):
        if n + 1 < n_sb:
            project(order[n + 1])
        rows = rows_of(sb)
        for t in (reversed(range(S5_SB)) if rev else range(S5_SB)):
            r = slice(rows.start + t * B, rows.start + (t + 1) * B)
            nr = ar * hr - ai * hi + bu_ref[r, 0:S5_LC]
            ni = ar * hi + ai * hr + bu_ref[r, S5_LC:2 * S5_LC]
            bu_ref[r, 0:S5_LC] = nr
            bu_ref[r, S5_LC:2 * S5_LC] = ni
            hr, hi = nr, ni
        yx_ref[rows, :] = _dot(bu_ref[rows, :].astype(BF16), c_ref[...])
    h_ref[0] = hr
    h_ref[1] = hi
    for b in range(B):
        y_ref[b] = yx_ref[pl.ds(b, TS, stride=B), :]


def _seq_tile(d, i):
    return jnp.where(i == 0, NTS - 1, jnp.where(d == 0, i - 1, NTS - 1 - i))


def _s5_scan(proj, w_in, w_out, a_r, a_i, rev):
    d = 1 if rev else 0
    return pl.pallas_call(
        functools.partial(_s5_kernel, rev=rev),
        grid=(S5_NC, NTS),
        in_specs=[
            pl.BlockSpec((B, TS, S5_UC), lambda c, i: (0, _seq_tile(d, i), c)),
            pl.BlockSpec((None, None, S5_UC, 2 * S5_LC), lambda c, i: (d, c, 0, 0)),
            pl.BlockSpec((None, None, 2 * S5_LC, S5_UC), lambda c, i: (d, c, 0, 0)),
            pl.BlockSpec((None, None, B, S5_LC), lambda c, i: (d, c, 0, 0)),
            pl.BlockSpec((None, None, B, S5_LC), lambda c, i: (d, c, 0, 0)),
        ],
        out_specs=pl.BlockSpec((B, TS, S5_UC), lambda c, i: (0, _seq_tile(d, i), c)),
        out_shape=jax.ShapeDtypeStruct((B, T, S5_W), F32),
        scratch_shapes=[
            pltpu.VMEM((TS * B, S5_UC), F32),
            pltpu.VMEM((TS * B, S5_UC), F32),
            pltpu.VMEM((TS * B, 2 * S5_LC), F32),
            pltpu.VMEM((2, B, S5_LC), F32),
        ],
        compiler_params=_cparams(("arbitrary", "arbitrary")),
        name="s5_bwd" if rev else "s5_fwd",
    )(proj, w_in, w_out, a_r, a_i)


def _s5_params(lam_re, lam_im, log_step, b_re, b_im, c_re, c_im):
    lr = jnp.minimum(lam_re.astype(F32), -1e-4)
    li = lam_im.astype(F32)
    step = jnp.exp(log_step.astype(F32))[..., None]
    mag = jnp.exp(lr * step)
    a_r = mag * jnp.cos(li * step)
    a_i = mag * jnp.sin(li * step)
    den = lr * lr + li * li
    coef_r = ((a_r - 1) * lr + a_i * li) / den
    coef_i = (a_i * lr - (a_r - 1) * li) / den
    br = b_re.astype(F32)
    bi = b_im.astype(F32)
    bb_r = coef_r[..., None] * br - coef_i[..., None] * bi
    bb_i = coef_r[..., None] * bi + coef_i[..., None] * br
    gpc = S5_LC // S5_STATE
    eye = jnp.eye(gpc, dtype=F32)

    def in_blocks(bb):
        bb = bb.reshape(2, S5_NC, gpc, S5_STATE, S5_GROUP)
        return jnp.einsum('dngpc,gh->dngchp', bb, eye).reshape(2, S5_NC, S5_UC, S5_LC)

    def out_blocks(cc):
        cc = cc.reshape(2, S5_NC, gpc, S5_GROUP, S5_STATE)
        return jnp.einsum('dngcp,gh->dngphc', cc, eye).reshape(2, S5_NC, S5_LC, S5_UC)

    w_in = jnp.concatenate([in_blocks(bb_r), in_blocks(bb_i)], axis=-1).astype(BF16)
    w_out = jnp.concatenate([out_blocks(c_re.astype(F32)), -out_blocks(c_im.astype(F32))], axis=-2).astype(BF16)

    def lanes(a):
        return jnp.broadcast_to(a.reshape(2, S5_NC, 1, S5_LC), (2, S5_NC, B, S5_LC))

    return w_in, w_out, lanes(a_r), lanes(a_i)


HG_HALF = HG_C // 2
HG_LEVELS = tuple(2 ** e for e in range(int(math.log2(HG_HALF))))


def _hgrn_maps(rev):
    pos = np.arange(HG_C)
    ph = np.arange(HG_HALF)
    if rev:
        pos = HG_C - 1 - pos
        ph = HG_HALF - 1 - ph
    tri = (pos[None, :] <= pos[:, None]).astype(np.float32)
    pt, ps = ph[:, None], ph[None, :]
    lv = np.full((HG_HALF, HG_HALF), -1, np.int32)
    lv[pt == ps] = len(HG_LEVELS)
    for e, m in enumerate(HG_LEVELS):
        x, y = pt // m, ps // m
        lv[(x == y + 1) & (x % 2 == 1)] = e
    return jnp.asarray(tri, BF16), jnp.asarray(lv)


def _hgrn_kernel(q_ref, f_ref, v_ref, lb_ref, tri_ref, lv_ref, o_ref, s_ref, *, rev):
    i = pl.program_id(1)

    @pl.when(i == 0)
    def _():
        s_ref[...] = jnp.zeros_like(s_ref)

    c = HG_C
    fl = f_ref[...]
    lb = lb_ref[...]
    qb = _silu(q_ref[...]).astype(BF16)
    v = v_ref[...]
    x2 = fl * LOG2E
    a = jnp.exp2(-jnp.abs(x2))
    s1 = 1.0 + a
    r = 1.0 / s1
    kb = ((1.0 - lb) * jnp.where(x2 >= 0.0, a * r, r)).astype(BF16)
    t2 = jnp.exp2(jnp.minimum(-x2, MAX_EXP_ARG * LOG2E))
    lf = jnp.minimum(x2, 0.0) - jnp.log2(s1) + jnp.log2(1.0 + lb * t2)

    row = lax.broadcasted_iota(jnp.int32, (c, 1), 0)
    pos = (c - 1 - row) if rev else row
    first, second = (slice(HG_HALF, c), slice(0, HG_HALF)) if rev else (slice(0, HG_HALF), slice(HG_HALF, c))
    halves = (first, second)

    def from_earlier(x, j):
        return pltpu.roll(x, (c - j) if rev else j, 0)

    def from_later(x, j):
        return pltpu.roll(x, j if rev else (c - j), 0)

    def head(x, h, rows):
        return x[rows, h * HG_DIM:(h + 1) * HG_DIM]

    hi = lf.astype(BF16)
    r1 = lf - hi.astype(F32)
    mid = r1.astype(BF16)
    lo = (r1 - mid.astype(F32)).astype(BF16)
    tri = tri_ref[...]
    cum = _dot(tri, hi) + _dot(tri, mid) + _dot(tri, lo)

    lv = lv_ref[...]
    diag = len(HG_LEVELS)
    scores = [[jnp.where(lv == diag, _dot_nt(head(qb, h, r), head(kb, h, r)), 0.0) for r in halves]
              for h in range(HG_HEADS)]

    g_end = cum
    for e, m in enumerate(HG_LEVELS):
        qm = qb * jnp.exp2(cum - from_earlier(g_end, m)).astype(BF16)
        km = kb * jnp.exp2(g_end - cum).astype(BF16)
        for h in range(HG_HEADS):
            for j, r in enumerate(halves):
                scores[h][j] = jnp.where(lv == e, _dot_nt(head(qm, h, r), head(km, h, r)), scores[h][j])
        g_end = jnp.where((pos & m) != 0, g_end, from_later(g_end, m))
    qm = qb * jnp.exp2(cum - from_earlier(g_end, HG_HALF)).astype(BF16)
    km = kb * jnp.exp2(g_end - cum).astype(BF16)
    cross = [_dot_nt(head(qm, h, second), head(km, h, first)) for h in range(HG_HEADS)]
    g_end = jnp.where((pos & HG_HALF) != 0, g_end, from_later(g_end, HG_HALF))

    q_in = qb * jnp.exp2(cum).astype(BF16)
    k_out = kb * jnp.exp2(g_end - cum).astype(BF16)
    decay = jnp.exp2(g_end[0:1, :])
    vb = v.astype(BF16)
    for h in range(HG_HEADS):
        sl = slice(h * HG_DIM, (h + 1) * HG_DIM)
        st = s_ref[h]
        inter = _dot_nt(q_in[:, sl], st.astype(BF16))
        v_first, v_second = head(vb, h, first), head(vb, h, second)
        o_ref[first, sl] = inter[first] + _dot(scores[h][0].astype(BF16), v_first)
        o_ref[second, sl] = (inter[second] + _dot(cross[h].astype(BF16), v_first)
                             + _dot(scores[h][1].astype(BF16), v_second))
        s_ref[h] = decay[:, sl] * st + _dot(v[:, sl].T.astype(BF16), k_out[:, sl])


def _hgrn_scan(hg, lb, rev):
    d = 1 if rev else 0
    tri, lvl = _hgrn_maps(rev)
    return pl.pallas_call(
        functools.partial(_hgrn_kernel, rev=rev),
        grid=(B, NTS),
        in_specs=[
            pl.BlockSpec((None, HG_C, HG_W), lambda b, i: (b, _seq_tile(d, i), 1)),
            pl.BlockSpec((None, HG_C, HG_W), lambda b, i: (b, _seq_tile(d, i), 2 + d)),
            pl.BlockSpec((None, HG_C, HG_W), lambda b, i: (b, _seq_tile(d, i), 4)),
            _const_spec((1, HG_W)),
            _const_spec((HG_C, HG_C)),
            _const_spec((HG_HALF, HG_HALF)),
        ],
        out_specs=pl.BlockSpec((None, HG_C, HG_W), lambda b, i: (b, _seq_tile(d, i), 0)),
        out_shape=jax.ShapeDtypeStruct((B, T, HG_W), F32),
        scratch_shapes=[pltpu.VMEM((HG_HEADS, HG_DIM, HG_DIM), F32)],
        compiler_params=_cparams(("arbitrary", "arbitrary")),
        name="hgrn_bwd" if rev else "hgrn_fwd",
    )(hg, hg, hg, lb, tri, lvl)


def _ffn_tail(mix_out, h_ref, mod_ref, g_ref, w_in_ref, w_out_ref, o_ref):
    h1 = h_ref[...] + mod_ref[2:3, :] * mix_out
    a = _norm_mod(h1, g_ref[...], mod_ref[3:4, :], mod_ref[4:5, :])
    gu = _dot(a.astype(BF16), w_in_ref[...])
    act = _silu(gu[:, :FFN_H]) * gu[:, FFN_H:]
    o_ref[...] = h1 + mod_ref[5:6, :] * _dot(act.astype(BF16), w_out_ref[...])


def _gelu_tanh(x):
    return 0.5 * x * (1.0 + jnp.tanh(math.sqrt(2.0 / math.pi) * (x + 0.044715 * (x * x * x))))


def _post_even_kernel(yf_ref, yb_ref, u_ref, of_ref, ob_ref, gate_ref, h_ref, mod_ref, g_ref,
                      dsk_ref, gw_ref, gb_ref, on_ref, wo_ref, w_in_ref, w_out_ref, o_ref):
    y = yf_ref[...] + yb_ref[...] + u_ref[...] * dsk_ref[...]
    z = _gelu_tanh(y)
    s5 = z * jax.nn.sigmoid(_dot(z.astype(BF16), gw_ref[...]) + gb_ref[...])
    mix = _dot(s5.astype(BF16), wo_ref[0:S5_W, :])
    o = of_ref[...] + ob_ref[...]
    gate = _silu(gate_ref[...])
    for h in range(HG_HEADS):
        sl = slice(h * HG_DIM, (h + 1) * HG_DIM)
        oh = o[:, sl]
        ms = jnp.mean(oh * oh, axis=-1, keepdims=True)
        hn = oh * lax.rsqrt(ms + EPS) * on_ref[...] * gate[:, sl]
        mix = mix + _dot(hn.astype(BF16), wo_ref[S5_W + h * HG_DIM:S5_W + (h + 1) * HG_DIM, :])
    _ffn_tail(mix, h_ref, mod_ref, g_ref, w_in_ref, w_out_ref, o_ref)


def _post_even(y_f, y_b, o_f, o_b, proj, hs, mod_l, g, dsk, gw, gb, on, wo, w_in, w_out):
    tok = lambda b, t: (b, t, 0)
    return pl.pallas_call(
        _post_even_kernel,
        grid=(B, NTS),
        in_specs=[
            pl.BlockSpec((None, TS, S5_W), tok),
            pl.BlockSpec((None, TS, S5_W), tok),
            pl.BlockSpec((None, TS, S5_W), tok),
            pl.BlockSpec((None, TS, HG_W), tok),
            pl.BlockSpec((None, TS, HG_W), tok),
            pl.BlockSpec((None, TS, HG_W), lambda b, t: (b, t, 5)),
            pl.BlockSpec((None, TS, D), tok),
            _mod_spec(NTS - 1),
            _const_spec((1, D)),
            _const_spec((1, S5_W)),
            _const_spec((S5_W, S5_W)),
            _const_spec((1, S5_W)),
            _const_spec((1, HG_DIM)),
            _const_spec((D, D)),
            _const_spec((D, 2 * FFN_H)),
            _const_spec((FFN_H, D)),
        ],
        out_specs=pl.BlockSpec((None, TS, D), tok),
        out_shape=jax.ShapeDtypeStruct((B, T, D), F32),
        compiler_params=_cparams(("arbitrary", "arbitrary")),
        name="post_even",
    )(y_f, y_b, proj, o_f, o_b, proj, hs, mod_l, g, dsk, gw, gb, on, wo, w_in, w_out)


def _post_odd_kernel(mxd_ref, mxm_ref, h_ref, mod_ref, g_ref, wo_ref, w_in_ref, w_out_ref, o_ref):
    mix = _dot(mxd_ref[...], wo_ref[0:DIFF_W, :]) + _dot(mxm_ref[...], wo_ref[DIFF_W:D, :])
    _ffn_tail(mix, h_ref, mod_ref, g_ref, w_in_ref, w_out_ref, o_ref)


def _post_odd(mxd, mxm, hs, mod_l, g, wo, w_in, w_out, latent_only):
    nt = NTS - 1 if latent_only else NTS
    tok = lambda b, t: (b, t, 0)
    return pl.pallas_call(
        _post_odd_kernel,
        grid=(B, nt),
        in_specs=[
            pl.BlockSpec((None, TS, DIFF_W), tok),
            pl.BlockSpec((None, TS, MLA_HEADS * MLA_V), tok),
            pl.BlockSpec((None, TS, D), tok),
            _mod_spec(NTS - 1),
            _const_spec((1, D)),
            _const_spec((D, D)),
            _const_spec((D, 2 * FFN_H)),
            _const_spec((FFN_H, D)),
        ],
        out_specs=pl.BlockSpec((None, TS, D), tok),
        out_shape=jax.ShapeDtypeStruct((B, nt * TS, D), F32),
        compiler_params=_cparams(("arbitrary", "arbitrary")),
        name="post_odd",
    )(mxd, mxm, hs, mod_l, g, wo, w_in, w_out)


def _group_rms(xs, r_ref):
    ms = _dot((xs * xs).astype(BF16), r_ref)
    return xs * lax.rsqrt(ms + EPS)


def _rope(xs, cos, sin_signed, first_half):
    rot = jnp.where(first_half, pltpu.roll(xs, LANE - ROPE_DIM // 4, 1), pltpu.roll(xs, ROPE_DIM // 4, 1))
    return xs * cos + rot * sin_signed


def _odd_in_kernel(x_ref, mod_ref, g_ref, w_ref, rope_ref, gv_ref, qan_ref, kvan_ref, wuq_ref, wukv_ref,
                   rbd_ref, rfull_ref, rhalf_ref,
                   dq_ref, k1_ref, k2_ref, dv_ref, mq_ref, mk_ref, mv_ref, *, mla_scale):
    lane = lax.broadcasted_iota(jnp.int32, (1, LANE), 1)
    first_half = (lane % (ROPE_DIM // 2)) < (ROPE_DIM // 4)
    low = lane < DIFF_DIM
    rbd, rfull, rhalf = rbd_ref[...], rfull_ref[...], rhalf_ref[...]
    g_dq, g_dk = gv_ref[0:1, :], gv_ref[1:2, :]
    g_qn, g_qr, g_kn, g_kr = gv_ref[2:3, :], gv_ref[3:4, :], gv_ref[4:5, :], gv_ref[5:6, :]
    diff_scale = DIFF_DIM ** -0.5 * LOG2E
    o_cq = 3 * DIFF_W
    o_ckv = o_cq + MLA_Q_RANK
    o_kr = o_ckv + MLA_KV_RANK

    def body(n):
        a = _norm_mod(x_ref[0:n], g_ref[...], mod_ref[0:1, :], mod_ref[1:2, :])
        p = _dot(a.astype(BF16), w_ref[...])
        cos_d, sin_d, cos_m, sin_m = rope_ref[0, 0:n], rope_ref[1, 0:n], rope_ref[2, 0:n], rope_ref[3, 0:n]
        ones = jnp.ones((n, LANE), BF16)

        for h in range(DIFF_HEADS):
            sl = slice(h * LANE, (h + 1) * LANE)
            qh = _rope(_group_rms(p[:, sl], rbd) * g_dq, cos_d, sin_d, first_half)
            dq_ref[0:n, sl] = (qh * diff_scale).astype(BF16)
            kh = _rope(_group_rms(p[:, DIFF_W + h * LANE:DIFF_W + (h + 1) * LANE], rbd) * g_dk,
                       cos_d, sin_d, first_half)
            k1_ref[0:n, sl] = jnp.where(low, kh, 0.0).astype(BF16)
            k2_ref[0:n, sl] = jnp.where(low, 0.0, kh).astype(BF16)
            dv_ref[0:n, 2 * h * LANE:(2 * h + 1) * LANE] = (
                p[:, 2 * DIFF_W + h * LANE:2 * DIFF_W + (h + 1) * LANE].astype(BF16))
            dv_ref[0:n, (2 * h + 1) * LANE:(2 * h + 2) * LANE] = ones

        cq = p[:, o_cq:o_ckv]
        cqn = cq * lax.rsqrt(jnp.mean(cq * cq, axis=-1, keepdims=True) + EPS) * qan_ref[...]
        q = _dot(cqn.astype(BF16), wuq_ref[...])
        ckv = p[:, o_ckv:o_kr]
        ckvn = ckv * lax.rsqrt(jnp.mean(ckv * ckv, axis=-1, keepdims=True) + EPS) * kvan_ref[...]
        kv = _dot(ckvn.astype(BF16), wukv_ref[...])
        k_rope = _rope(_group_rms(p[:, o_kr:o_kr + LANE], rhalf) * g_kr, cos_m, sin_m, first_half).astype(BF16)
        for h in range(MLA_HEADS):
            o = 2 * LANE * h
            q_nope = _group_rms(q[:, o:o + LANE], rfull) * g_qn
            q_rope = _rope(_group_rms(q[:, o + LANE:o + 2 * LANE], rhalf) * g_qr, cos_m, sin_m, first_half)
            mq_ref[0:n, o:o + LANE] = (q_nope * mla_scale).astype(BF16)
            mq_ref[0:n, o + LANE:o + 2 * LANE] = (q_rope * mla_scale).astype(BF16)
            mk_ref[0:n, o:o + LANE] = (_group_rms(kv[:, o:o + LANE], rfull) * g_kn).astype(BF16)
            mk_ref[0:n, o + LANE:o + 2 * LANE] = k_rope
            mv_ref[0:n, o:o + LANE] = kv[:, o + LANE:o + 2 * LANE].astype(BF16)
            mv_ref[0:n, o + LANE:o + 2 * LANE] = ones

    _for_tile_rows(True, body)


def _odd_in(hs, mod_l, g, w, rope, gv, qan, kvan, wuq, wukv, rbd, rfull, rhalf):
    tok = lambda b, t: (b, t, 0)
    sds = lambda n: jax.ShapeDtypeStruct((B, T, n), BF16)
    wide = 2 * LANE * MLA_HEADS
    widths = (DIFF_W, DIFF_W, DIFF_W, wide, wide, wide, wide)
    return pl.pallas_call(
        functools.partial(_odd_in_kernel, mla_scale=(MLA_NOPE + MLA_ROPE) ** -0.5 * LOG2E),
        grid=(B, NTL + 1),
        in_specs=[
            pl.BlockSpec((None, TM, D), tok),
            _mod_spec(NTL),
            _const_spec((1, D)),
            _const_spec((D, CD_PAD)),
            pl.BlockSpec((4, TM, LANE), lambda b, t: (0, t, 0)),
            _const_spec((SUBLANE, LANE)),
            _const_spec((1, MLA_Q_RANK)),
            _const_spec((1, MLA_KV_RANK)),
            _const_spec((MLA_Q_RANK, 2 * LANE * MLA_HEADS)),
            _const_spec((MLA_KV_RANK, 2 * LANE * MLA_HEADS)),
            _const_spec((LANE, LANE)),
            _const_spec((LANE, LANE)),
            _const_spec((LANE, LANE)),
        ],
        out_specs=[pl.BlockSpec((None, TM, n), tok) for n in widths],
        out_shape=[sds(n) for n in widths],
        compiler_params=_cparams(("arbitrary", "arbitrary")),
        name="odd_in_proj",
    )(hs, mod_l, g, w, rope, gv, qan, kvan, wuq, wukv, rbd, rfull, rhalf)


def _exp2_shifted(s):
    return jnp.exp2(s - jnp.max(s, axis=-1, keepdims=True)).astype(BF16)


def _pv_normalised(e, v_aug):
    o = _dot(e, v_aug)
    return o[:, :LANE] / o[:, LANE:]


def _for_query_rows(has_ctx, body):
    _for_tile_rows(has_ctx, lambda n: body(n, slice(0, T) if n == TM else slice(SEQ, T)))


def _attn_diff_kernel(lam_ref, q_ref, k1_ref, k2_ref, v_ref, sub_ref, o_ref, *, out_scale, has_ctx):
    lam = lam_ref[0, 0]

    def body(n, keys):
        for h in range(DIFF_HEADS):
            sl = slice(h * LANE, (h + 1) * LANE)
            qh = q_ref[0:n, sl]
            e = jnp.concatenate([_exp2_shifted(_dot_nt(qh, k1_ref[keys, sl])),
                                 _exp2_shifted(_dot_nt(qh, k2_ref[keys, sl]))], axis=0)
            o12 = _pv_normalised(e, v_ref[keys, 2 * LANE * h:2 * LANE * (h + 1)])
            o = o12[0:n] - lam * o12[n:2 * n]
            o = o * lax.rsqrt(jnp.mean(o * o, axis=-1, keepdims=True) + EPS) * sub_ref[...] * out_scale
            o_ref[0:n, sl] = o.astype(BF16)

    _for_query_rows(has_ctx, body)


def _attn_mla_kernel(q_ref, k_ref, v_ref, o_ref, *, has_ctx):
    def body(n, keys):
        for h in range(MLA_HEADS):
            sq = slice(2 * LANE * h, 2 * LANE * (h + 1))
            o = _pv_normalised(_exp2_shifted(_dot_nt(q_ref[0:n, sq], k_ref[keys, sq])), v_ref[keys, sq])
            o_ref[0:n, h * LANE:(h + 1) * LANE] = o.astype(BF16)

    _for_query_rows(has_ctx, body)


def _attention(lam, dq, k1, k2, dv, mq, mk, mv, sub, out_scale, latent_only):
    has_ctx = not latent_only
    nt = NTL + 1 if has_ctx else NTL
    rows = T if has_ctx else SEQ
    wide = 2 * LANE * MLA_HEADS
    qspec = lambda n: pl.BlockSpec((None, TM, n), lambda b, t: (b, t, 0))
    kspec = lambda n: pl.BlockSpec((None, T, n), lambda b, t: (b, 0, 0))
    mxd = pl.pallas_call(
        functools.partial(_attn_diff_kernel, out_scale=out_scale, has_ctx=has_ctx),
        grid=(B, nt),
        in_specs=[pl.BlockSpec(memory_space=pltpu.SMEM), qspec(DIFF_W), kspec(DIFF_W), kspec(DIFF_W), kspec(wide),
                  _const_spec((1, LANE))],
        out_specs=qspec(DIFF_W),
        out_shape=jax.ShapeDtypeStruct((B, rows, DIFF_W), BF16),
        compiler_params=_cparams(("arbitrary", "arbitrary")),
        name="attn_diff",
    )(lam, dq, k1, k2, dv, sub)
    mxm = pl.pallas_call(
        functools.partial(_attn_mla_kernel, has_ctx=has_ctx),
        grid=(B, nt),
        in_specs=[qspec(wide), kspec(wide), kspec(wide)],
        out_specs=qspec(MLA_HEADS * MLA_V),
        out_shape=jax.ShapeDtypeStruct((B, rows, MLA_HEADS * MLA_V), BF16),
        compiler_params=_cparams(("arbitrary", "arbitrary")),
        name="attn_mla",
    )(mq, mk, mv)
    return mxd, mxm


def _rope_tables():
    n_tok = SEQ
    rows = jnp.repeat(jnp.arange(n_tok // GRID_W, dtype=jnp.int32), GRID_W)
    cols = jnp.tile(jnp.arange(GRID_W, dtype=jnp.int32), n_tok // GRID_W)
    n_freq = ROPE_DIM // 4
    inv = jnp.power(ROPE_BASE, -jnp.arange(n_freq, dtype=F32) / n_freq)
    ang_r = rows.astype(F32)[:, None] * inv
    ang_c = cols.astype(F32)[:, None] * inv
    ang = jnp.concatenate([ang_r, ang_r, ang_c, ang_c], axis=-1)
    sign = jnp.where((jnp.arange(ROPE_DIM) % (ROPE_DIM // 2)) < n_freq, -1.0, 1.0).astype(F32)
    cos = jnp.concatenate([jnp.cos(ang), jnp.ones((CTX, ROPE_DIM), F32)], axis=0)
    sin = jnp.concatenate([jnp.sin(ang) * sign, jnp.zeros((CTX, ROPE_DIM), F32)], axis=0)
    one, zero = jnp.ones_like(cos), jnp.zeros_like(sin)
    cat = lambda a, b: jnp.concatenate([a, b], axis=-1)
    return jnp.stack([cat(cos, cos), cat(sin, sin), cat(cos, one), cat(sin, zero)])


def _averaging_mats():
    idx = np.arange(LANE)
    same64 = (idx[:, None] // DIFF_DIM) == (idx[None, :] // DIFF_DIM)
    rbd = np.where(same64, 1.0 / DIFF_DIM, 0.0)
    rfull = np.full((LANE, LANE), 1.0 / LANE)
    rhalf = np.where((idx[:, None] < MLA_ROPE) & (idx[None, :] < MLA_ROPE), 1.0 / MLA_ROPE, 0.0)
    return tuple(jnp.asarray(m, BF16) for m in (rbd, rfull, rhalf))


def kernel(x, c, ctx, c_ctx, ada_w, ada_b, norm_mix, norm_ffn, w_out, ffn_w_in, ffn_w_out, ab_w_in, s5_lambda_re, s5_lambda_im, s5_log_step, s5_b_re, s5_b_im, s5_c_re, s5_c_im, s5_d, s5_glu_w, s5_glu_b, hgrn_lb_logits, hgrn_out_norm, cd_w_in, diff_lambda, diff_qk_norm, diff_subln, mla_q_a_norm, mla_kv_a_norm, mla_w_uq, mla_w_ukv, mla_nope_norm, mla_rope_norm):
    assert x.shape == (B, SEQ, D) and ctx.shape == (B, CTX, D)
    mod = _modulation(c, c_ctx, ada_w, ada_b)
    hs = jnp.concatenate([x, ctx], axis=1)
    rope = _rope_tables()
    rbd, rfull, rhalf = _averaging_mats()

    lb_p = jax.nn.softmax(hgrn_lb_logits.astype(F32), axis=0)
    lower_bounds = jnp.cumsum(lb_p, axis=0) - lb_p[0:1]

    for l in range(DEPTH):
        last = l == DEPTH - 1
        g_mix = norm_mix[l].reshape(1, D)
        g_ffn = norm_ffn[l].reshape(1, D)
        wo = w_out[l].astype(BF16)
        w1 = ffn_w_in[l].astype(BF16)
        w2 = ffn_w_out[l].astype(BF16)
        if l % 2 == 0:
            e = l // 2
            proj = _even_in(hs, mod[l], g_mix, ab_w_in[e].astype(BF16))
            s5w_in, s5w_out, a_r, a_i = _s5_params(s5_lambda_re[e], s5_lambda_im[e], s5_log_step[e],
                                                   s5_b_re[e], s5_b_im[e], s5_c_re[e], s5_c_im[e])
            y_f = _s5_scan(proj, s5w_in, s5w_out, a_r, a_i, rev=False)
            y_b = _s5_scan(proj, s5w_in, s5w_out, a_r, a_i, rev=True)
            lb = lower_bounds[e].reshape(1, HG_W)
            o_f = _hgrn_scan(proj, lb, rev=False)
            o_b = _hgrn_scan(proj, lb, rev=True)
            hs = _post_even(y_f, y_b, o_f, o_b, proj, hs, mod[l], g_ffn,
                            s5_d[e].reshape(1, S5_W), s5_glu_w[e].astype(BF16), s5_glu_b[e].reshape(1, S5_W),
                            hgrn_out_norm[e].reshape(1, HG_DIM), wo, w1, w2)
        else:
            o = l // 2
            lam_init = 0.8 - 0.6 * math.exp(-0.3 * l)
            lv = diff_lambda[o].astype(F32)
            lam = (jnp.exp(jnp.sum(lv[0] * lv[1])) - jnp.exp(jnp.sum(lv[2] * lv[3])) + lam_init).reshape(1, 1)
            w_cd = jnp.pad(cd_w_in[o], ((0, 0), (0, CD_PAD - CD_IN))).astype(BF16)
            wuq = mla_w_uq[o].reshape(MLA_Q_RANK, MLA_HEADS, MLA_NOPE + MLA_ROPE)
            wuq = jnp.pad(wuq, ((0, 0), (0, 0), (0, 2 * LANE - MLA_NOPE - MLA_ROPE)))
            wuq = wuq.reshape(MLA_Q_RANK, 2 * LANE * MLA_HEADS).astype(BF16)
            pad_r = lambda v: jnp.pad(v, (0, LANE - MLA_ROPE))
            gv = jnp.zeros((SUBLANE, LANE), F32)
            gv = gv.at[0].set(jnp.tile(diff_qk_norm[o, 0], 2)).at[1].set(jnp.tile(diff_qk_norm[o, 1], 2))
            gv = gv.at[2].set(mla_nope_norm[o, 0]).at[3].set(pad_r(mla_rope_norm[o, 0]))
            gv = gv.at[4].set(mla_nope_norm[o, 1]).at[5].set(pad_r(mla_rope_norm[o, 1]))
            parts = _odd_in(hs, mod[l], g_mix, w_cd, rope, gv, mla_q_a_norm[o].reshape(1, MLA_Q_RANK),
                            mla_kv_a_norm[o].reshape(1, MLA_KV_RANK), wuq, mla_w_ukv[o].astype(BF16),
                            rbd, rfull, rhalf)
            mxd, mxm = _attention(lam, *parts, diff_subln[o].reshape(1, LANE), 1.0 - lam_init, latent_only=last)
            hs = _post_odd(mxd, mxm, hs, mod[l], g_ffn, wo, w1, w2, latent_only=last)
    return hs
```

```python
import functools
import math

import numpy as np
import jax
import jax.numpy as jnp
from jax import lax
from jax.experimental import pallas as pl
from jax.experimental.pallas import tpu as pltpu

F32 = jnp.float32
BF16 = jnp.bfloat16

D = 1024
B = 8
SEQ = 2048
CTX = 256
T = CTX + SEQ
DEPTH = 4
GRID_W = 64
FFN_H = ((8 * D + 3 * 256 - 1) // (3 * 256)) * 256
S5_W = D // 2
S5_GROUP = 16
S5_GROUPS = S5_W // S5_GROUP
S5_STATE = 64
HG_HEADS = 4
HG_DIM = D // 8
HG_W = HG_HEADS * HG_DIM
MAX_EXP_ARG = 60.0
DIFF_HEADS = 4
DIFF_DIM = D // 16
DIFF_W = DIFF_HEADS * 2 * DIFF_DIM
MLA_HEADS = 4
MLA_NOPE = D // 8
MLA_ROPE = D // 16
MLA_V = D // 8
MLA_Q_RANK = 3 * D // 8
MLA_KV_RANK = D // 4
ROPE_DIM = D // 16
ROPE_BASE = 10000.0
EPS = 1e-6
AB_IN = S5_W + 5 * HG_W
CD_IN = 3 * DIFF_W + MLA_Q_RANK + MLA_KV_RANK + MLA_ROPE
CD_PAD = CD_IN + 64

LANE = 128
SUBLANE = 8
TS = CTX
NTS = T // TS
TM = 512
NTL = SEQ // TM
MOD_ROWS = 16
CTX_ROW = B
S5_LC = 512
S5_NC = S5_GROUPS * S5_STATE // S5_LC
S5_UC = S5_LC // S5_STATE * S5_GROUP
S5_SB = 32
HG_C = TS
VMEM_LIMIT = 56 * 1024 * 1024
LOG2E = math.log2(math.e)


def _cparams(sem):
    return pltpu.CompilerParams(dimension_semantics=sem, vmem_limit_bytes=VMEM_LIMIT)


def _const_spec(shape):
    n = len(shape)
    return pl.BlockSpec(shape, lambda *_: (0,) * n, pipeline_mode=pl.Buffered(1))


def _silu(x):
    return x * jax.nn.sigmoid(x)


def _norm_mod(x, g, shift, scale):
    ms = jnp.mean(x * x, axis=-1, keepdims=True)
    return x * lax.rsqrt(ms + EPS) * g * (1.0 + scale) + shift


def _dot(a, b):
    return jnp.dot(a, b, preferred_element_type=F32)


def _dot_nt(a, b):
    return lax.dot_general(a, b, (((1,), (1,)), ((), ())), preferred_element_type=F32)


def _mod_kernel(s_ref, w_ref, b_ref, o_ref):
    s = _silu(s_ref[...])
    o_ref[...] = _dot(s.astype(BF16), w_ref[...].astype(BF16)) + b_ref[...]


def _modulation(c, c_ctx, ada_w, ada_b):
    s = jnp.zeros((MOD_ROWS, D), F32).at[:B].set(c).at[CTX_ROW].set(c_ctx)
    nb = 1536
    out = pl.pallas_call(
        _mod_kernel,
        grid=(DEPTH, 6 * D // nb),
        in_specs=[
            pl.BlockSpec((MOD_ROWS, D), lambda l, n: (0, 0)),
            pl.BlockSpec((None, D, nb), lambda l, n: (l, 0, n)),
            pl.BlockSpec((None, 1, nb), lambda l, n: (l, 0, n)),
        ],
        out_specs=pl.BlockSpec((None, MOD_ROWS, nb), lambda l, n: (l, 0, n)),
        out_shape=jax.ShapeDtypeStruct((DEPTH, MOD_ROWS, 6 * D), F32),
        compiler_params=_cparams(("arbitrary", "arbitrary")),
        name="adaln_mod",
    )(s, ada_w, ada_b.reshape(DEPTH, 1, 6 * D))
    return out.reshape(DEPTH, MOD_ROWS, 6, D)


def _mod_spec(ctx_tile):
    return pl.BlockSpec((None, 6, D), lambda b, t: (jnp.where(t == ctx_tile, CTX_ROW, b), 0, 0))


def _for_tile_rows(has_ctx, body):
    if has_ctx:
        t = pl.program_id(1)
        pl.when(t < NTL)(lambda: body(TM))
        pl.when(t == NTL)(lambda: body(CTX))
    else:
        body(TM)


def _even_in_kernel(x_ref, mod_ref, g_ref, w_ref, p_ref):
    def body(n):
        a = _norm_mod(x_ref[0:n], g_ref[...], mod_ref[0:1, :], mod_ref[1:2, :])
        p_ref[0:n] = _dot(a.astype(BF16), w_ref[...])

    _for_tile_rows(True, body)


def _even_in(hs, mod_l, g, w):
    return pl.pallas_call(
        _even_in_kernel,
        grid=(B, NTL + 1),
        in_specs=[
            pl.BlockSpec((None, TM, D), lambda b, t: (b, t, 0)),
            _mod_spec(NTL),
            _const_spec((1, D)),
            _const_spec((D, AB_IN)),
        ],
        out_specs=pl.BlockSpec((None, TM, AB_IN), lambda b, t: (b, t, 0)),
        out_shape=jax.ShapeDtypeStruct((B, T, AB_IN), F32),
        compiler_params=_cparams(("arbitrary", "arbitrary")),
        name="even_in_proj",
    )(hs, mod_l, g, w)


def _s5_kernel(u_ref, w_ref, c_ref, ar_ref, ai_ref, y_ref, ux_ref, yx_ref, bu_ref, h_ref, *, rev):
    i = pl.program_id(1)

    @pl.when(i == 0)
    def _():
        h_ref[...] = jnp.zeros_like(h_ref)

    ar = ar_ref[...]
    ai = ai_ref[...]
    hr, hi = h_ref[0], h_ref[1]
    n_sb = TS // S5_SB
    order = list(reversed(range(n_sb))) if rev else list(range(n_sb))
    for b in range(B):
        ux_ref[pl.ds(b, TS, stride=B), :] = u_ref[b]

    def rows_of(sb):
        return slice(sb * S5_SB * B, (sb + 1) * S5_SB * B)

    def project(sb):
        bu_ref[rows_of(sb), :] = _dot(ux_ref[rows_of(sb), :].astype(BF16), w_ref[...])

    project(order[0])
    for n, sb in enumerate(order):
        if n + 1 < n_sb:
            project(order[n + 1])
        rows = rows_of(sb)
        for t in (reversed(range(S5_SB)) if rev else range(S5_SB)):
            r = slice(rows.start + t * B, rows.start + (t + 1) * B)
            nr = ar * hr - ai * hi + bu_ref[r, 0:S5_LC]
            ni = ar * hi + ai * hr + bu_ref[r, S5_LC:2 * S5_LC]
            bu_ref[r, 0:S5_LC] = nr
            bu_ref[r, S5_LC:2 * S5_LC] = ni
            hr, hi = nr, ni
        yx_ref[rows, :] = _dot(bu_ref[rows, :].astype(BF16), c_ref[...])
    h_ref[0] = hr
    h_ref[1] = hi
    for b in range(B):
        y_ref[b] = yx_ref[pl.ds(b, TS, stride=B), :]


def _seq_tile(d, i):
    return jnp.where(i == 0, NTS - 1, jnp.where(d == 0, i - 1, NTS - 1 - i))


def _s5_scan(proj, w_in, w_out, a_r, a_i, rev):
    d = 1 if rev else 0
    return pl.pallas_call(
        functools.partial(_s5_kernel, rev=rev),
        grid=(S5_NC, NTS),
        in_specs=[
            pl.BlockSpec((B, TS, S5_UC), lambda c, i: (0, _seq_tile(d, i), c)),
            pl.BlockSpec((None, None, S5_UC, 2 * S5_LC), lambda c, i: (d, c, 0, 0)),
            pl.BlockSpec((None, None, 2 * S5_LC, S5_UC), lambda c, i: (d, c, 0, 0)),
            pl.BlockSpec((None, None, B, S5_LC), lambda c, i: (d, c, 0, 0)),
            pl.BlockSpec((None, None, B, S5_LC), lambda c, i: (d, c, 0, 0)),
        ],
        out_specs=pl.BlockSpec((B, TS, S5_UC), lambda c, i: (0, _seq_tile(d, i), c)),
        out_shape=jax.ShapeDtypeStruct((B, T, S5_W), F32),
        scratch_shapes=[
            pltpu.VMEM((TS * B, S5_UC), F32),
            pltpu.VMEM((TS * B, S5_UC), F32),
            pltpu.VMEM((TS * B, 2 * S5_LC), F32),
            pltpu.VMEM((2, B, S5_LC), F32),
        ],
        compiler_params=_cparams(("arbitrary", "arbitrary")),
        name="s5_bwd" if rev else "s5_fwd",
    )(proj, w_in, w_out, a_r, a_i)


def _s5_params(lam_re, lam_im, log_step, b_re, b_im, c_re, c_im):
    lr = jnp.minimum(lam_re.astype(F32), -1e-4)
    li = lam_im.astype(F32)
    step = jnp.exp(log_step.astype(F32))[..., None]
    mag = jnp.exp(lr * step)
    a_r = mag * jnp.cos(li * step)
    a_i = mag * jnp.sin(li * step)
    den = lr * lr + li * li
    coef_r = ((a_r - 1) * lr + a_i * li) / den
    coef_i = (a_i * lr - (a_r - 1) * li) / den
    br = b_re.astype(F32)
    bi = b_im.astype(F32)
    bb_r = coef_r[..., None] * br - coef_i[..., None] * bi
    bb_i = coef_r[..., None] * bi + coef_i[..., None] * br
    gpc = S5_LC // S5_STATE
    eye = jnp.eye(gpc, dtype=F32)

    def in_blocks(bb):
        bb = bb.reshape(2, S5_NC, gpc, S5_STATE, S5_GROUP)
        return jnp.einsum('dngpc,gh->dngchp', bb, eye).reshape(2, S5_NC, S5_UC, S5_LC)

    def out_blocks(cc):
        cc = cc.reshape(2, S5_NC, gpc, S5_GROUP, S5_STATE)
        return jnp.einsum('dngcp,gh->dngphc', cc, eye).reshape(2, S5_NC, S5_LC, S5_UC)

    w_in = jnp.concatenate([in_blocks(bb_r), in_blocks(bb_i)], axis=-1).astype(BF16)
    w_out = jnp.concatenate([out_blocks(c_re.astype(F32)), -out_blocks(c_im.astype(F32))], axis=-2).astype(BF16)

    def lanes(a):
        return jnp.broadcast_to(a.reshape(2, S5_NC, 1, S5_LC), (2, S5_NC, B, S5_LC))

    return w_in, w_out, lanes(a_r), lanes(a_i)


HG_HALF = HG_C // 2
HG_LEVELS = tuple(2 ** e for e in range(int(math.log2(HG_HALF))))


def _hgrn_maps(rev):
    pos = np.arange(HG_C)
    ph = np.arange(HG_HALF)
    if rev:
        pos = HG_C - 1 - pos
        ph = HG_HALF - 1 - ph
    tri = (pos[None, :] <= pos[:, None]).astype(np.float32)
    pt, ps = ph[:, None], ph[None, :]
    lv = np.full((HG_HALF, HG_HALF), -1, np.int32)
    lv[pt == ps] = len(HG_LEVELS)
    for e, m in enumerate(HG_LEVELS):
        x, y = pt // m, ps // m
        lv[(x == y + 1) & (x % 2 == 1)] = e
    return jnp.asarray(tri, BF16), jnp.asarray(lv)


def _hgrn_kernel(q_ref, f_ref, v_ref, lb_ref, tri_ref, lv_ref, o_ref, s_ref, *, rev):
    i = pl.program_id(1)

    @pl.when(i == 0)
    def _():
        s_ref[...] = jnp.zeros_like(s_ref)

    c = HG_C
    fl = f_ref[...]
    lb = lb_ref[...]
    qb = _silu(q_ref[...]).astype(BF16)
    v = v_ref[...]
    x2 = fl * LOG2E
    a = jnp.exp2(-jnp.abs(x2))
    s1 = 1.0 + a
    r = 1.0 / s1
    kb = ((1.0 - lb) * jnp.where(x2 >= 0.0, a * r, r)).astype(BF16)
    t2 = jnp.exp2(jnp.minimum(-x2, MAX_EXP_ARG * LOG2E))
    lf = jnp.minimum(x2, 0.0) - jnp.log2(s1) + jnp.log2(1.0 + lb * t2)

    row = lax.broadcasted_iota(jnp.int32, (c, 1), 0)
    pos = (c - 1 - row) if rev else row
    first, second = (slice(HG_HALF, c), slice(0, HG_HALF)) if rev else (slice(0, HG_HALF), slice(HG_HALF, c))
    halves = (first, second)

    def from_earlier(x, j):
        return pltpu.roll(x, (c - j) if rev else j, 0)

    def from_later(x, j):
        return pltpu.roll(x, j if rev else (c - j), 0)

    def head(x, h, rows):
        return x[rows, h * HG_DIM:(h + 1) * HG_DIM]

    hi = lf.astype(BF16)
    r1 = lf - hi.astype(F32)
    mid = r1.astype(BF16)
    lo = (r1 - mid.astype(F32)).astype(BF16)
    tri = tri_ref[...]
    cum = _dot(tri, hi) + _dot(tri, mid) + _dot(tri, lo)

    lv = lv_ref[...]
    diag = len(HG_LEVELS)
    scores = [[jnp.where(lv == diag, _dot_nt(head(qb, h, r), head(kb, h, r)), 0.0) for r in halves]
              for h in range(HG_HEADS)]

    g_end = cum
    for e, m in enumerate(HG_LEVELS):
        qm = qb * jnp.exp2(cum - from_earlier(g_end, m)).astype(BF16)
        km = kb * jnp.exp2(g_end - cum).astype(BF16)
        for h in range(HG_HEADS):
            for j, r in enumerate(halves):
                scores[h][j] = jnp.where(lv == e, _dot_nt(head(qm, h, r), head(km, h, r)), scores[h][j])
        g_end = jnp.where((pos & m) != 0, g_end, from_later(g_end, m))
    qm = qb * jnp.exp2(cum - from_earlier(g_end, HG_HALF)).astype(BF16)
    km = kb * jnp.exp2(g_end - cum).astype(BF16)
    cross = [_dot_nt(head(qm, h, second), head(km, h, first)) for h in range(HG_HEADS)]
    g_end = jnp.where((pos & HG_HALF) != 0, g_end, from_later(g_end, HG_HALF))

    q_in = qb * jnp.exp2(cum).astype(BF16)
    k_out = kb * jnp.exp2(g_end - cum).astype(BF16)
    decay = jnp.exp2(g_end[0:1, :])
    vb = v.astype(BF16)
    for h in range(HG_HEADS):
        sl = slice(h * HG_DIM, (h + 1) * HG_DIM)
        st = s_ref[h]
        inter = _dot_nt(q_in[:, sl], st.astype(BF16))
        v_first, v_second = head(vb, h, first), head(vb, h, second)
        o_ref[first, sl] = inter[first] + _dot(scores[h][0].astype(BF16), v_first)
        o_ref[second, sl] = (inter[second] + _dot(cross[h].astype(BF16), v_first)
                             + _dot(scores[h][1].astype(BF16), v_second))
        s_ref[h] = decay[:, sl] * st + _dot(v[:, sl].T.astype(BF16), k_out[:, sl])


def _hgrn_scan(hg, lb, rev):
    d = 1 if rev else 0
    tri, lvl = _hgrn_maps(rev)
    return pl.pallas_call(
        functools.partial(_hgrn_kernel, rev=rev),
        grid=(B, NTS),
        in_specs=[
            pl.BlockSpec((None, HG_C, HG_W), lambda b, i: (b, _seq_tile(d, i), 1)),
            pl.BlockSpec((None, HG_C, HG_W), lambda b, i: (b, _seq_tile(d, i), 2 + d)),
            pl.BlockSpec((None, HG_C, HG_W), lambda b, i: (b, _seq_tile(d, i), 4)),
            _const_spec((1, HG_W)),
            _const_spec((HG_C, HG_C)),
            _const_spec((HG_HALF, HG_HALF)),
        ],
        out_specs=pl.BlockSpec((None, HG_C, HG_W), lambda b, i: (b, _seq_tile(d, i), 0)),
        out_shape=jax.ShapeDtypeStruct((B, T, HG_W), F32),
        scratch_shapes=[pltpu.VMEM((HG_HEADS, HG_DIM, HG_DIM), F32)],
        compiler_params=_cparams(("arbitrary", "arbitrary")),
        name="hgrn_bwd" if rev else "hgrn_fwd",
    )(hg, hg, hg, lb, tri, lvl)


def _ffn_tail(mix_out, h_ref, mod_ref, g_ref, w_in_ref, w_out_ref, o_ref):
    h1 = h_ref[...] + mod_ref[2:3, :] * mix_out
    a = _norm_mod(h1, g_ref[...], mod_ref[3:4, :], mod_ref[4:5, :])
    gu = _dot(a.astype(BF16), w_in_ref[...])
    act = _silu(gu[:, :FFN_H]) * gu[:, FFN_H:]
    o_ref[...] = h1 + mod_ref[5:6, :] * _dot(act.astype(BF16), w_out_ref[...])


def _gelu_tanh(x):
    return 0.5 * x * (1.0 + jnp.tanh(math.sqrt(2.0 / math.pi) * (x + 0.044715 * (x * x * x))))


def _post_even_kernel(yf_ref, yb_ref, u_ref, of_ref, ob_ref, gate_ref, h_ref, mod_ref, g_ref,
                      dsk_ref, gw_ref, gb_ref, on_ref, wo_ref, w_in_ref, w_out_ref, o_ref):
    y = yf_ref[...] + yb_ref[...] + u_ref[...] * dsk_ref[...]
    z = _gelu_tanh(y)
    s5 = z * jax.nn.sigmoid(_dot(z.astype(BF16), gw_ref[...]) + gb_ref[...])
    mix = _dot(s5.astype(BF16), wo_ref[0:S5_W, :])
    o = of_ref[...] + ob_ref[...]
    gate = _silu(gate_ref[...])
    for h in range(HG_HEADS):
        sl = slice(h * HG_DIM, (h + 1) * HG_DIM)
        oh = o[:, sl]
        ms = jnp.mean(oh * oh, axis=-1, keepdims=True)
        hn = oh * lax.rsqrt(ms + EPS) * on_ref[...] * gate[:, sl]
        mix = mix + _dot(hn.astype(BF16), wo_ref[S5_W + h * HG_DIM:S5_W + (h + 1) * HG_DIM, :])
    _ffn_tail(mix, h_ref, mod_ref, g_ref, w_in_ref, w_out_ref, o_ref)


def _post_even(y_f, y_b, o_f, o_b, proj, hs, mod_l, g, dsk, gw, gb, on, wo, w_in, w_out):
    tok = lambda b, t: (b, t, 0)
    return pl.pallas_call(
        _post_even_kernel,
        grid=(B, NTS),
        in_specs=[
            pl.BlockSpec((None, TS, S5_W), tok),
            pl.BlockSpec((None, TS, S5_W), tok),
            pl.BlockSpec((None, TS, S5_W), tok),
            pl.BlockSpec((None, TS, HG_W), tok),
            pl.BlockSpec((None, TS, HG_W), tok),
            pl.BlockSpec((None, TS, HG_W), lambda b, t: (b, t, 5)),
            pl.BlockSpec((None, TS, D), tok),
            _mod_spec(NTS - 1),
            _const_spec((1, D)),
            _const_spec((1, S5_W)),
            _const_spec((S5_W, S5_W)),
            _const_spec((1, S5_W)),
            _const_spec((1, HG_DIM)),
            _const_spec((D, D)),
            _const_spec((D, 2 * FFN_H)),
            _const_spec((FFN_H, D)),
        ],
        out_specs=pl.BlockSpec((None, TS, D), tok),
        out_shape=jax.ShapeDtypeStruct((B, T, D), F32),
        compiler_params=_cparams(("arbitrary", "arbitrary")),
        name="post_even",
    )(y_f, y_b, proj, o_f, o_b, proj, hs, mod_l, g, dsk, gw, gb, on, wo, w_in, w_out)


def _post_odd_kernel(mx_ref, h_ref, mod_ref, g_ref, wo_ref, w_in_ref, w_out_ref, o_ref):
    _ffn_tail(_dot(mx_ref[...], wo_ref[...]), h_ref, mod_ref, g_ref, w_in_ref, w_out_ref, o_ref)


def _post_odd(mx, hs, mod_l, g, wo, w_in, w_out, latent_only):
    nt = NTS - 1 if latent_only else NTS
    tok = lambda b, t: (b, t, 0)
    return pl.pallas_call(
        _post_odd_kernel,
        grid=(B, nt),
        in_specs=[
            pl.BlockSpec((None, TS, D), tok),
            pl.BlockSpec((None, TS, D), tok),
            _mod_spec(NTS - 1),
            _const_spec((1, D)),
            _const_spec((D, D)),
            _const_spec((D, 2 * FFN_H)),
            _const_spec((FFN_H, D)),
        ],
        out_specs=pl.BlockSpec((None, TS, D), tok),
        out_shape=jax.ShapeDtypeStruct((B, nt * TS, D), F32),
        compiler_params=_cparams(("arbitrary", "arbitrary")),
        name="post_odd",
    )(mx, hs, mod_l, g, wo, w_in, w_out)


def _group_rms(xs, r):
    ms = _dot((xs * xs).astype(BF16), r)
    return xs * lax.rsqrt(ms + EPS)


def _rope(xs, cos, sin_signed, first_half):
    rot = jnp.where(first_half, pltpu.roll(xs, LANE - ROPE_DIM // 4, 1), pltpu.roll(xs, ROPE_DIM // 4, 1))
    return xs * cos + rot * sin_signed


def _odd_in_kernel(x_ref, mod_ref, g_ref, w_ref, rope_ref, gv_ref, qan_ref, kvan_ref, wuq_ref, wukv_ref,
                   rpair_ref, rq_ref, rfull_ref, rhalf_ref,
                   dq_ref, k1_ref, k2_ref, dv_ref, mq_ref, mk_ref, mv_ref, *, mla_scale):
    lane = lax.broadcasted_iota(jnp.int32, (1, LANE), 1)
    first_half = (lane % (ROPE_DIM // 2)) < (ROPE_DIM // 4)
    low = lane < DIFF_DIM
    rpair, rq, rfull, rhalf = rpair_ref[...], rq_ref[...], rfull_ref[...], rhalf_ref[...]
    g_dq, g_dk = gv_ref[0:1, :], gv_ref[1:2, :]
    g_qn, g_qr, g_kn, g_kr = gv_ref[2:3, :], gv_ref[3:4, :], gv_ref[4:5, :], gv_ref[5:6, :]
    diff_scale = DIFF_DIM ** -0.5 * LOG2E
    o_cq = 3 * DIFF_W
    o_ckv = o_cq + MLA_Q_RANK
    o_kr = o_ckv + MLA_KV_RANK

    def body(n):
        a = _norm_mod(x_ref[0:n], g_ref[...], mod_ref[0:1, :], mod_ref[1:2, :])
        p = _dot(a.astype(BF16), w_ref[...])
        cos_d, sin_d, cos_m, sin_m = rope_ref[0, 0:n], rope_ref[1, 0:n], rope_ref[2, 0:n], rope_ref[3, 0:n]
        ones = jnp.ones((n, LANE), BF16)

        for h2 in range(0, DIFF_HEADS, 2):
            pair = slice(h2 * LANE, (h2 + 2) * LANE)
            qn = _group_rms(p[:, pair], rpair)
            kn = _group_rms(p[:, DIFF_W + pair.start:DIFF_W + pair.stop], rpair)
            for j in range(2):
                sl = slice((h2 + j) * LANE, (h2 + j + 1) * LANE)
                half = slice(j * LANE, (j + 1) * LANE)
                qh = _rope(qn[:, half] * g_dq, cos_d, sin_d, first_half)
                dq_ref[0:n, sl] = (qh * diff_scale).astype(BF16)
                kh = _rope(kn[:, half] * g_dk, cos_d, sin_d, first_half)
                k1_ref[0:n, sl] = jnp.where(low, kh, 0.0).astype(BF16)
                k2_ref[0:n, sl] = jnp.where(low, 0.0, kh).astype(BF16)
        for h in range(DIFF_HEADS):
            dv_ref[0:n, 2 * h * LANE:(2 * h + 1) * LANE] = (
                p[:, 2 * DIFF_W + h * LANE:2 * DIFF_W + (h + 1) * LANE].astype(BF16))
            dv_ref[0:n, (2 * h + 1) * LANE:(2 * h + 2) * LANE] = ones

        cq = p[:, o_cq:o_ckv]
        cqn = cq * lax.rsqrt(jnp.mean(cq * cq, axis=-1, keepdims=True) + EPS) * qan_ref[...]
        q = _dot(cqn.astype(BF16), wuq_ref[...])
        ckv = p[:, o_ckv:o_kr]
        ckvn = ckv * lax.rsqrt(jnp.mean(ckv * ckv, axis=-1, keepdims=True) + EPS) * kvan_ref[...]
        kv = _dot(ckvn.astype(BF16), wukv_ref[...])
        k_rope = _rope(_group_rms(p[:, o_kr:o_kr + LANE], rhalf) * g_kr, cos_m, sin_m, first_half).astype(BF16)
        for h in range(MLA_HEADS):
            o = 2 * LANE * h
            qn = _group_rms(q[:, o:o + 2 * LANE], rq)
            q_nope = qn[:, :LANE] * g_qn
            q_rope = _rope(qn[:, LANE:] * g_qr, cos_m, sin_m, first_half)
            mq_ref[0:n, o:o + LANE] = (q_nope * mla_scale).astype(BF16)
            mq_ref[0:n, o + LANE:o + 2 * LANE] = (q_rope * mla_scale).astype(BF16)
            mk_ref[0:n, o:o + LANE] = (_group_rms(kv[:, o:o + LANE], rfull) * g_kn).astype(BF16)
            mk_ref[0:n, o + LANE:o + 2 * LANE] = k_rope
            mv_ref[0:n, o:o + LANE] = kv[:, o + LANE:o + 2 * LANE].astype(BF16)
            mv_ref[0:n, o + LANE:o + 2 * LANE] = ones

    _for_tile_rows(True, body)


def _odd_in(hs, mod_l, g, w, rope, gv, qan, kvan, wuq, wukv, rmats):
    tok = lambda b, t: (b, t, 0)
    sds = lambda n: jax.ShapeDtypeStruct((B, T, n), BF16)
    wide = 2 * LANE * MLA_HEADS
    widths = (DIFF_W, DIFF_W, DIFF_W, wide, wide, wide, wide)
    return pl.pallas_call(
        functools.partial(_odd_in_kernel, mla_scale=(MLA_NOPE + MLA_ROPE) ** -0.5 * LOG2E),
        grid=(B, NTL + 1),
        in_specs=[
            pl.BlockSpec((None, TM, D), tok),
            _mod_spec(NTL),
            _const_spec((1, D)),
            _const_spec((D, CD_PAD)),
            pl.BlockSpec((4, TM, LANE), lambda b, t: (0, t, 0)),
            _const_spec((SUBLANE, LANE)),
            _const_spec((1, MLA_Q_RANK)),
            _const_spec((1, MLA_KV_RANK)),
            _const_spec((MLA_Q_RANK, 2 * LANE * MLA_HEADS)),
            _const_spec((MLA_KV_RANK, 2 * LANE * MLA_HEADS)),
            _const_spec((2 * LANE, 2 * LANE)),
            _const_spec((2 * LANE, 2 * LANE)),
            _const_spec((LANE, LANE)),
            _const_spec((LANE, LANE)),
        ],
        out_specs=[pl.BlockSpec((None, TM, n), tok) for n in widths],
        out_shape=[sds(n) for n in widths],
        compiler_params=_cparams(("arbitrary", "arbitrary")),
        name="odd_in_proj",
    )(hs, mod_l, g, w, rope, gv, qan, kvan, wuq, wukv, *rmats)


def _exp2_shifted(s):
    return jnp.exp2(s - jnp.max(s, axis=-1, keepdims=True)).astype(BF16)


def _pv_normalised(e, v_aug):
    o = _dot(e, v_aug)
    return o[:, :LANE] / o[:, LANE:]


def _attn_kernel(lam_ref, dq_ref, k1_ref, k2_ref, dv_ref, mq_ref, mk_ref, mv_ref, sub_ref, o_ref,
                 *, out_scale, has_ctx):
    lam = lam_ref[0, 0]

    def body(keys):
        for h in range(DIFF_HEADS):
            sl = slice(h * LANE, (h + 1) * LANE)
            sv = slice(2 * LANE * h, 2 * LANE * (h + 1))
            qh = dq_ref[:, sl]
            o1 = _pv_normalised(_exp2_shifted(_dot_nt(qh, k1_ref[keys, sl])), dv_ref[keys, sv])
            o2 = _pv_normalised(_exp2_shifted(_dot_nt(qh, k2_ref[keys, sl])), dv_ref[keys, sv])
            o = o1 - lam * o2
            o = o * lax.rsqrt(jnp.mean(o * o, axis=-1, keepdims=True) + EPS) * sub_ref[...] * out_scale
            o_ref[:, sl] = o.astype(BF16)
        for h in range(MLA_HEADS):
            sq = slice(2 * LANE * h, 2 * LANE * (h + 1))
            o = _pv_normalised(_exp2_shifted(_dot_nt(mq_ref[:, sq], mk_ref[keys, sq])), mv_ref[keys, sq])
            o_ref[:, DIFF_W + h * LANE:DIFF_W + (h + 1) * LANE] = o.astype(BF16)

    if has_ctx:
        t = pl.program_id(1)
        pl.when(t < NTS - 1)(lambda: body(slice(0, T)))
        pl.when(t == NTS - 1)(lambda: body(slice(SEQ, T)))
    else:
        body(slice(0, T))


def _attention(lam, dq, k1, k2, dv, mq, mk, mv, sub, out_scale, latent_only):
    has_ctx = not latent_only
    nt = NTS if has_ctx else NTS - 1
    wide = 2 * LANE * MLA_HEADS
    qspec = lambda n: pl.BlockSpec((None, TS, n), lambda b, t: (b, t, 0))
    kspec = lambda n: pl.BlockSpec((None, T, n), lambda b, t: (b, 0, 0))
    return pl.pallas_call(
        functools.partial(_attn_kernel, out_scale=out_scale, has_ctx=has_ctx),
        grid=(B, nt),
        in_specs=[
            pl.BlockSpec(memory_space=pltpu.SMEM),
            qspec(DIFF_W), kspec(DIFF_W), kspec(DIFF_W), kspec(wide),
            qspec(wide), kspec(wide), kspec(wide),
            _const_spec((1, LANE)),
        ],
        out_specs=qspec(D),
        out_shape=jax.ShapeDtypeStruct((B, nt * TS, D), BF16),
        compiler_params=_cparams(("arbitrary", "arbitrary")),
        name="attention",
    )(lam, dq, k1, k2, dv, mq, mk, mv, sub)


def _rope_tables():
    n_tok = SEQ
    rows = jnp.repeat(jnp.arange(n_tok // GRID_W, dtype=jnp.int32), GRID_W)
    cols = jnp.tile(jnp.arange(GRID_W, dtype=jnp.int32), n_tok // GRID_W)
    n_freq = ROPE_DIM // 4
    inv = jnp.power(ROPE_BASE, -jnp.arange(n_freq, dtype=F32) / n_freq)
    ang_r = rows.astype(F32)[:, None] * inv
    ang_c = cols.astype(F32)[:, None] * inv
    ang = jnp.concatenate([ang_r, ang_r, ang_c, ang_c], axis=-1)
    sign = jnp.where((jnp.arange(ROPE_DIM) % (ROPE_DIM // 2)) < n_freq, -1.0, 1.0).astype(F32)
    cos = jnp.concatenate([jnp.cos(ang), jnp.ones((CTX, ROPE_DIM), F32)], axis=0)
    sin = jnp.concatenate([jnp.sin(ang) * sign, jnp.zeros((CTX, ROPE_DIM), F32)], axis=0)
    one, zero = jnp.ones_like(cos), jnp.zeros_like(sin)
    cat = lambda a, b: jnp.concatenate([a, b], axis=-1)
    return jnp.stack([cat(cos, cos), cat(sin, sin), cat(cos, one), cat(sin, zero)])


def _averaging_mats():
    idx = np.arange(2 * LANE)
    same64 = (idx[:, None] // DIFF_DIM) == (idx[None, :] // DIFF_DIM)
    rpair = np.where(same64, 1.0 / DIFF_DIM, 0.0)
    rfull = np.full((LANE, LANE), 1.0 / LANE)
    il = np.arange(LANE)
    rhalf = np.where((il[:, None] < MLA_ROPE) & (il[None, :] < MLA_ROPE), 1.0 / MLA_ROPE, 0.0)
    rq = np.zeros((2 * LANE, 2 * LANE))
    rq[:LANE, :LANE] = rfull
    rq[LANE:, LANE:] = rhalf
    return tuple(jnp.asarray(m, BF16) for m in (rpair, rq, rfull, rhalf))


def kernel(x, c, ctx, c_ctx, ada_w, ada_b, norm_mix, norm_ffn, w_out, ffn_w_in, ffn_w_out, ab_w_in, s5_lambda_re, s5_lambda_im, s5_log_step, s5_b_re, s5_b_im, s5_c_re, s5_c_im, s5_d, s5_glu_w, s5_glu_b, hgrn_lb_logits, hgrn_out_norm, cd_w_in, diff_lambda, diff_qk_norm, diff_subln, mla_q_a_norm, mla_kv_a_norm, mla_w_uq, mla_w_ukv, mla_nope_norm, mla_rope_norm):
    assert x.shape == (B, SEQ, D) and ctx.shape == (B, CTX, D)
    mod = _modulation(c, c_ctx, ada_w, ada_b)
    hs = jnp.concatenate([x, ctx], axis=1)
    rope = _rope_tables()
    rmats = _averaging_mats()

    lb_p = jax.nn.softmax(hgrn_lb_logits.astype(F32), axis=0)
    lower_bounds = jnp.cumsum(lb_p, axis=0) - lb_p[0:1]

    for l in range(DEPTH):
        last = l == DEPTH - 1
        g_mix = norm_mix[l].reshape(1, D)
        g_ffn = norm_ffn[l].reshape(1, D)
        wo = w_out[l].astype(BF16)
        w1 = ffn_w_in[l].astype(BF16)
        w2 = ffn_w_out[l].astype(BF16)
        if l % 2 == 0:
            e = l // 2
            proj = _even_in(hs, mod[l], g_mix, ab_w_in[e].astype(BF16))
            s5w_in, s5w_out, a_r, a_i = _s5_params(s5_lambda_re[e], s5_lambda_im[e], s5_log_step[e],
                                                   s5_b_re[e], s5_b_im[e], s5_c_re[e], s5_c_im[e])
            y_f = _s5_scan(proj, s5w_in, s5w_out, a_r, a_i, rev=False)
            y_b = _s5_scan(proj, s5w_in, s5w_out, a_r, a_i, rev=True)
            lb = lower_bounds[e].reshape(1, HG_W)
            o_f = _hgrn_scan(proj, lb, rev=False)
            o_b = _hgrn_scan(proj, lb, rev=True)
            hs = _post_even(y_f, y_b, o_f, o_b, proj, hs, mod[l], g_ffn,
                            s5_d[e].reshape(1, S5_W), s5_glu_w[e].astype(BF16), s5_glu_b[e].reshape(1, S5_W),
                            hgrn_out_norm[e].reshape(1, HG_DIM), wo, w1, w2)
        else:
            o = l // 2
            lam_init = 0.8 - 0.6 * math.exp(-0.3 * l)
            lv = diff_lambda[o].astype(F32)
            lam = (jnp.exp(jnp.sum(lv[0] * lv[1])) - jnp.exp(jnp.sum(lv[2] * lv[3])) + lam_init).reshape(1, 1)
            w_cd = jnp.pad(cd_w_in[o], ((0, 0), (0, CD_PAD - CD_IN))).astype(BF16)
            wuq = mla_w_uq[o].reshape(MLA_Q_RANK, MLA_HEADS, MLA_NOPE + MLA_ROPE)
            wuq = jnp.pad(wuq, ((0, 0), (0, 0), (0, 2 * LANE - MLA_NOPE - MLA_ROPE)))
            wuq = wuq.reshape(MLA_Q_RANK, 2 * LANE * MLA_HEADS).astype(BF16)
            pad_r = lambda v: jnp.pad(v, (0, LANE - MLA_ROPE))
            gv = jnp.zeros((SUBLANE, LANE), F32)
            gv = gv.at[0].set(jnp.tile(diff_qk_norm[o, 0], 2)).at[1].set(jnp.tile(diff_qk_norm[o, 1], 2))
            gv = gv.at[2].set(mla_nope_norm[o, 0]).at[3].set(pad_r(mla_rope_norm[o, 0]))
            gv = gv.at[4].set(mla_nope_norm[o, 1]).at[5].set(pad_r(mla_rope_norm[o, 1]))
            parts = _odd_in(hs, mod[l], g_mix, w_cd, rope, gv, mla_q_a_norm[o].reshape(1, MLA_Q_RANK),
                            mla_kv_a_norm[o].reshape(1, MLA_KV_RANK), wuq, mla_w_ukv[o].astype(BF16), rmats)
            mx = _attention(lam, *parts, diff_subln[o].reshape(1, LANE), 1.0 - lam_init, latent_only=last)
            hs = _post_odd(mx, hs, mod[l], g_ffn, wo, w1, w2, latent_only=last)
    return hs
```

```python
import functools
import math

import numpy as np
import jax
import jax.numpy as jnp
from jax import lax
from jax.experimental import pallas as pl
from jax.experimental.pallas import tpu as pltpu

F32 = jnp.float32
BF16 = jnp.bfloat16

D = 1024
B = 8
SEQ = 2048
CTX = 256
T = CTX + SEQ
DEPTH = 4
GRID_W = 64
FFN_H = ((8 * D + 3 * 256 - 1) // (3 * 256)) * 256
S5_W = D // 2
S5_GROUP = 16
S5_GROUPS = S5_W // S5_GROUP
S5_STATE = 64
HG_HEADS = 4
HG_DIM = D // 8
HG_W = HG_HEADS * HG_DIM
MAX_EXP_ARG = 60.0
DIFF_HEADS = 4
DIFF_DIM = D // 16
DIFF_W = DIFF_HEADS * 2 * DIFF_DIM
MLA_HEADS = 4
MLA_NOPE = D // 8
MLA_ROPE = D // 16
MLA_V = D // 8
MLA_Q_RANK = 3 * D // 8
MLA_KV_RANK = D // 4
ROPE_DIM = D // 16
ROPE_BASE = 10000.0
EPS = 1e-6
AB_IN = S5_W + 5 * HG_W
CD_IN = 3 * DIFF_W + MLA_Q_RANK + MLA_KV_RANK + MLA_ROPE
CD_PAD = CD_IN + 64

LANE = 128
SUBLANE = 8
TS = CTX
NTS = T // TS
TM = 512
NTL = SEQ // TM
MOD_ROWS = 16
CTX_ROW = B
S5_LC = 512
S5_NC = S5_GROUPS * S5_STATE // S5_LC
S5_UC = S5_LC // S5_STATE * S5_GROUP
S5_SB = 32
HG_C = TS
VMEM_LIMIT = 56 * 1024 * 1024
LOG2E = math.log2(math.e)


def _cparams(sem):
    return pltpu.CompilerParams(dimension_semantics=sem, vmem_limit_bytes=VMEM_LIMIT)


def _const_spec(shape):
    n = len(shape)
    return pl.BlockSpec(shape, lambda *_: (0,) * n, pipeline_mode=pl.Buffered(1))


def _layer_spec(shape, l):
    n = len(shape)
    return pl.BlockSpec((None,) + tuple(shape), lambda *_: (l,) + (0,) * n, pipeline_mode=pl.Buffered(1))


def _stream_specs(rows, n_lat, ctx_block):
    return [pl.BlockSpec((None, rows, D), lambda b, t: (b, jnp.minimum(t, n_lat - 1), 0)),
            pl.BlockSpec((None, CTX, D), lambda b, t: (b, ctx_block, 0))]


def _silu(x):
    return x * jax.nn.sigmoid(x)


def _norm_mod(x, g, shift, scale):
    ms = jnp.mean(x * x, axis=-1, keepdims=True)
    return x * lax.rsqrt(ms + EPS) * g * (1.0 + scale) + shift


def _dot(a, b):
    return jnp.dot(a, b, preferred_element_type=F32)


def _dot_nt(a, b):
    return lax.dot_general(a, b, (((1,), (1,)), ((), ())), preferred_element_type=F32)


def _mod_kernel(s_ref, w_ref, b_ref, o_ref):
    s = _silu(s_ref[...])
    o_ref[...] = _dot(s.astype(BF16), w_ref[...].astype(BF16)) + b_ref[...]


def _modulation(c, c_ctx, ada_w, ada_b):
    s = jnp.zeros((MOD_ROWS, D), F32).at[:B].set(c).at[CTX_ROW].set(c_ctx)
    nb = 1536
    out = pl.pallas_call(
        _mod_kernel,
        grid=(DEPTH, 6 * D // nb),
        in_specs=[
            pl.BlockSpec((MOD_ROWS, D), lambda l, n: (0, 0)),
            pl.BlockSpec((None, D, nb), lambda l, n: (l, 0, n)),
            pl.BlockSpec((None, 1, nb), lambda l, n: (l, 0, n)),
        ],
        out_specs=pl.BlockSpec((None, MOD_ROWS, nb), lambda l, n: (l, 0, n)),
        out_shape=jax.ShapeDtypeStruct((DEPTH, MOD_ROWS, 6 * D), F32),
        compiler_params=_cparams(("arbitrary", "arbitrary")),
        name="adaln_mod",
    )(s, ada_w, ada_b.reshape(DEPTH, 1, 6 * D))
    return out.reshape(DEPTH, MOD_ROWS, 6, D)


def _mod_spec(ctx_tile):
    return pl.BlockSpec((None, 6, D), lambda b, t: (jnp.where(t == ctx_tile, CTX_ROW, b), 0, 0))


def _for_tile_rows(has_ctx, body):
    if has_ctx:
        t = pl.program_id(1)
        pl.when(t < NTL)(lambda: body(TM))
        pl.when(t == NTL)(lambda: body(CTX))
    else:
        body(TM)


def _even_in_kernel(xl_ref, xc_ref, mod_ref, g_ref, w_ref, p_ref):
    def body(n):
        x = xl_ref[...] if n == TM else xc_ref[...]
        a = _norm_mod(x, g_ref[...], mod_ref[0:1, :], mod_ref[1:2, :])
        p_ref[0:n] = _dot(a.astype(BF16), w_ref[...])

    _for_tile_rows(True, body)


def _even_in(h_lat, h_ctx, ctx_block, mod_l, g, w_all, e):
    return pl.pallas_call(
        _even_in_kernel,
        grid=(B, NTL + 1),
        in_specs=_stream_specs(TM, NTL, ctx_block) + [
            _mod_spec(NTL),
            _const_spec((1, D)),
            _layer_spec((D, AB_IN), e),
        ],
        out_specs=pl.BlockSpec((None, TM, AB_IN), lambda b, t: (b, t, 0)),
        out_shape=jax.ShapeDtypeStruct((B, T, AB_IN), F32),
        compiler_params=_cparams(("arbitrary", "arbitrary")),
        name="even_in_proj",
    )(h_lat, h_ctx, mod_l, g, w_all)


def _s5_kernel(u_ref, w_ref, c_ref, ar_ref, ai_ref, y_ref, ux_ref, yx_ref, bu_ref, h_ref, *, rev):
    i = pl.program_id(1)

    @pl.when(i == 0)
    def _():
        h_ref[...] = jnp.zeros_like(h_ref)

    ar = ar_ref[...]
    ai = ai_ref[...]
    hr, hi = h_ref[0], h_ref[1]
    n_sb = TS // S5_SB
    order = list(reversed(range(n_sb))) if rev else list(range(n_sb))
    for b in range(B):
        ux_ref[pl.ds(b, TS, stride=B), :] = u_ref[b]

    def rows_of(sb):
        return slice(sb * S5_SB * B, (sb + 1) * S5_SB * B)

    def project(sb):
        bu_ref[rows_of(sb), :] = _dot(ux_ref[rows_of(sb), :].astype(BF16), w_ref[...])

    project(order[0])
    for n, sb in enumerate(order):
        if n + 1 < n_sb:
            project(order[n + 1])
        rows = rows_of(sb)
        for t in (reversed(range(S5_SB)) if rev else range(S5_SB)):
            r = slice(rows.start + t * B, rows.start + (t + 1) * B)
            nr = ar * hr - ai * hi + bu_ref[r, 0:S5_LC]
            ni = ar * hi + ai * hr + bu_ref[r, S5_LC:2 * S5_LC]
            bu_ref[r, 0:S5_LC] = nr
            bu_ref[r, S5_LC:2 * S5_LC] = ni
            hr, hi = nr, ni
        yx_ref[rows, :] = _dot(bu_ref[rows, :].astype(BF16), c_ref[...])
    h_ref[0] = hr
    h_ref[1] = hi
    for b in range(B):
        y_ref[b] = yx_ref[pl.ds(b, TS, stride=B), :]


def _seq_tile(d, i):
    return jnp.where(i == 0, NTS - 1, jnp.where(d == 0, i - 1, NTS - 1 - i))


def _s5_scan(proj, w_in, w_out, a_r, a_i, rev):
    d = 1 if rev else 0
    return pl.pallas_call(
        functools.partial(_s5_kernel, rev=rev),
        grid=(S5_NC, NTS),
        in_specs=[
            pl.BlockSpec((B, TS, S5_UC), lambda c, i: (0, _seq_tile(d, i), c)),
            pl.BlockSpec((None, None, S5_UC, 2 * S5_LC), lambda c, i: (d, c, 0, 0)),
            pl.BlockSpec((None, None, 2 * S5_LC, S5_UC), lambda c, i: (d, c, 0, 0)),
            pl.BlockSpec((None, None, B, S5_LC), lambda c, i: (d, c, 0, 0)),
            pl.BlockSpec((None, None, B, S5_LC), lambda c, i: (d, c, 0, 0)),
        ],
        out_specs=pl.BlockSpec((B, TS, S5_UC), lambda c, i: (0, _seq_tile(d, i), c)),
        out_shape=jax.ShapeDtypeStruct((B, T, S5_W), F32),
        scratch_shapes=[
            pltpu.VMEM((TS * B, S5_UC), F32),
            pltpu.VMEM((TS * B, S5_UC), F32),
            pltpu.VMEM((TS * B, 2 * S5_LC), F32),
            pltpu.VMEM((2, B, S5_LC), F32),
        ],
        compiler_params=_cparams(("arbitrary", "arbitrary")),
        name="s5_bwd" if rev else "s5_fwd",
    )(proj, w_in, w_out, a_r, a_i)


def _s5_params(lam_re, lam_im, log_step, b_re, b_im, c_re, c_im):
    lr = jnp.minimum(lam_re.astype(F32), -1e-4)
    li = lam_im.astype(F32)
    step = jnp.exp(log_step.astype(F32))[..., None]
    mag = jnp.exp(lr * step)
    a_r = mag * jnp.cos(li * step)
    a_i = mag * jnp.sin(li * step)
    den = lr * lr + li * li
    coef_r = ((a_r - 1) * lr + a_i * li) / den
    coef_i = (a_i * lr - (a_r - 1) * li) / den
    br = b_re.astype(F32)
    bi = b_im.astype(F32)
    bb_r = coef_r[..., None] * br - coef_i[..., None] * bi
    bb_i = coef_r[..., None] * bi + coef_i[..., None] * br
    gpc = S5_LC // S5_STATE
    eye = jnp.eye(gpc, dtype=F32)

    def in_blocks(bb):
        bb = bb.reshape(2, S5_NC, gpc, S5_STATE, S5_GROUP)
        return jnp.einsum('dngpc,gh->dngchp', bb, eye).reshape(2, S5_NC, S5_UC, S5_LC)

    def out_blocks(cc):
        cc = cc.reshape(2, S5_NC, gpc, S5_GROUP, S5_STATE)
        return jnp.einsum('dngcp,gh->dngphc', cc, eye).reshape(2, S5_NC, S5_LC, S5_UC)

    w_in = jnp.concatenate([in_blocks(bb_r), in_blocks(bb_i)], axis=-1).astype(BF16)
    w_out = jnp.concatenate([out_blocks(c_re.astype(F32)), -out_blocks(c_im.astype(F32))], axis=-2).astype(BF16)

    def lanes(a):
        return jnp.broadcast_to(a.reshape(2, S5_NC, 1, S5_LC), (2, S5_NC, B, S5_LC))

    return w_in, w_out, lanes(a_r), lanes(a_i)


HG_HALF = HG_C // 2
HG_LEVELS = tuple(2 ** e for e in range(int(math.log2(HG_HALF))))


def _hgrn_maps(rev):
    pos = np.arange(HG_C)
    ph = np.arange(HG_HALF)
    if rev:
        pos = HG_C - 1 - pos
        ph = HG_HALF - 1 - ph
    tri = (pos[None, :] <= pos[:, None]).astype(np.float32)
    pt, ps = ph[:, None], ph[None, :]
    lv = np.full((HG_HALF, HG_HALF), -1, np.int32)
    lv[pt == ps] = len(HG_LEVELS)
    for e, m in enumerate(HG_LEVELS):
        x, y = pt // m, ps // m
        lv[(x == y + 1) & (x % 2 == 1)] = e
    return jnp.asarray(tri, BF16), jnp.asarray(lv)


def _hgrn_kernel(q_ref, f_ref, v_ref, lb_ref, tri_ref, lv_ref, o_ref, s_ref, *, rev):
    i = pl.program_id(1)

    @pl.when(i == 0)
    def _():
        s_ref[...] = jnp.zeros_like(s_ref)

    c = HG_C
    fl = f_ref[...]
    lb = lb_ref[...]
    qb = _silu(q_ref[...]).astype(BF16)
    v = v_ref[...]
    x2 = fl * LOG2E
    a = jnp.exp2(-jnp.abs(x2))
    s1 = 1.0 + a
    r = 1.0 / s1
    kb = ((1.0 - lb) * jnp.where(x2 >= 0.0, a * r, r)).astype(BF16)
    t2 = jnp.exp2(jnp.minimum(-x2, MAX_EXP_ARG * LOG2E))
    lf = jnp.minimum(x2, 0.0) - jnp.log2(s1) + jnp.log2(1.0 + lb * t2)

    row = lax.broadcasted_iota(jnp.int32, (c, 1), 0)
    pos = (c - 1 - row) if rev else row
    first, second = (slice(HG_HALF, c), slice(0, HG_HALF)) if rev else (slice(0, HG_HALF), slice(HG_HALF, c))
    halves = (first, second)

    def from_earlier(x, j):
        return pltpu.roll(x, (c - j) if rev else j, 0)

    def from_later(x, j):
        return pltpu.roll(x, j if rev else (c - j), 0)

    def head(x, h, rows):
        return x[rows, h * HG_DIM:(h + 1) * HG_DIM]

    hi = lf.astype(BF16)
    r1 = lf - hi.astype(F32)
    mid = r1.astype(BF16)
    lo = (r1 - mid.astype(F32)).astype(BF16)
    tri = tri_ref[...]
    cum = _dot(tri, hi) + _dot(tri, mid) + _dot(tri, lo)

    lv = lv_ref[...]
    diag = len(HG_LEVELS)
    scores = [[jnp.where(lv == diag, _dot_nt(head(qb, h, r), head(kb, h, r)), 0.0) for r in halves]
              for h in range(HG_HEADS)]

    g_end = cum
    for e, m in enumerate(HG_LEVELS):
        qm = qb * jnp.exp2(cum - from_earlier(g_end, m)).astype(BF16)
        km = kb * jnp.exp2(g_end - cum).astype(BF16)
        for h in range(HG_HEADS):
            for j, r in enumerate(halves):
                scores[h][j] = jnp.where(lv == e, _dot_nt(head(qm, h, r), head(km, h, r)), scores[h][j])
        g_end = jnp.where((pos & m) != 0, g_end, from_later(g_end, m))
    qm = qb * jnp.exp2(cum - from_earlier(g_end, HG_HALF)).astype(BF16)
    km = kb * jnp.exp2(g_end - cum).astype(BF16)
    cross = [_dot_nt(head(qm, h, second), head(km, h, first)) for h in range(HG_HEADS)]
    g_end = jnp.where((pos & HG_HALF) != 0, g_end, from_later(g_end, HG_HALF))

    q_in = qb * jnp.exp2(cum).astype(BF16)
    k_out = kb * jnp.exp2(g_end - cum).astype(BF16)
    decay = jnp.exp2(g_end[0:1, :])
    vb = v.astype(BF16)
    for h in range(HG_HEADS):
        sl = slice(h * HG_DIM, (h + 1) * HG_DIM)
        st = s_ref[h]
        inter = _dot_nt(q_in[:, sl], st.astype(BF16))
        v_first, v_second = head(vb, h, first), head(vb, h, second)
        o_ref[first, sl] = inter[first] + _dot(scores[h][0].astype(BF16), v_first)
        o_ref[second, sl] = (inter[second] + _dot(cross[h].astype(BF16), v_first)
                             + _dot(scores[h][1].astype(BF16), v_second))
        s_ref[h] = decay[:, sl] * st + _dot(v[:, sl].T.astype(BF16), k_out[:, sl])


def _hgrn_scan(hg, lb, rev):
    d = 1 if rev else 0
    tri, lvl = _hgrn_maps(rev)
    return pl.pallas_call(
        functools.partial(_hgrn_kernel, rev=rev),
        grid=(B, NTS),
        in_specs=[
            pl.BlockSpec((None, HG_C, HG_W), lambda b, i: (b, _seq_tile(d, i), 1)),
            pl.BlockSpec((None, HG_C, HG_W), lambda b, i: (b, _seq_tile(d, i), 2 + d)),
            pl.BlockSpec((None, HG_C, HG_W), lambda b, i: (b, _seq_tile(d, i), 4)),
            _const_spec((1, HG_W)),
            _const_spec((HG_C, HG_C)),
            _const_spec((HG_HALF, HG_HALF)),
        ],
        out_specs=pl.BlockSpec((None, HG_C, HG_W), lambda b, i: (b, _seq_tile(d, i), 0)),
        out_shape=jax.ShapeDtypeStruct((B, T, HG_W), F32),
        scratch_shapes=[pltpu.VMEM((HG_HEADS, HG_DIM, HG_DIM), F32)],
        compiler_params=_cparams(("arbitrary", "arbitrary")),
        name="hgrn_bwd" if rev else "hgrn_fwd",
    )(hg, hg, hg, lb, tri, lvl)


def _ffn_tail(mix_out, h, mod_ref, g_ref, w_in_ref, w_out_ref):
    h1 = h + mod_ref[2:3, :] * mix_out
    a = _norm_mod(h1, g_ref[...], mod_ref[3:4, :], mod_ref[4:5, :])
    gu = _dot(a.astype(BF16), w_in_ref[...])
    act = _silu(gu[:, :FFN_H]) * gu[:, FFN_H:]
    return h1 + mod_ref[5:6, :] * _dot(act.astype(BF16), w_out_ref[...])


def _gelu_tanh(x):
    return 0.5 * x * (1.0 + jnp.tanh(math.sqrt(2.0 / math.pi) * (x + 0.044715 * (x * x * x))))


def _post_even_kernel(yf_ref, yb_ref, u_ref, of_ref, ob_ref, gate_ref, hl_ref, hc_ref, mod_ref, g_ref,
                      dsk_ref, gw_ref, gb_ref, on_ref, wo_ref, w_in_ref, w_out_ref, o_ref):
    y = yf_ref[...] + yb_ref[...] + u_ref[...] * dsk_ref[...]
    z = _gelu_tanh(y)
    s5 = z * jax.nn.sigmoid(_dot(z.astype(BF16), gw_ref[...]) + gb_ref[...])
    mix = _dot(s5.astype(BF16), wo_ref[0:S5_W, :])
    o = of_ref[...] + ob_ref[...]
    gate = _silu(gate_ref[...])
    for h in range(HG_HEADS):
        sl = slice(h * HG_DIM, (h + 1) * HG_DIM)
        oh = o[:, sl]
        ms = jnp.mean(oh * oh, axis=-1, keepdims=True)
        hn = oh * lax.rsqrt(ms + EPS) * on_ref[...] * gate[:, sl]
        mix = mix + _dot(hn.astype(BF16), wo_ref[S5_W + h * HG_DIM:S5_W + (h + 1) * HG_DIM, :])
    h = jnp.where(pl.program_id(1) == NTS - 1, hc_ref[...], hl_ref[...])
    o_ref[...] = _ffn_tail(mix, h, mod_ref, g_ref, w_in_ref, w_out_ref)


def _post_even(y_f, y_b, o_f, o_b, proj, h_lat, h_ctx, ctx_block, mod_l, g, dsk, gw, gb, on, wo, w_in, w_out, l):
    tok = lambda b, t: (b, t, 0)
    return pl.pallas_call(
        _post_even_kernel,
        grid=(B, NTS),
        in_specs=[
            pl.BlockSpec((None, TS, S5_W), tok),
            pl.BlockSpec((None, TS, S5_W), tok),
            pl.BlockSpec((None, TS, S5_W), tok),
            pl.BlockSpec((None, TS, HG_W), tok),
            pl.BlockSpec((None, TS, HG_W), tok),
            pl.BlockSpec((None, TS, HG_W), lambda b, t: (b, t, 5)),
        ] + _stream_specs(TS, NTS - 1, ctx_block) + [
            _mod_spec(NTS - 1),
            _const_spec((1, D)),
            _const_spec((1, S5_W)),
            _const_spec((S5_W, S5_W)),
            _const_spec((1, S5_W)),
            _const_spec((1, HG_DIM)),
            _layer_spec((D, D), l),
            _layer_spec((D, 2 * FFN_H), l),
            _layer_spec((FFN_H, D), l),
        ],
        out_specs=pl.BlockSpec((None, TS, D), tok),
        out_shape=jax.ShapeDtypeStruct((B, T, D), F32),
        compiler_params=_cparams(("arbitrary", "arbitrary")),
        name="post_even",
    )(y_f, y_b, proj, o_f, o_b, proj, h_lat, h_ctx, mod_l, g, dsk, gw, gb, on, wo, w_in, w_out)


def _post_odd_kernel(mx_ref, h_ref, mod_ref, g_ref, wo_ref, w_in_ref, w_out_ref, o_ref):
    o_ref[...] = _ffn_tail(_dot(mx_ref[...], wo_ref[...]), h_ref[...], mod_ref, g_ref, w_in_ref, w_out_ref)


def _post_odd(mx, hs, mod_l, g, wo, w_in, w_out, l, latent_only):
    nt = NTS - 1 if latent_only else NTS
    tok = lambda b, t: (b, t, 0)
    return pl.pallas_call(
        _post_odd_kernel,
        grid=(B, nt),
        in_specs=[
            pl.BlockSpec((None, TS, D), tok),
            pl.BlockSpec((None, TS, D), tok),
            _mod_spec(NTS - 1),
            _const_spec((1, D)),
            _layer_spec((D, D), l),
            _layer_spec((D, 2 * FFN_H), l),
            _layer_spec((FFN_H, D), l),
        ],
        out_specs=pl.BlockSpec((None, TS, D), tok),
        out_shape=jax.ShapeDtypeStruct((B, nt * TS, D), F32),
        compiler_params=_cparams(("arbitrary", "arbitrary")),
        name="post_odd",
    )(mx, hs, mod_l, g, wo, w_in, w_out)


def _group_rms(xs, r):
    ms = _dot((xs * xs).astype(BF16), r)
    return xs * lax.rsqrt(ms + EPS)


def _rope(xs, cos, sin_signed, first_half):
    rot = jnp.where(first_half, pltpu.roll(xs, LANE - ROPE_DIM // 4, 1), pltpu.roll(xs, ROPE_DIM // 4, 1))
    return xs * cos + rot * sin_signed


def _odd_in_kernel(x_ref, mod_ref, g_ref, w_ref, rope_ref, gv_ref, qan_ref, kvan_ref, wuq_ref, wukv_ref,
                   rpair_ref, rq_ref, rfull_ref, rhalf_ref,
                   dq_ref, k1_ref, k2_ref, dv_ref, mq_ref, mk_ref, mv_ref, *, mla_scale):
    lane = lax.broadcasted_iota(jnp.int32, (1, LANE), 1)
    first_half = (lane % (ROPE_DIM // 2)) < (ROPE_DIM // 4)
    low = lane < DIFF_DIM
    rpair, rq, rfull, rhalf = rpair_ref[...], rq_ref[...], rfull_ref[...], rhalf_ref[...]
    g_dq, g_dk = gv_ref[0:1, :], gv_ref[1:2, :]
    g_qn, g_qr, g_kn, g_kr = gv_ref[2:3, :], gv_ref[3:4, :], gv_ref[4:5, :], gv_ref[5:6, :]
    diff_scale = DIFF_DIM ** -0.5 * LOG2E
    o_cq = 3 * DIFF_W
    o_ckv = o_cq + MLA_Q_RANK
    o_kr = o_ckv + MLA_KV_RANK

    def body(n):
        a = _norm_mod(x_ref[0:n], g_ref[...], mod_ref[0:1, :], mod_ref[1:2, :])
        p = _dot(a.astype(BF16), w_ref[...])
        cos_d, sin_d, cos_m, sin_m = rope_ref[0, 0:n], rope_ref[1, 0:n], rope_ref[2, 0:n], rope_ref[3, 0:n]
        ones = jnp.ones((n, LANE), BF16)

        for h2 in range(0, DIFF_HEADS, 2):
            pair = slice(h2 * LANE, (h2 + 2) * LANE)
            qn = _group_rms(p[:, pair], rpair)
            kn = _group_rms(p[:, DIFF_W + pair.start:DIFF_W + pair.stop], rpair)
            for j in range(2):
                sl = slice((h2 + j) * LANE, (h2 + j + 1) * LANE)
                half = slice(j * LANE, (j + 1) * LANE)
                qh = _rope(qn[:, half] * g_dq, cos_d, sin_d, first_half)
                dq_ref[0:n, sl] = (qh * diff_scale).astype(BF16)
                kh = _rope(kn[:, half] * g_dk, cos_d, sin_d, first_half)
                k1_ref[0:n, sl] = jnp.where(low, kh, 0.0).astype(BF16)
                k2_ref[0:n, sl] = jnp.where(low, 0.0, kh).astype(BF16)
        for h in range(DIFF_HEADS):
            dv_ref[0:n, 2 * h * LANE:(2 * h + 1) * LANE] = (
                p[:, 2 * DIFF_W + h * LANE:2 * DIFF_W + (h + 1) * LANE].astype(BF16))
            dv_ref[0:n, (2 * h + 1) * LANE:(2 * h + 2) * LANE] = ones

        cq = p[:, o_cq:o_ckv]
        cqn = cq * lax.rsqrt(jnp.mean(cq * cq, axis=-1, keepdims=True) + EPS) * qan_ref[...]
        q = _dot(cqn.astype(BF16), wuq_ref[...])
        ckv = p[:, o_ckv:o_kr]
        ckvn = ckv * lax.rsqrt(jnp.mean(ckv * ckv, axis=-1, keepdims=True) + EPS) * kvan_ref[...]
        kv = _dot(ckvn.astype(BF16), wukv_ref[...])
        k_rope = _rope(_group_rms(p[:, o_kr:o_kr + LANE], rhalf) * g_kr, cos_m, sin_m, first_half).astype(BF16)
        for h in range(MLA_HEADS):
            o = 2 * LANE * h
            qn = _group_rms(q[:, o:o + 2 * LANE], rq)
            q_nope = qn[:, :LANE] * g_qn
            q_rope = _rope(qn[:, LANE:] * g_qr, cos_m, sin_m, first_half)
            mq_ref[0:n, o:o + LANE] = (q_nope * mla_scale).astype(BF16)
            mq_ref[0:n, o + LANE:o + 2 * LANE] = (q_rope * mla_scale).astype(BF16)
            mk_ref[0:n, o:o + LANE] = (_group_rms(kv[:, o:o + LANE], rfull) * g_kn).astype(BF16)
            mk_ref[0:n, o + LANE:o + 2 * LANE] = k_rope
            mv_ref[0:n, o:o + LANE] = kv[:, o + LANE:o + 2 * LANE].astype(BF16)
            mv_ref[0:n, o + LANE:o + 2 * LANE] = ones

    _for_tile_rows(True, body)


def _odd_in(hs, mod_l, g, w_all, o, rope, gv, qan, kvan, wuq, wukv, rmats):
    tok = lambda b, t: (b, t, 0)
    sds = lambda n: jax.ShapeDtypeStruct((B, T, n), BF16)
    wide = 2 * LANE * MLA_HEADS
    widths = (DIFF_W, DIFF_W, DIFF_W, wide, wide, wide, wide)
    return pl.pallas_call(
        functools.partial(_odd_in_kernel, mla_scale=(MLA_NOPE + MLA_ROPE) ** -0.5 * LOG2E),
        grid=(B, NTL + 1),
        in_specs=[
            pl.BlockSpec((None, TM, D), tok),
            _mod_spec(NTL),
            _const_spec((1, D)),
            _layer_spec((D, CD_PAD), o),
            pl.BlockSpec((4, TM, LANE), lambda b, t: (0, t, 0)),
            _const_spec((SUBLANE, LANE)),
            _const_spec((1, MLA_Q_RANK)),
            _const_spec((1, MLA_KV_RANK)),
            _const_spec((MLA_Q_RANK, 2 * LANE * MLA_HEADS)),
            _const_spec((MLA_KV_RANK, 2 * LANE * MLA_HEADS)),
            _const_spec((2 * LANE, 2 * LANE)),
            _const_spec((2 * LANE, 2 * LANE)),
            _const_spec((LANE, LANE)),
            _const_spec((LANE, LANE)),
        ],
        out_specs=[pl.BlockSpec((None, TM, n), tok) for n in widths],
        out_shape=[sds(n) for n in widths],
        compiler_params=_cparams(("arbitrary", "arbitrary")),
        name="odd_in_proj",
    )(hs, mod_l, g, w_all, rope, gv, qan, kvan, wuq, wukv, *rmats)


def _exp2_shifted(s):
    return jnp.exp2(s - jnp.max(s, axis=-1, keepdims=True)).astype(BF16)


def _pv_normalised(e, v_aug):
    o = _dot(e, v_aug)
    return o[:, :LANE] / o[:, LANE:]


def _attn_kernel(lam_ref, dq_ref, k1_ref, k2_ref, dv_ref, mq_ref, mk_ref, mv_ref, sub_ref, o_ref,
                 *, out_scale, has_ctx):
    lam = lam_ref[0, 0]

    def body(keys):
        for h in range(DIFF_HEADS):
            sl = slice(h * LANE, (h + 1) * LANE)
            sv = slice(2 * LANE * h, 2 * LANE * (h + 1))
            qh = dq_ref[:, sl]
            o1 = _pv_normalised(_exp2_shifted(_dot_nt(qh, k1_ref[keys, sl])), dv_ref[keys, sv])
            o2 = _pv_normalised(_exp2_shifted(_dot_nt(qh, k2_ref[keys, sl])), dv_ref[keys, sv])
            o = o1 - lam * o2
            o = o * lax.rsqrt(jnp.mean(o * o, axis=-1, keepdims=True) + EPS) * sub_ref[...] * out_scale
            o_ref[:, sl] = o.astype(BF16)
        for h in range(MLA_HEADS):
            sq = slice(2 * LANE * h, 2 * LANE * (h + 1))
            o = _pv_normalised(_exp2_shifted(_dot_nt(mq_ref[:, sq], mk_ref[keys, sq])), mv_ref[keys, sq])
            o_ref[:, DIFF_W + h * LANE:DIFF_W + (h + 1) * LANE] = o.astype(BF16)

    if has_ctx:
        t = pl.program_id(1)
        pl.when(t == 0)(lambda: body(slice(SEQ, T)))
        pl.when(t > 0)(lambda: body(slice(0, T)))
    else:
        body(slice(0, T))


def _attention(lam, dq, k1, k2, dv, mq, mk, mv, sub, out_scale, latent_only):
    has_ctx = not latent_only
    nt = NTS if has_ctx else NTS - 1
    wide = 2 * LANE * MLA_HEADS
    tile = (lambda t: (t + NTS - 1) % NTS) if has_ctx else (lambda t: t)
    qspec = lambda n: pl.BlockSpec((None, TS, n), lambda b, t: (b, tile(t), 0))
    kspec = lambda n: pl.BlockSpec((None, T, n), lambda b, t: (b, 0, 0))
    return pl.pallas_call(
        functools.partial(_attn_kernel, out_scale=out_scale, has_ctx=has_ctx),
        grid=(B, nt),
        in_specs=[
            pl.BlockSpec(memory_space=pltpu.SMEM),
            qspec(DIFF_W), kspec(DIFF_W), kspec(DIFF_W), kspec(wide),
            qspec(wide), kspec(wide), kspec(wide),
            _const_spec((1, LANE)),
        ],
        out_specs=qspec(D),
        out_shape=jax.ShapeDtypeStruct((B, nt * TS, D), BF16),
        compiler_params=_cparams(("arbitrary", "arbitrary")),
        name="attention",
    )(lam, dq, k1, k2, dv, mq, mk, mv, sub)


def _rope_tables():
    n_tok = SEQ
    rows = jnp.repeat(jnp.arange(n_tok // GRID_W, dtype=jnp.int32), GRID_W)
    cols = jnp.tile(jnp.arange(GRID_W, dtype=jnp.int32), n_tok // GRID_W)
    n_freq = ROPE_DIM // 4
    inv = jnp.power(ROPE_BASE, -jnp.arange(n_freq, dtype=F32) / n_freq)
    ang_r = rows.astype(F32)[:, None] * inv
    ang_c = cols.astype(F32)[:, None] * inv
    ang = jnp.concatenate([ang_r, ang_r, ang_c, ang_c], axis=-1)
    sign = jnp.where((jnp.arange(ROPE_DIM) % (ROPE_DIM // 2)) < n_freq, -1.0, 1.0).astype(F32)
    cos = jnp.concatenate([jnp.cos(ang), jnp.ones((CTX, ROPE_DIM), F32)], axis=0)
    sin = jnp.concatenate([jnp.sin(ang) * sign, jnp.zeros((CTX, ROPE_DIM), F32)], axis=0)
    one, zero = jnp.ones_like(cos), jnp.zeros_like(sin)
    cat = lambda a, b: jnp.concatenate([a, b], axis=-1)
    return jnp.stack([cat(cos, cos), cat(sin, sin), cat(cos, one), cat(sin, zero)])


def _averaging_mats():
    idx = np.arange(2 * LANE)
    same64 = (idx[:, None] // DIFF_DIM) == (idx[None, :] // DIFF_DIM)
    rpair = np.where(same64, 1.0 / DIFF_DIM, 0.0)
    rfull = np.full((LANE, LANE), 1.0 / LANE)
    il = np.arange(LANE)
    rhalf = np.where((il[:, None] < MLA_ROPE) & (il[None, :] < MLA_ROPE), 1.0 / MLA_ROPE, 0.0)
    rq = np.zeros((2 * LANE, 2 * LANE))
    rq[:LANE, :LANE] = rfull
    rq[LANE:, LANE:] = rhalf
    return tuple(jnp.asarray(m, BF16) for m in (rpair, rq, rfull, rhalf))


def kernel(x, c, ctx, c_ctx, ada_w, ada_b, norm_mix, norm_ffn, w_out, ffn_w_in, ffn_w_out, ab_w_in, s5_lambda_re, s5_lambda_im, s5_log_step, s5_b_re, s5_b_im, s5_c_re, s5_c_im, s5_d, s5_glu_w, s5_glu_b, hgrn_lb_logits, hgrn_out_norm, cd_w_in, diff_lambda, diff_qk_norm, diff_subln, mla_q_a_norm, mla_kv_a_norm, mla_w_uq, mla_w_ukv, mla_nope_norm, mla_rope_norm):
    assert x.shape == (B, SEQ, D) and ctx.shape == (B, CTX, D)
    mod = _modulation(c, c_ctx, ada_w, ada_b)
    wo_all = w_out.astype(BF16)
    w1_all = ffn_w_in.astype(BF16)
    w2_all = ffn_w_out.astype(BF16)
    wab_all = ab_w_in.astype(BF16)
    wcd_all = jnp.pad(cd_w_in, ((0, 0), (0, 0), (0, CD_PAD - CD_IN))).astype(BF16)
    h_lat, h_ctx, ctx_block = x, ctx, 0
    rope = _rope_tables()
    rmats = _averaging_mats()

    lb_p = jax.nn.softmax(hgrn_lb_logits.astype(F32), axis=0)
    lower_bounds = jnp.cumsum(lb_p, axis=0) - lb_p[0:1]

    for l in range(DEPTH):
        last = l == DEPTH - 1
        g_mix = norm_mix[l].reshape(1, D)
        g_ffn = norm_ffn[l].reshape(1, D)
        if l % 2 == 0:
            e = l // 2
            proj = _even_in(h_lat, h_ctx, ctx_block, mod[l], g_mix, wab_all, e)
            s5w_in, s5w_out, a_r, a_i = _s5_params(s5_lambda_re[e], s5_lambda_im[e], s5_log_step[e],
                                                   s5_b_re[e], s5_b_im[e], s5_c_re[e], s5_c_im[e])
            y_f = _s5_scan(proj, s5w_in, s5w_out, a_r, a_i, rev=False)
            y_b = _s5_scan(proj, s5w_in, s5w_out, a_r, a_i, rev=True)
            lb = lower_bounds[e].reshape(1, HG_W)
            o_f = _hgrn_scan(proj, lb, rev=False)
            o_b = _hgrn_scan(proj, lb, rev=True)
            hs = _post_even(y_f, y_b, o_f, o_b, proj, h_lat, h_ctx, ctx_block, mod[l], g_ffn,
                            s5_d[e].reshape(1, S5_W), s5_glu_w[e].astype(BF16), s5_glu_b[e].reshape(1, S5_W),
                            hgrn_out_norm[e].reshape(1, HG_DIM), wo_all, w1_all, w2_all, l)
        else:
            o = l // 2
            lam_init = 0.8 - 0.6 * math.exp(-0.3 * l)
            lv = diff_lambda[o].astype(F32)
            lam = (jnp.exp(jnp.sum(lv[0] * lv[1])) - jnp.exp(jnp.sum(lv[2] * lv[3])) + lam_init).reshape(1, 1)
            wuq = mla_w_uq[o].reshape(MLA_Q_RANK, MLA_HEADS, MLA_NOPE + MLA_ROPE)
            wuq = jnp.pad(wuq, ((0, 0), (0, 0), (0, 2 * LANE - MLA_NOPE - MLA_ROPE)))
            wuq = wuq.reshape(MLA_Q_RANK, 2 * LANE * MLA_HEADS).astype(BF16)
            pad_r = lambda v: jnp.pad(v, (0, LANE - MLA_ROPE))
            gv = jnp.zeros((SUBLANE, LANE), F32)
            gv = gv.at[0].set(jnp.tile(diff_qk_norm[o, 0], 2)).at[1].set(jnp.tile(diff_qk_norm[o, 1], 2))
            gv = gv.at[2].set(mla_nope_norm[o, 0]).at[3].set(pad_r(mla_rope_norm[o, 0]))
            gv = gv.at[4].set(mla_nope_norm[o, 1]).at[5].set(pad_r(mla_rope_norm[o, 1]))
            parts = _odd_in(hs, mod[l], g_mix, wcd_all, o, rope, gv, mla_q_a_norm[o].reshape(1, MLA_Q_RANK),
                            mla_kv_a_norm[o].reshape(1, MLA_KV_RANK), wuq, mla_w_ukv[o].astype(BF16), rmats)
            mx = _attention(lam, *parts, diff_subln[o].reshape(1, LANE), 1.0 - lam_init, latent_only=last)
            hs = _post_odd(mx, hs, mod[l], g_ffn, wo_all, w1_all, w2_all, l, latent_only=last)
        h_lat, h_ctx, ctx_block = hs, hs, SEQ // CTX
    return hs
```

```python
import functools
import math

import numpy as np
import jax
import jax.numpy as jnp
from jax import lax
from jax.experimental import pallas as pl
from jax.experimental.pallas import tpu as pltpu

F32 = jnp.float32
BF16 = jnp.bfloat16

D = 1024
B = 8
SEQ = 2048
CTX = 256
T = CTX + SEQ
DEPTH = 4
GRID_W = 64
FFN_H = ((8 * D + 3 * 256 - 1) // (3 * 256)) * 256
S5_W = D // 2
S5_GROUP = 16
S5_GROUPS = S5_W // S5_GROUP
S5_STATE = 64
HG_HEADS = 4
HG_DIM = D // 8
HG_W = HG_HEADS * HG_DIM
MAX_EXP_ARG = 60.0
DIFF_HEADS = 4
DIFF_DIM = D // 16
DIFF_W = DIFF_HEADS * 2 * DIFF_DIM
MLA_HEADS = 4
MLA_NOPE = D // 8
MLA_ROPE = D // 16
MLA_V = D // 8
MLA_Q_RANK = 3 * D // 8
MLA_KV_RANK = D // 4
ROPE_DIM = D // 16
ROPE_BASE = 10000.0
EPS = 1e-6
AB_IN = S5_W + 5 * HG_W
CD_IN = 3 * DIFF_W + MLA_Q_RANK + MLA_KV_RANK + MLA_ROPE
CD_PAD = CD_IN + 64

LANE = 128
SUBLANE = 8
TS = CTX
NTS = T // TS
TM = 512
NTL = SEQ // TM
MOD_ROWS = 16
CTX_ROW = B
S5_LC = 512
S5_NC = S5_GROUPS * S5_STATE // S5_LC
S5_UC = S5_LC // S5_STATE * S5_GROUP
S5_SB = 32
HG_C = TS
VMEM_LIMIT = 56 * 1024 * 1024
LOG2E = math.log2(math.e)


def _cparams(sem):
    return pltpu.CompilerParams(dimension_semantics=sem, vmem_limit_bytes=VMEM_LIMIT)


def _const_spec(shape):
    n = len(shape)
    return pl.BlockSpec(shape, lambda *_: (0,) * n, pipeline_mode=pl.Buffered(1))


def _layer_spec(shape, l):
    n = len(shape)
    return pl.BlockSpec((None,) + tuple(shape), lambda *_: (l,) + (0,) * n, pipeline_mode=pl.Buffered(1))


def _stream_specs(rows, n_lat, ctx_block):
    return [pl.BlockSpec((None, rows, D), lambda b, t: (b, jnp.minimum(t, n_lat - 1), 0)),
            pl.BlockSpec((None, CTX, D), lambda b, t: (b, ctx_block, 0))]


def _silu(x):
    return x * jax.nn.sigmoid(x)


def _norm_mod(x, g, shift, scale):
    ms = jnp.mean(x * x, axis=-1, keepdims=True)
    return x * lax.rsqrt(ms + EPS) * g * (1.0 + scale) + shift


def _dot(a, b):
    return jnp.dot(a, b, preferred_element_type=F32)


def _dot_nt(a, b):
    return lax.dot_general(a, b, (((1,), (1,)), ((), ())), preferred_element_type=F32)


def _mod_kernel(s_ref, w_ref, b_ref, o_ref):
    s = _silu(s_ref[...])
    o_ref[...] = _dot(s.astype(BF16), w_ref[...].astype(BF16)) + b_ref[...]


def _modulation(c, c_ctx, ada_w, ada_b):
    s = jnp.zeros((MOD_ROWS, D), F32).at[:B].set(c).at[CTX_ROW].set(c_ctx)
    nb = 1536
    out = pl.pallas_call(
        _mod_kernel,
        grid=(DEPTH, 6 * D // nb),
        in_specs=[
            pl.BlockSpec((MOD_ROWS, D), lambda l, n: (0, 0)),
            pl.BlockSpec((None, D, nb), lambda l, n: (l, 0, n)),
            pl.BlockSpec((None, 1, nb), lambda l, n: (l, 0, n)),
        ],
        out_specs=pl.BlockSpec((None, MOD_ROWS, nb), lambda l, n: (l, 0, n)),
        out_shape=jax.ShapeDtypeStruct((DEPTH, MOD_ROWS, 6 * D), F32),
        compiler_params=_cparams(("arbitrary", "arbitrary")),
        name="adaln_mod",
    )(s, ada_w, ada_b.reshape(DEPTH, 1, 6 * D))
    return out.reshape(DEPTH, MOD_ROWS, 6, D)


def _mod_spec(ctx_tile):
    return pl.BlockSpec((None, 6, D), lambda b, t: (jnp.where(t == ctx_tile, CTX_ROW, b), 0, 0))


def _for_tile_rows(has_ctx, body):
    if has_ctx:
        t = pl.program_id(1)
        pl.when(t < NTL)(lambda: body(TM))
        pl.when(t == NTL)(lambda: body(CTX))
    else:
        body(TM)


def _even_in_kernel(xl_ref, xc_ref, mod_ref, g_ref, w_ref, p_ref):
    def body(n):
        x = xl_ref[...] if n == TM else xc_ref[...]
        a = _norm_mod(x, g_ref[...], mod_ref[0:1, :], mod_ref[1:2, :])
        p_ref[0:n] = _dot(a.astype(BF16), w_ref[...])

    _for_tile_rows(True, body)


def _even_in(h_lat, h_ctx, ctx_block, mod_l, g, w_all, e):
    return pl.pallas_call(
        _even_in_kernel,
        grid=(B, NTL + 1),
        in_specs=_stream_specs(TM, NTL, ctx_block) + [
            _mod_spec(NTL),
            _const_spec((1, D)),
            _layer_spec((D, AB_IN), e),
        ],
        out_specs=pl.BlockSpec((None, TM, AB_IN), lambda b, t: (b, t, 0)),
        out_shape=jax.ShapeDtypeStruct((B, T, AB_IN), F32),
        compiler_params=_cparams(("arbitrary", "arbitrary")),
        name="even_in_proj",
    )(h_lat, h_ctx, mod_l, g, w_all)


def _s5_tile(u_ref, w_ref, c_ref, ar_ref, ai_ref, y_ref, ux_ref, yx_ref, bu_ref, h_ref, rev):
    ar = ar_ref[...]
    ai = ai_ref[...]
    hr, hi = h_ref[0], h_ref[1]
    n_sb = TS // S5_SB
    order = list(reversed(range(n_sb))) if rev else list(range(n_sb))
    for b in range(B):
        ux_ref[pl.ds(b, TS, stride=B), :] = u_ref[b]

    def rows_of(sb):
        return slice(sb * S5_SB * B, (sb + 1) * S5_SB * B)

    def project(sb):
        bu_ref[rows_of(sb), :] = _dot(ux_ref[rows_of(sb), :].astype(BF16), w_ref[...])

    project(order[0])
    for n, sb in enumerate(order):
        if n + 1 < n_sb:
            project(order[n + 1])
        rows = rows_of(sb)
        for t in (reversed(range(S5_SB)) if rev else range(S5_SB)):
            r = slice(rows.start + t * B, rows.start + (t + 1) * B)
            nr = ar * hr - ai * hi + bu_ref[r, 0:S5_LC]
            ni = ar * hi + ai * hr + bu_ref[r, S5_LC:2 * S5_LC]
            bu_ref[r, 0:S5_LC] = nr
            bu_ref[r, S5_LC:2 * S5_LC] = ni
            hr, hi = nr, ni
        yx_ref[rows, :] = _dot(bu_ref[rows, :].astype(BF16), c_ref[...])
    h_ref[0] = hr
    h_ref[1] = hi
    for b in range(B):
        y_ref[b] = yx_ref[pl.ds(b, TS, stride=B), :]


def _s5_kernel(uf_ref, wf_ref, cf_ref, arf_ref, aif_ref, ur_ref, wr_ref, cr_ref, arr_ref, air_ref,
               yf_ref, yr_ref, uxf_ref, yxf_ref, buf_ref, hf_ref, uxr_ref, yxr_ref, bur_ref, hr_ref):
    @pl.when(pl.program_id(1) == 0)
    def _():
        hf_ref[...] = jnp.zeros_like(hf_ref)
        hr_ref[...] = jnp.zeros_like(hr_ref)

    _s5_tile(uf_ref, wf_ref, cf_ref, arf_ref, aif_ref, yf_ref, uxf_ref, yxf_ref, buf_ref, hf_ref, rev=False)
    _s5_tile(ur_ref, wr_ref, cr_ref, arr_ref, air_ref, yr_ref, uxr_ref, yxr_ref, bur_ref, hr_ref, rev=True)


def _seq_tile(d, i):
    return jnp.where(i == 0, NTS - 1, jnp.where(d == 0, i - 1, NTS - 1 - i))


def _s5_scan(proj, w_in, w_out, a_r, a_i):
    def operands(d):
        par = lambda *shape: pl.BlockSpec((None, None) + shape, lambda c, i: (d, c, 0, 0))
        return [pl.BlockSpec((B, TS, S5_UC), lambda c, i: (0, _seq_tile(d, i), c)),
                par(S5_UC, 2 * S5_LC), par(2 * S5_LC, S5_UC), par(B, S5_LC), par(B, S5_LC)]

    scratch = [pltpu.VMEM((TS * B, S5_UC), F32), pltpu.VMEM((TS * B, S5_UC), F32),
               pltpu.VMEM((TS * B, 2 * S5_LC), F32), pltpu.VMEM((2, B, S5_LC), F32)]
    out = jax.ShapeDtypeStruct((B, T, S5_W), F32)
    return pl.pallas_call(
        _s5_kernel,
        grid=(S5_NC, NTS),
        in_specs=operands(0) + operands(1),
        out_specs=[pl.BlockSpec((B, TS, S5_UC), lambda c, i, d=d: (0, _seq_tile(d, i), c)) for d in (0, 1)],
        out_shape=[out, out],
        scratch_shapes=scratch + scratch,
        compiler_params=_cparams(("arbitrary", "arbitrary")),
        name="s5_scan",
    )(proj, w_in, w_out, a_r, a_i, proj, w_in, w_out, a_r, a_i)


def _s5_params(lam_re, lam_im, log_step, b_re, b_im, c_re, c_im):
    lr = jnp.minimum(lam_re.astype(F32), -1e-4)
    li = lam_im.astype(F32)
    step = jnp.exp(log_step.astype(F32))[..., None]
    mag = jnp.exp(lr * step)
    a_r = mag * jnp.cos(li * step)
    a_i = mag * jnp.sin(li * step)
    den = lr * lr + li * li
    coef_r = ((a_r - 1) * lr + a_i * li) / den
    coef_i = (a_i * lr - (a_r - 1) * li) / den
    br = b_re.astype(F32)
    bi = b_im.astype(F32)
    bb_r = coef_r[..., None] * br - coef_i[..., None] * bi
    bb_i = coef_r[..., None] * bi + coef_i[..., None] * br
    gpc = S5_LC // S5_STATE
    eye = jnp.eye(gpc, dtype=F32)

    def in_blocks(bb):
        bb = bb.reshape(2, S5_NC, gpc, S5_STATE, S5_GROUP)
        return jnp.einsum('dngpc,gh->dngchp', bb, eye).reshape(2, S5_NC, S5_UC, S5_LC)

    def out_blocks(cc):
        cc = cc.reshape(2, S5_NC, gpc, S5_GROUP, S5_STATE)
        return jnp.einsum('dngcp,gh->dngphc', cc, eye).reshape(2, S5_NC, S5_LC, S5_UC)

    w_in = jnp.concatenate([in_blocks(bb_r), in_blocks(bb_i)], axis=-1).astype(BF16)
    w_out = jnp.concatenate([out_blocks(c_re.astype(F32)), -out_blocks(c_im.astype(F32))], axis=-2).astype(BF16)

    def lanes(a):
        return jnp.broadcast_to(a.reshape(2, S5_NC, 1, S5_LC), (2, S5_NC, B, S5_LC))

    return w_in, w_out, lanes(a_r), lanes(a_i)


HG_HALF = HG_C // 2
HG_LEVELS = tuple(2 ** e for e in range(int(math.log2(HG_HALF))))


def _hgrn_maps(rev):
    pos = np.arange(HG_C)
    ph = np.arange(HG_HALF)
    if rev:
        pos = HG_C - 1 - pos
        ph = HG_HALF - 1 - ph
    tri = (pos[None, :] <= pos[:, None]).astype(np.float32)
    pt, ps = ph[:, None], ph[None, :]
    lv = np.full((HG_HALF, HG_HALF), -1, np.int32)
    lv[pt == ps] = len(HG_LEVELS)
    for e, m in enumerate(HG_LEVELS):
        x, y = pt // m, ps // m
        lv[(x == y + 1) & (x % 2 == 1)] = e
    return jnp.asarray(tri, BF16), jnp.asarray(lv)


def _hgrn_chunk(q_ref, f_ref, v_ref, lb, tri_ref, lv_ref, o_ref, s_ref, rev):
    c = HG_C
    fl = f_ref[...]
    qb = _silu(q_ref[...]).astype(BF16)
    v = v_ref[...]
    x2 = fl * LOG2E
    a = jnp.exp2(-jnp.abs(x2))
    s1 = 1.0 + a
    r = 1.0 / s1
    kb = ((1.0 - lb) * jnp.where(x2 >= 0.0, a * r, r)).astype(BF16)
    t2 = jnp.exp2(jnp.minimum(-x2, MAX_EXP_ARG * LOG2E))
    lf = jnp.minimum(x2, 0.0) - jnp.log2(s1) + jnp.log2(1.0 + lb * t2)

    row = lax.broadcasted_iota(jnp.int32, (c, 1), 0)
    pos = (c - 1 - row) if rev else row
    first, second = (slice(HG_HALF, c), slice(0, HG_HALF)) if rev else (slice(0, HG_HALF), slice(HG_HALF, c))
    halves = (first, second)

    def from_earlier(x, j):
        return pltpu.roll(x, (c - j) if rev else j, 0)

    def from_later(x, j):
        return pltpu.roll(x, j if rev else (c - j), 0)

    def head(x, h, rows):
        return x[rows, h * HG_DIM:(h + 1) * HG_DIM]

    hi = lf.astype(BF16)
    r1 = lf - hi.astype(F32)
    mid = r1.astype(BF16)
    lo = (r1 - mid.astype(F32)).astype(BF16)
    tri = tri_ref[...]
    cum = _dot(tri, hi) + _dot(tri, mid) + _dot(tri, lo)

    lv = lv_ref[...]
    diag = len(HG_LEVELS)
    scores = [[jnp.where(lv == diag, _dot_nt(head(qb, h, r), head(kb, h, r)), 0.0) for r in halves]
              for h in range(HG_HEADS)]

    g_end = cum
    for e, m in enumerate(HG_LEVELS):
        qm = qb * jnp.exp2(cum - from_earlier(g_end, m)).astype(BF16)
        km = kb * jnp.exp2(g_end - cum).astype(BF16)
        for h in range(HG_HEADS):
            for j, r in enumerate(halves):
                scores[h][j] = jnp.where(lv == e, _dot_nt(head(qm, h, r), head(km, h, r)), scores[h][j])
        g_end = jnp.where((pos & m) != 0, g_end, from_later(g_end, m))
    qm = qb * jnp.exp2(cum - from_earlier(g_end, HG_HALF)).astype(BF16)
    km = kb * jnp.exp2(g_end - cum).astype(BF16)
    cross = [_dot_nt(head(qm, h, second), head(km, h, first)) for h in range(HG_HEADS)]
    g_end = jnp.where((pos & HG_HALF) != 0, g_end, from_later(g_end, HG_HALF))

    q_in = qb * jnp.exp2(cum).astype(BF16)
    k_out = kb * jnp.exp2(g_end - cum).astype(BF16)
    decay = jnp.exp2(g_end[0:1, :])
    vb = v.astype(BF16)
    for h in range(HG_HEADS):
        sl = slice(h * HG_DIM, (h + 1) * HG_DIM)
        st = s_ref[h]
        inter = _dot_nt(q_in[:, sl], st.astype(BF16))
        v_first, v_second = head(vb, h, first), head(vb, h, second)
        o_ref[first, sl] = inter[first] + _dot(scores[h][0].astype(BF16), v_first)
        o_ref[second, sl] = (inter[second] + _dot(cross[h].astype(BF16), v_first)
                             + _dot(scores[h][1].astype(BF16), v_second))
        s_ref[h] = decay[:, sl] * st + _dot(v[:, sl].T.astype(BF16), k_out[:, sl])


def _hgrn_kernel(qf_ref, ff_ref, vf_ref, qr_ref, fr_ref, vr_ref, lb_ref, trif_ref, lvf_ref, trir_ref, lvr_ref,
                 of_ref, or_ref, sf_ref, sr_ref):
    @pl.when(pl.program_id(1) == 0)
    def _():
        sf_ref[...] = jnp.zeros_like(sf_ref)
        sr_ref[...] = jnp.zeros_like(sr_ref)

    lb = lb_ref[...]
    _hgrn_chunk(qf_ref, ff_ref, vf_ref, lb, trif_ref, lvf_ref, of_ref, sf_ref, rev=False)
    _hgrn_chunk(qr_ref, fr_ref, vr_ref, lb, trir_ref, lvr_ref, or_ref, sr_ref, rev=True)


def _hgrn_scan(hg, lb):
    maps = [_hgrn_maps(rev) for rev in (False, True)]
    blk = lambda d, col: pl.BlockSpec((None, HG_C, HG_W), lambda b, i: (b, _seq_tile(d, i), col))
    state = pltpu.VMEM((HG_HEADS, HG_DIM, HG_DIM), F32)
    out = jax.ShapeDtypeStruct((B, T, HG_W), F32)
    return pl.pallas_call(
        _hgrn_kernel,
        grid=(B, NTS),
        in_specs=[blk(0, 1), blk(0, 2), blk(0, 4), blk(1, 1), blk(1, 3), blk(1, 4),
                  _const_spec((1, HG_W)),
                  _const_spec((HG_C, HG_C)), _const_spec((HG_HALF, HG_HALF)),
                  _const_spec((HG_C, HG_C)), _const_spec((HG_HALF, HG_HALF))],
        out_specs=[blk(0, 0), blk(1, 0)],
        out_shape=[out, out],
        scratch_shapes=[state, state],
        compiler_params=_cparams(("arbitrary", "arbitrary")),
        name="hgrn_scan",
    )(hg, hg, hg, hg, hg, hg, lb, *maps[0], *maps[1])


def _ffn_tail(mix_out, h, mod_ref, g_ref, w_in_ref, w_out_ref):
    h1 = h + mod_ref[2:3, :] * mix_out
    a = _norm_mod(h1, g_ref[...], mod_ref[3:4, :], mod_ref[4:5, :])
    gu = _dot(a.astype(BF16), w_in_ref[...])
    act = _silu(gu[:, :FFN_H]) * gu[:, FFN_H:]
    return h1 + mod_ref[5:6, :] * _dot(act.astype(BF16), w_out_ref[...])


def _gelu_tanh(x):
    return 0.5 * x * (1.0 + jnp.tanh(math.sqrt(2.0 / math.pi) * (x + 0.044715 * (x * x * x))))


def _post_even_kernel(yf_ref, yb_ref, u_ref, of_ref, ob_ref, gate_ref, hl_ref, hc_ref, mod_ref, g_ref,
                      dsk_ref, gw_ref, gb_ref, on_ref, wo_ref, w_in_ref, w_out_ref, o_ref):
    y = yf_ref[...] + yb_ref[...] + u_ref[...] * dsk_ref[...]
    z = _gelu_tanh(y)
    s5 = z * jax.nn.sigmoid(_dot(z.astype(BF16), gw_ref[...]) + gb_ref[...])
    mix = _dot(s5.astype(BF16), wo_ref[0:S5_W, :])
    o = of_ref[...] + ob_ref[...]
    gate = _silu(gate_ref[...])
    for h in range(HG_HEADS):
        sl = slice(h * HG_DIM, (h + 1) * HG_DIM)
        oh = o[:, sl]
        ms = jnp.mean(oh * oh, axis=-1, keepdims=True)
        hn = oh * lax.rsqrt(ms + EPS) * on_ref[...] * gate[:, sl]
        mix = mix + _dot(hn.astype(BF16), wo_ref[S5_W + h * HG_DIM:S5_W + (h + 1) * HG_DIM, :])
    h = jnp.where(pl.program_id(1) == NTS - 1, hc_ref[...], hl_ref[...])
    o_ref[...] = _ffn_tail(mix, h, mod_ref, g_ref, w_in_ref, w_out_ref)


def _post_even(y_f, y_b, o_f, o_b, proj, h_lat, h_ctx, ctx_block, mod_l, g, dsk, gw, gb, on, wo, w_in, w_out, l):
    tok = lambda b, t: (b, t, 0)
    return pl.pallas_call(
        _post_even_kernel,
        grid=(B, NTS),
        in_specs=[
            pl.BlockSpec((None, TS, S5_W), tok),
            pl.BlockSpec((None, TS, S5_W), tok),
            pl.BlockSpec((None, TS, S5_W), tok),
            pl.BlockSpec((None, TS, HG_W), tok),
            pl.BlockSpec((None, TS, HG_W), tok),
            pl.BlockSpec((None, TS, HG_W), lambda b, t: (b, t, 5)),
        ] + _stream_specs(TS, NTS - 1, ctx_block) + [
            _mod_spec(NTS - 1),
            _const_spec((1, D)),
            _const_spec((1, S5_W)),
            _const_spec((S5_W, S5_W)),
            _const_spec((1, S5_W)),
            _const_spec((1, HG_DIM)),
            _layer_spec((D, D), l),
            _layer_spec((D, 2 * FFN_H), l),
            _layer_spec((FFN_H, D), l),
        ],
        out_specs=pl.BlockSpec((None, TS, D), tok),
        out_shape=jax.ShapeDtypeStruct((B, T, D), F32),
        compiler_params=_cparams(("arbitrary", "arbitrary")),
        name="post_even",
    )(y_f, y_b, proj, o_f, o_b, proj, h_lat, h_ctx, mod_l, g, dsk, gw, gb, on, wo, w_in, w_out)


def _post_odd_kernel(mx_ref, h_ref, mod_ref, g_ref, wo_ref, w_in_ref, w_out_ref, o_ref):
    o_ref[...] = _ffn_tail(_dot(mx_ref[...], wo_ref[...]), h_ref[...], mod_ref, g_ref, w_in_ref, w_out_ref)


def _post_odd(mx, hs, mod_l, g, wo, w_in, w_out, l, latent_only):
    nt = NTS - 1 if latent_only else NTS
    tok = lambda b, t: (b, t, 0)
    return pl.pallas_call(
        _post_odd_kernel,
        grid=(B, nt),
        in_specs=[
            pl.BlockSpec((None, TS, D), tok),
            pl.BlockSpec((None, TS, D), tok),
            _mod_spec(NTS - 1),
            _const_spec((1, D)),
            _layer_spec((D, D), l),
            _layer_spec((D, 2 * FFN_H), l),
            _layer_spec((FFN_H, D), l),
        ],
        out_specs=pl.BlockSpec((None, TS, D), tok),
        out_shape=jax.ShapeDtypeStruct((B, nt * TS, D), F32),
        compiler_params=_cparams(("arbitrary", "arbitrary")),
        name="post_odd",
    )(mx, hs, mod_l, g, wo, w_in, w_out)


def _group_rms(xs, r):
    ms = _dot((xs * xs).astype(BF16), r)
    return xs * lax.rsqrt(ms + EPS)


def _rope(xs, cos, sin_signed, first_half):
    rot = jnp.where(first_half, pltpu.roll(xs, LANE - ROPE_DIM // 4, 1), pltpu.roll(xs, ROPE_DIM // 4, 1))
    return xs * cos + rot * sin_signed


def _odd_in_kernel(x_ref, mod_ref, g_ref, w_ref, rope_ref, gv_ref, qan_ref, kvan_ref, wuq_ref, wukv_ref,
                   rpair_ref, rq_ref, rfull_ref, rhalf_ref,
                   dq_ref, k1_ref, k2_ref, dv_ref, mq_ref, mk_ref, mv_ref, *, mla_scale):
    lane = lax.broadcasted_iota(jnp.int32, (1, LANE), 1)
    first_half = (lane % (ROPE_DIM // 2)) < (ROPE_DIM // 4)
    low = lane < DIFF_DIM
    rpair, rq, rfull, rhalf = rpair_ref[...], rq_ref[...], rfull_ref[...], rhalf_ref[...]
    g_dq, g_dk = gv_ref[0:1, :], gv_ref[1:2, :]
    g_qn, g_qr, g_kn, g_kr = gv_ref[2:3, :], gv_ref[3:4, :], gv_ref[4:5, :], gv_ref[5:6, :]
    diff_scale = DIFF_DIM ** -0.5 * LOG2E
    o_cq = 3 * DIFF_W
    o_ckv = o_cq + MLA_Q_RANK
    o_kr = o_ckv + MLA_KV_RANK

    def body(n):
        a = _norm_mod(x_ref[0:n], g_ref[...], mod_ref[0:1, :], mod_ref[1:2, :])
        p = _dot(a.astype(BF16), w_ref[...])
        cos_d, sin_d, cos_m, sin_m = rope_ref[0, 0:n], rope_ref[1, 0:n], rope_ref[2, 0:n], rope_ref[3, 0:n]
        ones = jnp.ones((n, LANE), BF16)

        for h2 in range(0, DIFF_HEADS, 2):
            pair = slice(h2 * LANE, (h2 + 2) * LANE)
            qn = _group_rms(p[:, pair], rpair)
            kn = _group_rms(p[:, DIFF_W + pair.start:DIFF_W + pair.stop], rpair)
            for j in range(2):
                sl = slice((h2 + j) * LANE, (h2 + j + 1) * LANE)
                half = slice(j * LANE, (j + 1) * LANE)
                qh = _rope(qn[:, half] * g_dq, cos_d, sin_d, first_half)
                dq_ref[0:n, sl] = (qh * diff_scale).astype(BF16)
                kh = _rope(kn[:, half] * g_dk, cos_d, sin_d, first_half)
                k1_ref[0:n, sl] = jnp.where(low, kh, 0.0).astype(BF16)
                k2_ref[0:n, sl] = jnp.where(low, 0.0, kh).astype(BF16)
        for h in range(DIFF_HEADS):
            dv_ref[0:n, 2 * h * LANE:(2 * h + 1) * LANE] = (
                p[:, 2 * DIFF_W + h * LANE:2 * DIFF_W + (h + 1) * LANE].astype(BF16))
            dv_ref[0:n, (2 * h + 1) * LANE:(2 * h + 2) * LANE] = ones

        cq = p[:, o_cq:o_ckv]
        cqn = cq * lax.rsqrt(jnp.mean(cq * cq, axis=-1, keepdims=True) + EPS) * qan_ref[...]
        q = _dot(cqn.astype(BF16), wuq_ref[...])
        ckv = p[:, o_ckv:o_kr]
        ckvn = ckv * lax.rsqrt(jnp.mean(ckv * ckv, axis=-1, keepdims=True) + EPS) * kvan_ref[...]
        kv = _dot(ckvn.astype(BF16), wukv_ref[...])
        k_rope = _rope(_group_rms(p[:, o_kr:o_kr + LANE], rhalf) * g_kr, cos_m, sin_m, first_half).astype(BF16)
        for h in range(MLA_HEADS):
            o = 2 * LANE * h
            qn = _group_rms(q[:, o:o + 2 * LANE], rq)
            q_nope = qn[:, :LANE] * g_qn
            q_rope = _rope(qn[:, LANE:] * g_qr, cos_m, sin_m, first_half)
            mq_ref[0:n, o:o + LANE] = (q_nope * mla_scale).astype(BF16)
            mq_ref[0:n, o + LANE:o + 2 * LANE] = (q_rope * mla_scale).astype(BF16)
            mk_ref[0:n, o:o + LANE] = (_group_rms(kv[:, o:o + LANE], rfull) * g_kn).astype(BF16)
            mk_ref[0:n, o + LANE:o + 2 * LANE] = k_rope
            mv_ref[0:n, o:o + LANE] = kv[:, o + LANE:o + 2 * LANE].astype(BF16)
            mv_ref[0:n, o + LANE:o + 2 * LANE] = ones

    _for_tile_rows(True, body)


def _odd_in(hs, mod_l, g, w_all, o, rope, gv, qan, kvan, wuq, wukv, rmats):
    tok = lambda b, t: (b, t, 0)
    sds = lambda n: jax.ShapeDtypeStruct((B, T, n), BF16)
    wide = 2 * LANE * MLA_HEADS
    widths = (DIFF_W, DIFF_W, DIFF_W, wide, wide, wide, wide)
    return pl.pallas_call(
        functools.partial(_odd_in_kernel, mla_scale=(MLA_NOPE + MLA_ROPE) ** -0.5 * LOG2E),
        grid=(B, NTL + 1),
        in_specs=[
            pl.BlockSpec((None, TM, D), tok),
            _mod_spec(NTL),
            _const_spec((1, D)),
            _layer_spec((D, CD_PAD), o),
            pl.BlockSpec((4, TM, LANE), lambda b, t: (0, t, 0)),
            _const_spec((SUBLANE, LANE)),
            _const_spec((1, MLA_Q_RANK)),
            _const_spec((1, MLA_KV_RANK)),
            _const_spec((MLA_Q_RANK, 2 * LANE * MLA_HEADS)),
            _const_spec((MLA_KV_RANK, 2 * LANE * MLA_HEADS)),
            _const_spec((2 * LANE, 2 * LANE)),
            _const_spec((2 * LANE, 2 * LANE)),
            _const_spec((LANE, LANE)),
            _const_spec((LANE, LANE)),
        ],
        out_specs=[pl.BlockSpec((None, TM, n), tok) for n in widths],
        out_shape=[sds(n) for n in widths],
        compiler_params=_cparams(("arbitrary", "arbitrary")),
        name="odd_in_proj",
    )(hs, mod_l, g, w_all, rope, gv, qan, kvan, wuq, wukv, *rmats)


def _exp2_shifted(s):
    return jnp.exp2(s - jnp.max(s, axis=-1, keepdims=True)).astype(BF16)


def _pv_normalised(e, v_aug):
    o = _dot(e, v_aug)
    return o[:, :LANE] / o[:, LANE:]


def _attn_kernel(lam_ref, dq_ref, k1_ref, k2_ref, dv_ref, mq_ref, mk_ref, mv_ref, sub_ref, o_ref,
                 *, out_scale, has_ctx):
    lam = lam_ref[0, 0]

    def body(keys):
        for h in range(DIFF_HEADS):
            sl = slice(h * LANE, (h + 1) * LANE)
            sv = slice(2 * LANE * h, 2 * LANE * (h + 1))
            qh = dq_ref[:, sl]
            o1 = _pv_normalised(_exp2_shifted(_dot_nt(qh, k1_ref[keys, sl])), dv_ref[keys, sv])
            o2 = _pv_normalised(_exp2_shifted(_dot_nt(qh, k2_ref[keys, sl])), dv_ref[keys, sv])
            o = o1 - lam * o2
            o = o * lax.rsqrt(jnp.mean(o * o, axis=-1, keepdims=True) + EPS) * sub_ref[...] * out_scale
            o_ref[:, sl] = o.astype(BF16)
        for h in range(MLA_HEADS):
            sq = slice(2 * LANE * h, 2 * LANE * (h + 1))
            o = _pv_normalised(_exp2_shifted(_dot_nt(mq_ref[:, sq], mk_ref[keys, sq])), mv_ref[keys, sq])
            o_ref[:, DIFF_W + h * LANE:DIFF_W + (h + 1) * LANE] = o.astype(BF16)

    if has_ctx:
        t = pl.program_id(1)
        pl.when(t == 0)(lambda: body(slice(SEQ, T)))
        pl.when(t > 0)(lambda: body(slice(0, T)))
    else:
        body(slice(0, T))


def _attention(lam, dq, k1, k2, dv, mq, mk, mv, sub, out_scale, latent_only):
    has_ctx = not latent_only
    nt = NTS if has_ctx else NTS - 1
    wide = 2 * LANE * MLA_HEADS
    tile = (lambda t: (t + NTS - 1) % NTS) if has_ctx else (lambda t: t)
    qspec = lambda n: pl.BlockSpec((None, TS, n), lambda b, t: (b, tile(t), 0))
    kspec = lambda n: pl.BlockSpec((None, T, n), lambda b, t: (b, 0, 0))
    return pl.pallas_call(
        functools.partial(_attn_kernel, out_scale=out_scale, has_ctx=has_ctx),
        grid=(B, nt),
        in_specs=[
            pl.BlockSpec(memory_space=pltpu.SMEM),
            qspec(DIFF_W), kspec(DIFF_W), kspec(DIFF_W), kspec(wide),
            qspec(wide), kspec(wide), kspec(wide),
            _const_spec((1, LANE)),
        ],
        out_specs=qspec(D),
        out_shape=jax.ShapeDtypeStruct((B, nt * TS, D), BF16),
        compiler_params=_cparams(("arbitrary", "arbitrary")),
        name="attention",
    )(lam, dq, k1, k2, dv, mq, mk, mv, sub)


def _rope_tables():
    n_tok = SEQ
    rows = jnp.repeat(jnp.arange(n_tok // GRID_W, dtype=jnp.int32), GRID_W)
    cols = jnp.tile(jnp.arange(GRID_W, dtype=jnp.int32), n_tok // GRID_W)
    n_freq = ROPE_DIM // 4
    inv = jnp.power(ROPE_BASE, -jnp.arange(n_freq, dtype=F32) / n_freq)
    ang_r = rows.astype(F32)[:, None] * inv
    ang_c = cols.astype(F32)[:, None] * inv
    ang = jnp.concatenate([ang_r, ang_r, ang_c, ang_c], axis=-1)
    sign = jnp.where((jnp.arange(ROPE_DIM) % (ROPE_DIM // 2)) < n_freq, -1.0, 1.0).astype(F32)
    cos = jnp.concatenate([jnp.cos(ang), jnp.ones((CTX, ROPE_DIM), F32)], axis=0)
    sin = jnp.concatenate([jnp.sin(ang) * sign, jnp.zeros((CTX, ROPE_DIM), F32)], axis=0)
    one, zero = jnp.ones_like(cos), jnp.zeros_like(sin)
    cat = lambda a, b: jnp.concatenate([a, b], axis=-1)
    return jnp.stack([cat(cos, cos), cat(sin, sin), cat(cos, one), cat(sin, zero)])


def _averaging_mats():
    idx = np.arange(2 * LANE)
    same64 = (idx[:, None] // DIFF_DIM) == (idx[None, :] // DIFF_DIM)
    rpair = np.where(same64, 1.0 / DIFF_DIM, 0.0)
    rfull = np.full((LANE, LANE), 1.0 / LANE)
    il = np.arange(LANE)
    rhalf = np.where((il[:, None] < MLA_ROPE) & (il[None, :] < MLA_ROPE), 1.0 / MLA_ROPE, 0.0)
    rq = np.zeros((2 * LANE, 2 * LANE))
    rq[:LANE, :LANE] = rfull
    rq[LANE:, LANE:] = rhalf
    return tuple(jnp.asarray(m, BF16) for m in (rpair, rq, rfull, rhalf))


def kernel(x, c, ctx, c_ctx, ada_w, ada_b, norm_mix, norm_ffn, w_out, ffn_w_in, ffn_w_out, ab_w_in, s5_lambda_re, s5_lambda_im, s5_log_step, s5_b_re, s5_b_im, s5_c_re, s5_c_im, s5_d, s5_glu_w, s5_glu_b, hgrn_lb_logits, hgrn_out_norm, cd_w_in, diff_lambda, diff_qk_norm, diff_subln, mla_q_a_norm, mla_kv_a_norm, mla_w_uq, mla_w_ukv, mla_nope_norm, mla_rope_norm):
    assert x.shape == (B, SEQ, D) and ctx.shape == (B, CTX, D)
    mod = _modulation(c, c_ctx, ada_w, ada_b)
    wo_all = w_out.astype(BF16)
    w1_all = ffn_w_in.astype(BF16)
    w2_all = ffn_w_out.astype(BF16)
    wab_all = ab_w_in.astype(BF16)
    wcd_all = jnp.pad(cd_w_in, ((0, 0), (0, 0), (0, CD_PAD - CD_IN))).astype(BF16)
    h_lat, h_ctx, ctx_block = x, ctx, 0
    rope = _rope_tables()
    rmats = _averaging_mats()

    lb_p = jax.nn.softmax(hgrn_lb_logits.astype(F32), axis=0)
    lower_bounds = jnp.cumsum(lb_p, axis=0) - lb_p[0:1]

    for l in range(DEPTH):
        last = l == DEPTH - 1
        g_mix = norm_mix[l].reshape(1, D)
        g_ffn = norm_ffn[l].reshape(1, D)
        if l % 2 == 0:
            e = l // 2
            proj = _even_in(h_lat, h_ctx, ctx_block, mod[l], g_mix, wab_all, e)
            s5w_in, s5w_out, a_r, a_i = _s5_params(s5_lambda_re[e], s5_lambda_im[e], s5_log_step[e],
                                                   s5_b_re[e], s5_b_im[e], s5_c_re[e], s5_c_im[e])
            y_f, y_b = _s5_scan(proj, s5w_in, s5w_out, a_r, a_i)
            lb = lower_bounds[e].reshape(1, HG_W)
            o_f, o_b = _hgrn_scan(proj, lb)
            hs = _post_even(y_f, y_b, o_f, o_b, proj, h_lat, h_ctx, ctx_block, mod[l], g_ffn,
                            s5_d[e].reshape(1, S5_W), s5_glu_w[e].astype(BF16), s5_glu_b[e].reshape(1, S5_W),
                            hgrn_out_norm[e].reshape(1, HG_DIM), wo_all, w1_all, w2_all, l)
        else:
            o = l // 2
            lam_init = 0.8 - 0.6 * math.exp(-0.3 * l)
            lv = diff_lambda[o].astype(F32)
            lam = (jnp.exp(jnp.sum(lv[0] * lv[1])) - jnp.exp(jnp.sum(lv[2] * lv[3])) + lam_init).reshape(1, 1)
            wuq = mla_w_uq[o].reshape(MLA_Q_RANK, MLA_HEADS, MLA_NOPE + MLA_ROPE)
            wuq = jnp.pad(wuq, ((0, 0), (0, 0), (0, 2 * LANE - MLA_NOPE - MLA_ROPE)))
            wuq = wuq.reshape(MLA_Q_RANK, 2 * LANE * MLA_HEADS).astype(BF16)
            pad_r = lambda v: jnp.pad(v, (0, LANE - MLA_ROPE))
            gv = jnp.zeros((SUBLANE, LANE), F32)
            gv = gv.at[0].set(jnp.tile(diff_qk_norm[o, 0], 2)).at[1].set(jnp.tile(diff_qk_norm[o, 1], 2))
            gv = gv.at[2].set(mla_nope_norm[o, 0]).at[3].set(pad_r(mla_rope_norm[o, 0]))
            gv = gv.at[4].set(mla_nope_norm[o, 1]).at[5].set(pad_r(mla_rope_norm[o, 1]))
            parts = _odd_in(hs, mod[l], g_mix, wcd_all, o, rope, gv, mla_q_a_norm[o].reshape(1, MLA_Q_RANK),
                            mla_kv_a_norm[o].reshape(1, MLA_KV_RANK), wuq, mla_w_ukv[o].astype(BF16), rmats)
            mx = _attention(lam, *parts, diff_subln[o].reshape(1, LANE), 1.0 - lam_init, latent_only=last)
            hs = _post_odd(mx, hs, mod[l], g_ffn, wo_all, w1_all, w2_all, l, latent_only=last)
        h_lat, h_ctx, ctx_block = hs, hs, SEQ // CTX
    return hs
```

```python
import functools
import math

import numpy as np
import jax
import jax.numpy as jnp
from jax import lax
from jax.experimental import pallas as pl
from jax.experimental.pallas import tpu as pltpu

F32 = jnp.float32
BF16 = jnp.bfloat16

D = 1024
B = 8
SEQ = 2048
CTX = 256
T = CTX + SEQ
DEPTH = 4
GRID_W = 64
FFN_H = ((8 * D + 3 * 256 - 1) // (3 * 256)) * 256
S5_W = D // 2
S5_GROUP = 16
S5_GROUPS = S5_W // S5_GROUP
S5_STATE = 64
HG_HEADS = 4
HG_DIM = D // 8
HG_W = HG_HEADS * HG_DIM
MAX_EXP_ARG = 60.0
DIFF_HEADS = 4
DIFF_DIM = D // 16
DIFF_W = DIFF_HEADS * 2 * DIFF_DIM
MLA_HEADS = 4
MLA_NOPE = D // 8
MLA_ROPE = D // 16
MLA_V = D // 8
MLA_Q_RANK = 3 * D // 8
MLA_KV_RANK = D // 4
ROPE_DIM = D // 16
ROPE_BASE = 10000.0
EPS = 1e-6
AB_IN = S5_W + 5 * HG_W
CD_IN = 3 * DIFF_W + MLA_Q_RANK + MLA_KV_RANK + MLA_ROPE
CD_PAD = CD_IN + 64

LANE = 128
SUBLANE = 8
TS = CTX
NTS = T // TS
TM = 512
NTL = SEQ // TM
MOD_ROWS = 16
CTX_ROW = B
S5_LC = 512
S5_NC = S5_GROUPS * S5_STATE // S5_LC
S5_UC = S5_LC // S5_STATE * S5_GROUP
S5_SB = 16
HG_C = TS
VMEM_LIMIT = 56 * 1024 * 1024
LOG2E = math.log2(math.e)


def _cparams(sem):
    return pltpu.CompilerParams(dimension_semantics=sem, vmem_limit_bytes=VMEM_LIMIT)


def _const_spec(shape):
    n = len(shape)
    return pl.BlockSpec(shape, lambda *_: (0,) * n, pipeline_mode=pl.Buffered(1))


def _layer_spec(shape, l):
    n = len(shape)
    return pl.BlockSpec((None,) + tuple(shape), lambda *_: (l,) + (0,) * n, pipeline_mode=pl.Buffered(1))


def _stream_specs(rows, n_lat, ctx_block):
    return [pl.BlockSpec((None, rows, D), lambda b, t: (b, jnp.minimum(t, n_lat - 1), 0)),
            pl.BlockSpec((None, CTX, D), lambda b, t: (b, ctx_block, 0))]


def _silu(x):
    return x * jax.nn.sigmoid(x)


def _norm_mod(x, g, shift, scale):
    ms = jnp.mean(x * x, axis=-1, keepdims=True)
    return x * lax.rsqrt(ms + EPS) * g * (1.0 + scale) + shift


def _dot(a, b):
    return jnp.dot(a, b, preferred_element_type=F32)


def _dot_nt(a, b):
    return lax.dot_general(a, b, (((1,), (1,)), ((), ())), preferred_element_type=F32)


def _mod_kernel(s_ref, w_ref, b_ref, o_ref):
    s = _silu(s_ref[...])
    o_ref[...] = _dot(s.astype(BF16), w_ref[...].astype(BF16)) + b_ref[...]


def _modulation(c, c_ctx, ada_w, ada_b):
    s = jnp.zeros((MOD_ROWS, D), F32).at[:B].set(c).at[CTX_ROW].set(c_ctx)
    nb = 1536
    out = pl.pallas_call(
        _mod_kernel,
        grid=(DEPTH, 6 * D // nb),
        in_specs=[
            pl.BlockSpec((MOD_ROWS, D), lambda l, n: (0, 0)),
            pl.BlockSpec((None, D, nb), lambda l, n: (l, 0, n)),
            pl.BlockSpec((None, 1, nb), lambda l, n: (l, 0, n)),
        ],
        out_specs=pl.BlockSpec((None, MOD_ROWS, nb), lambda l, n: (l, 0, n)),
        out_shape=jax.ShapeDtypeStruct((DEPTH, MOD_ROWS, 6 * D), F32),
        compiler_params=_cparams(("arbitrary", "arbitrary")),
        name="adaln_mod",
    )(s, ada_w, ada_b.reshape(DEPTH, 1, 6 * D))
    return out.reshape(DEPTH, MOD_ROWS, 6, D)


def _mod_spec(ctx_tile):
    return pl.BlockSpec((None, 6, D), lambda b, t: (jnp.where(t == ctx_tile, CTX_ROW, b), 0, 0))


def _for_tile_rows(has_ctx, body):
    if has_ctx:
        t = pl.program_id(1)
        pl.when(t < NTL)(lambda: body(TM))
        pl.when(t == NTL)(lambda: body(CTX))
    else:
        body(TM)


def _even_in_kernel(xl_ref, xc_ref, mod_ref, g_ref, w_ref, p_ref):
    def body(n):
        x = xl_ref[...] if n == TM else xc_ref[...]
        a = _norm_mod(x, g_ref[...], mod_ref[0:1, :], mod_ref[1:2, :])
        p_ref[0:n] = _dot(a.astype(BF16), w_ref[...])

    _for_tile_rows(True, body)


def _even_in(h_lat, h_ctx, ctx_block, mod_l, g, w_all, e):
    return pl.pallas_call(
        _even_in_kernel,
        grid=(B, NTL + 1),
        in_specs=_stream_specs(TM, NTL, ctx_block) + [
            _mod_spec(NTL),
            _const_spec((1, D)),
            _layer_spec((D, AB_IN), e),
        ],
        out_specs=pl.BlockSpec((None, TM, AB_IN), lambda b, t: (b, t, 0)),
        out_shape=jax.ShapeDtypeStruct((B, T, AB_IN), F32),
        compiler_params=_cparams(("arbitrary", "arbitrary")),
        name="even_in_proj",
    )(h_lat, h_ctx, mod_l, g, w_all)


def _s5_kernel(uf_ref, wf_ref, cf_ref, arf_ref, aif_ref, ur_ref, wr_ref, cr_ref, arr_ref, air_ref,
               yf_ref, yr_ref, uxf_ref, yxf_ref, buf_ref, hf_ref, uxr_ref, yxr_ref, bur_ref, hr_ref):
    @pl.when(pl.program_id(1) == 0)
    def _():
        hf_ref[...] = jnp.zeros_like(hf_ref)
        hr_ref[...] = jnp.zeros_like(hr_ref)

    n_sb = TS // S5_SB
    dirs = (
        (False, uf_ref, wf_ref, cf_ref, arf_ref[...], aif_ref[...], yf_ref, uxf_ref, yxf_ref, buf_ref, hf_ref),
        (True, ur_ref, wr_ref, cr_ref, arr_ref[...], air_ref[...], yr_ref, uxr_ref, yxr_ref, bur_ref, hr_ref),
    )

    def rows_of(sb):
        return slice(sb * S5_SB * B, (sb + 1) * S5_SB * B)

    def sub_block(rev, n):
        return n_sb - 1 - n if rev else n

    def project(d, n):
        rev, _, w_ref, _, _, _, _, ux_ref, _, bu_ref, _ = d
        rows = rows_of(sub_block(rev, n))
        bu_ref[rows, :] = _dot(ux_ref[rows, :].astype(BF16), w_ref[...])

    for _, u_ref, _, _, _, _, _, ux_ref, _, _, _ in dirs:
        for b in range(B):
            ux_ref[pl.ds(b, TS, stride=B), :] = u_ref[b]
    state = [(d[10][0], d[10][1]) for d in dirs]
    for d in dirs:
        project(d, 0)
    for n in range(n_sb):
        if n + 1 < n_sb:
            for d in dirs:
                project(d, n + 1)
        for k in range(S5_SB):
            for i, (rev, _, _, _, ar, ai, _, _, _, bu_ref, _) in enumerate(dirs):
                t = S5_SB - 1 - k if rev else k
                r0 = rows_of(sub_block(rev, n)).start + t * B
                r = slice(r0, r0 + B)
                hr, hi = state[i]
                nr = ar * hr - ai * hi + bu_ref[r, 0:S5_LC]
                ni = ar * hi + ai * hr + bu_ref[r, S5_LC:2 * S5_LC]
                bu_ref[r, 0:S5_LC] = nr
                bu_ref[r, S5_LC:2 * S5_LC] = ni
                state[i] = (nr, ni)
        for rev, _, _, c_ref, _, _, _, _, yx_ref, bu_ref, _ in dirs:
            rows = rows_of(sub_block(rev, n))
            yx_ref[rows, :] = _dot(bu_ref[rows, :].astype(BF16), c_ref[...])
    for i, (_, _, _, _, _, _, y_ref, _, yx_ref, _, h_ref) in enumerate(dirs):
        h_ref[0], h_ref[1] = state[i]
        for b in range(B):
            y_ref[b] = yx_ref[pl.ds(b, TS, stride=B), :]


def _seq_tile(d, i):
    return jnp.where(i == 0, NTS - 1, jnp.where(d == 0, i - 1, NTS - 1 - i))


def _s5_scan(proj, w_in, w_out, a_r, a_i):
    def operands(d):
        par = lambda *shape: pl.BlockSpec((None, None) + shape, lambda c, i: (d, c, 0, 0))
        return [pl.BlockSpec((B, TS, S5_UC), lambda c, i: (0, _seq_tile(d, i), c)),
                par(S5_UC, 2 * S5_LC), par(2 * S5_LC, S5_UC), par(B, S5_LC), par(B, S5_LC)]

    scratch = [pltpu.VMEM((TS * B, S5_UC), F32), pltpu.VMEM((TS * B, S5_UC), F32),
               pltpu.VMEM((TS * B, 2 * S5_LC), F32), pltpu.VMEM((2, B, S5_LC), F32)]
    out = jax.ShapeDtypeStruct((B, T, S5_W), F32)
    return pl.pallas_call(
        _s5_kernel,
        grid=(S5_NC, NTS),
        in_specs=operands(0) + operands(1),
        out_specs=[pl.BlockSpec((B, TS, S5_UC), lambda c, i, d=d: (0, _seq_tile(d, i), c)) for d in (0, 1)],
        out_shape=[out, out],
        scratch_shapes=scratch + scratch,
        compiler_params=_cparams(("arbitrary", "arbitrary")),
        name="s5_scan",
    )(proj, w_in, w_out, a_r, a_i, proj, w_in, w_out, a_r, a_i)


def _s5_params(lam_re, lam_im, log_step, b_re, b_im, c_re, c_im):
    lr = jnp.minimum(lam_re.astype(F32), -1e-4)
    li = lam_im.astype(F32)
    step = jnp.exp(log_step.astype(F32))[..., None]
    mag = jnp.exp(lr * step)
    a_r = mag * jnp.cos(li * step)
    a_i = mag * jnp.sin(li * step)
    den = lr * lr + li * li
    coef_r = ((a_r - 1) * lr + a_i * li) / den
    coef_i = (a_i * lr - (a_r - 1) * li) / den
    br = b_re.astype(F32)
    bi = b_im.astype(F32)
    bb_r = coef_r[..., None] * br - coef_i[..., None] * bi
    bb_i = coef_r[..., None] * bi + coef_i[..., None] * br
    gpc = S5_LC // S5_STATE
    eye = jnp.eye(gpc, dtype=F32)

    def in_blocks(bb):
        bb = bb.reshape(2, S5_NC, gpc, S5_STATE, S5_GROUP)
        return jnp.einsum('dngpc,gh->dngchp', bb, eye).reshape(2, S5_NC, S5_UC, S5_LC)

    def out_blocks(cc):
        cc = cc.reshape(2, S5_NC, gpc, S5_GROUP, S5_STATE)
        return jnp.einsum('dngcp,gh->dngphc', cc, eye).reshape(2, S5_NC, S5_LC, S5_UC)

    w_in = jnp.concatenate([in_blocks(bb_r), in_blocks(bb_i)], axis=-1).astype(BF16)
    w_out = jnp.concatenate([out_blocks(c_re.astype(F32)), -out_blocks(c_im.astype(F32))], axis=-2).astype(BF16)

    def lanes(a):
        return jnp.broadcast_to(a.reshape(2, S5_NC, 1, S5_LC), (2, S5_NC, B, S5_LC))

    return w_in, w_out, lanes(a_r), lanes(a_i)


HG_HALF = HG_C // 2
HG_LEVELS = tuple(2 ** e for e in range(int(math.log2(HG_HALF))))


def _hgrn_maps(rev):
    pos = np.arange(HG_C)
    ph = np.arange(HG_HALF)
    if rev:
        pos = HG_C - 1 - pos
        ph = HG_HALF - 1 - ph
    tri = (pos[None, :] <= pos[:, None]).astype(np.float32)
    pt, ps = ph[:, None], ph[None, :]
    lv = np.full((HG_HALF, HG_HALF), -1, np.int32)
    lv[pt == ps] = len(HG_LEVELS)
    for e, m in enumerate(HG_LEVELS):
        x, y = pt // m, ps // m
        lv[(x == y + 1) & (x % 2 == 1)] = e
    return jnp.asarray(tri, BF16), jnp.asarray(lv)


def _hgrn_chunk(q_ref, f_ref, v_ref, lb, tri_ref, lv_ref, o_ref, s_ref, rev):
    c = HG_C
    fl = f_ref[...]
    qb = _silu(q_ref[...]).astype(BF16)
    v = v_ref[...]
    x2 = fl * LOG2E
    a = jnp.exp2(-jnp.abs(x2))
    s1 = 1.0 + a
    r = 1.0 / s1
    kb = ((1.0 - lb) * jnp.where(x2 >= 0.0, a * r, r)).astype(BF16)
    t2 = jnp.exp2(jnp.minimum(-x2, MAX_EXP_ARG * LOG2E))
    lf = jnp.minimum(x2, 0.0) - jnp.log2(s1) + jnp.log2(1.0 + lb * t2)

    row = lax.broadcasted_iota(jnp.int32, (c, 1), 0)
    pos = (c - 1 - row) if rev else row
    first, second = (slice(HG_HALF, c), slice(0, HG_HALF)) if rev else (slice(0, HG_HALF), slice(HG_HALF, c))
    halves = (first, second)

    def from_earlier(x, j):
        return pltpu.roll(x, (c - j) if rev else j, 0)

    def from_later(x, j):
        return pltpu.roll(x, j if rev else (c - j), 0)

    def head(x, h, rows):
        return x[rows, h * HG_DIM:(h + 1) * HG_DIM]

    hi = lf.astype(BF16)
    r1 = lf - hi.astype(F32)
    mid = r1.astype(BF16)
    lo = (r1 - mid.astype(F32)).astype(BF16)
    tri = tri_ref[...]
    cum = _dot(tri, hi) + _dot(tri, mid) + _dot(tri, lo)

    lv = lv_ref[...]
    diag = len(HG_LEVELS)
    scores = [[jnp.where(lv == diag, _dot_nt(head(qb, h, r), head(kb, h, r)), 0.0) for r in halves]
              for h in range(HG_HEADS)]

    g_end = cum
    for e, m in enumerate(HG_LEVELS):
        qm = qb * jnp.exp2(cum - from_earlier(g_end, m)).astype(BF16)
        km = kb * jnp.exp2(g_end - cum).astype(BF16)
        for h in range(HG_HEADS):
            for j, r in enumerate(halves):
                scores[h][j] = jnp.where(lv == e, _dot_nt(head(qm, h, r), head(km, h, r)), scores[h][j])
        g_end = jnp.where((pos & m) != 0, g_end, from_later(g_end, m))
    qm = qb * jnp.exp2(cum - from_earlier(g_end, HG_HALF)).astype(BF16)
    km = kb * jnp.exp2(g_end - cum).astype(BF16)
    cross = [_dot_nt(head(qm, h, second), head(km, h, first)) for h in range(HG_HEADS)]
    g_end = jnp.where((pos & HG_HALF) != 0, g_end, from_later(g_end, HG_HALF))

    q_in = qb * jnp.exp2(cum).astype(BF16)
    k_out = kb * jnp.exp2(g_end - cum).astype(BF16)
    decay = jnp.exp2(g_end[0:1, :])
    vb = v.astype(BF16)
    for h in range(HG_HEADS):
        sl = slice(h * HG_DIM, (h + 1) * HG_DIM)
        st = s_ref[h]
        inter = _dot_nt(q_in[:, sl], st.astype(BF16))
        v_first, v_second = head(vb, h, first), head(vb, h, second)
        o_ref[first, sl] = inter[first] + _dot(scores[h][0].astype(BF16), v_first)
        o_ref[second, sl] = (inter[second] + _dot(cross[h].astype(BF16), v_first)
                             + _dot(scores[h][1].astype(BF16), v_second))
        s_ref[h] = decay[:, sl] * st + _dot(v[:, sl].T.astype(BF16), k_out[:, sl])


def _hgrn_kernel(qf_ref, ff_ref, vf_ref, qr_ref, fr_ref, vr_ref, lb_ref, trif_ref, lvf_ref, trir_ref, lvr_ref,
                 of_ref, or_ref, sf_ref, sr_ref):
    @pl.when(pl.program_id(1) == 0)
    def _():
        sf_ref[...] = jnp.zeros_like(sf_ref)
        sr_ref[...] = jnp.zeros_like(sr_ref)

    lb = lb_ref[...]
    _hgrn_chunk(qf_ref, ff_ref, vf_ref, lb, trif_ref, lvf_ref, of_ref, sf_ref, rev=False)
    _hgrn_chunk(qr_ref, fr_ref, vr_ref, lb, trir_ref, lvr_ref, or_ref, sr_ref, rev=True)


def _hgrn_scan(hg, lb):
    maps = [_hgrn_maps(rev) for rev in (False, True)]
    blk = lambda d, col: pl.BlockSpec((None, HG_C, HG_W), lambda b, i: (b, _seq_tile(d, i), col))
    state = pltpu.VMEM((HG_HEADS, HG_DIM, HG_DIM), F32)
    out = jax.ShapeDtypeStruct((B, T, HG_W), F32)
    return pl.pallas_call(
        _hgrn_kernel,
        grid=(B, NTS),
        in_specs=[blk(0, 1), blk(0, 2), blk(0, 4), blk(1, 1), blk(1, 3), blk(1, 4),
                  _const_spec((1, HG_W)),
                  _const_spec((HG_C, HG_C)), _const_spec((HG_HALF, HG_HALF)),
                  _const_spec((HG_C, HG_C)), _const_spec((HG_HALF, HG_HALF))],
        out_specs=[blk(0, 0), blk(1, 0)],
        out_shape=[out, out],
        scratch_shapes=[state, state],
        compiler_params=_cparams(("arbitrary", "arbitrary")),
        name="hgrn_scan",
    )(hg, hg, hg, hg, hg, hg, lb, *maps[0], *maps[1])


def _ffn_tail(mix_out, h, mod_ref, g_ref, w_in_ref, w_out_ref):
    h1 = h + mod_ref[2:3, :] * mix_out
    a = _norm_mod(h1, g_ref[...], mod_ref[3:4, :], mod_ref[4:5, :])
    gu = _dot(a.astype(BF16), w_in_ref[...])
    act = _silu(gu[:, :FFN_H]) * gu[:, FFN_H:]
    return h1 + mod_ref[5:6, :] * _dot(act.astype(BF16), w_out_ref[...])


def _gelu_tanh(x):
    return 0.5 * x * (1.0 + jnp.tanh(math.sqrt(2.0 / math.pi) * (x + 0.044715 * (x * x * x))))


def _post_even_kernel(yf_ref, yb_ref, u_ref, of_ref, ob_ref, gate_ref, hl_ref, hc_ref, mod_ref, g_ref,
                      dsk_ref, gw_ref, gb_ref, on_ref, wo_ref, w_in_ref, w_out_ref, o_ref):
    y = yf_ref[...] + yb_ref[...] + u_ref[...] * dsk_ref[...]
    z = _gelu_tanh(y)
    s5 = z * jax.nn.sigmoid(_dot(z.astype(BF16), gw_ref[...]) + gb_ref[...])
    mix = _dot(s5.astype(BF16), wo_ref[0:S5_W, :])
    o = of_ref[...] + ob_ref[...]
    gate = _silu(gate_ref[...])
    for h in range(HG_HEADS):
        sl = slice(h * HG_DIM, (h + 1) * HG_DIM)
        oh = o[:, sl]
        ms = jnp.mean(oh * oh, axis=-1, keepdims=True)
        hn = oh * lax.rsqrt(ms + EPS) * on_ref[...] * gate[:, sl]
        mix = mix + _dot(hn.astype(BF16), wo_ref[S5_W + h * HG_DIM:S5_W + (h + 1) * HG_DIM, :])
    h = jnp.where(pl.program_id(1) == NTS - 1, hc_ref[...], hl_ref[...])
    o_ref[...] = _ffn_tail(mix, h, mod_ref, g_ref, w_in_ref, w_out_ref)


def _post_even(y_f, y_b, o_f, o_b, proj, h_lat, h_ctx, ctx_block, mod_l, g, dsk, gw, gb, on, wo, w_in, w_out, l):
    tok = lambda b, t: (b, t, 0)
    return pl.pallas_call(
        _post_even_kernel,
        grid=(B, NTS),
        in_specs=[
            pl.BlockSpec((None, TS, S5_W), tok),
            pl.BlockSpec((None, TS, S5_W), tok),
            pl.BlockSpec((None, TS, S5_W), tok),
            pl.BlockSpec((None, TS, HG_W), tok),
            pl.BlockSpec((None, TS, HG_W), tok),
            pl.BlockSpec((None, TS, HG_W), lambda b, t: (b, t, 5)),
        ] + _stream_specs(TS, NTS - 1, ctx_block) + [
            _mod_spec(NTS - 1),
            _const_spec((1, D)),
            _const_spec((1, S5_W)),
            _const_spec((S5_W, S5_W)),
            _const_spec((1, S5_W)),
            _const_spec((1, HG_DIM)),
            _layer_spec((D, D), l),
            _layer_spec((D, 2 * FFN_H), l),
            _layer_spec((FFN_H, D), l),
        ],
        out_specs=pl.BlockSpec((None, TS, D), tok),
        out_shape=jax.ShapeDtypeStruct((B, T, D), F32),
        compiler_params=_cparams(("arbitrary", "arbitrary")),
        name="post_even",
    )(y_f, y_b, proj, o_f, o_b, proj, h_lat, h_ctx, mod_l, g, dsk, gw, gb, on, wo, w_in, w_out)


def _post_odd_kernel(mx_ref, h_ref, mod_ref, g_ref, wo_ref, w_in_ref, w_out_ref, o_ref):
    o_ref[...] = _ffn_tail(_dot(mx_ref[...], wo_ref[...]), h_ref[...], mod_ref, g_ref, w_in_ref, w_out_ref)


def _post_odd(mx, hs, mod_l, g, wo, w_in, w_out, l, latent_only):
    nt = NTS - 1 if latent_only else NTS
    tok = lambda b, t: (b, t, 0)
    return pl.pallas_call(
        _post_odd_kernel,
        grid=(B, nt),
        in_specs=[
            pl.BlockSpec((None, TS, D), tok),
            pl.BlockSpec((None, TS, D), tok),
            _mod_spec(NTS - 1),
            _const_spec((1, D)),
            _layer_spec((D, D), l),
            _layer_spec((D, 2 * FFN_H), l),
            _layer_spec((FFN_H, D), l),
        ],
        out_specs=pl.BlockSpec((None, TS, D), tok),
        out_shape=jax.ShapeDtypeStruct((B, nt * TS, D), F32),
        compiler_params=_cparams(("arbitrary", "arbitrary")),
        name="post_odd",
    )(mx, hs, mod_l, g, wo, w_in, w_out)


def _group_rms(xs, r):
    ms = _dot((xs * xs).astype(BF16), r)
    return xs * lax.rsqrt(ms + EPS)


def _rope(xs, cos, sin_signed, first_half):
    rot = jnp.where(first_half, pltpu.roll(xs, LANE - ROPE_DIM // 4, 1), pltpu.roll(xs, ROPE_DIM // 4, 1))
    return xs * cos + rot * sin_signed


def _odd_in_kernel(x_ref, mod_ref, g_ref, w_ref, rope_ref, gv_ref, qan_ref, kvan_ref, wuq_ref, wukv_ref,
                   rpair_ref, rq_ref, rfull_ref, rhalf_ref,
                   dq_ref, k1_ref, k2_ref, dv_ref, mq_ref, mk_ref, mv_ref, *, mla_scale):
    lane = lax.broadcasted_iota(jnp.int32, (1, LANE), 1)
    first_half = (lane % (ROPE_DIM // 2)) < (ROPE_DIM // 4)
    low = lane < DIFF_DIM
    rpair, rq, rfull, rhalf = rpair_ref[...], rq_ref[...], rfull_ref[...], rhalf_ref[...]
    g_dq, g_dk = gv_ref[0:1, :], gv_ref[1:2, :]
    g_qn, g_qr, g_kn, g_kr = gv_ref[2:3, :], gv_ref[3:4, :], gv_ref[4:5, :], gv_ref[5:6, :]
    diff_scale = DIFF_DIM ** -0.5 * LOG2E
    o_cq = 3 * DIFF_W
    o_ckv = o_cq + MLA_Q_RANK
    o_kr = o_ckv + MLA_KV_RANK

    def body(n):
        a = _norm_mod(x_ref[0:n], g_ref[...], mod_ref[0:1, :], mod_ref[1:2, :])
        p = _dot(a.astype(BF16), w_ref[...])
        cos_d, sin_d, cos_m, sin_m = rope_ref[0, 0:n], rope_ref[1, 0:n], rope_ref[2, 0:n], rope_ref[3, 0:n]
        ones = jnp.ones((n, LANE), BF16)

        for h2 in range(0, DIFF_HEADS, 2):
            pair = slice(h2 * LANE, (h2 + 2) * LANE)
            qn = _group_rms(p[:, pair], rpair)
            kn = _group_rms(p[:, DIFF_W + pair.start:DIFF_W + pair.stop], rpair)
            for j in range(2):
                sl = slice((h2 + j) * LANE, (h2 + j + 1) * LANE)
                half = slice(j * LANE, (j + 1) * LANE)
                qh = _rope(qn[:, half] * g_dq, cos_d, sin_d, first_half)
                dq_ref[0:n, sl] = (qh * diff_scale).astype(BF16)
                kh = _rope(kn[:, half] * g_dk, cos_d, sin_d, first_half)
                k1_ref[0:n, sl] = jnp.where(low, kh, 0.0).astype(BF16)
                k2_ref[0:n, sl] = jnp.where(low, 0.0, kh).astype(BF16)
        for h in range(DIFF_HEADS):
            dv_ref[0:n, 2 * h * LANE:(2 * h + 1) * LANE] = (
                p[:, 2 * DIFF_W + h * LANE:2 * DIFF_W + (h + 1) * LANE].astype(BF16))
            dv_ref[0:n, (2 * h + 1) * LANE:(2 * h + 2) * LANE] = ones

        cq = p[:, o_cq:o_ckv]
        cqn = cq * lax.rsqrt(jnp.mean(cq * cq, axis=-1, keepdims=True) + EPS) * qan_ref[...]
        q = _dot(cqn.astype(BF16), wuq_ref[...])
        ckv = p[:, o_ckv:o_kr]
        ckvn = ckv * lax.rsqrt(jnp.mean(ckv * ckv, axis=-1, keepdims=True) + EPS) * kvan_ref[...]
        kv = _dot(ckvn.astype(BF16), wukv_ref[...])
        k_rope = _rope(_group_rms(p[:, o_kr:o_kr + LANE], rhalf) * g_kr, cos_m, sin_m, first_half).astype(BF16)
        for h in range(MLA_HEADS):
            o = 2 * LANE * h
            qn = _group_rms(q[:, o:o + 2 * LANE], rq)
            q_nope = qn[:, :LANE] * g_qn
            q_rope = _rope(qn[:, LANE:] * g_qr, cos_m, sin_m, first_half)
            mq_ref[0:n, o:o + LANE] = (q_nope * mla_scale).astype(BF16)
            mq_ref[0:n, o + LANE:o + 2 * LANE] = (q_rope * mla_scale).astype(BF16)
            mk_ref[0:n, o:o + LANE] = (_group_rms(kv[:, o:o + LANE], rfull) * g_kn).astype(BF16)
            mk_ref[0:n, o + LANE:o + 2 * LANE] = k_rope
            mv_ref[0:n, o:o + LANE] = kv[:, o + LANE:o + 2 * LANE].astype(BF16)
            mv_ref[0:n, o + LANE:o + 2 * LANE] = ones

    _for_tile_rows(True, body)


def _odd_in(hs, mod_l, g, w_all, o, rope, gv, qan, kvan, wuq, wukv, rmats):
    tok = lambda b, t: (b, t, 0)
    sds = lambda n: jax.ShapeDtypeStruct((B, T, n), BF16)
    wide = 2 * LANE * MLA_HEADS
    widths = (DIFF_W, DIFF_W, DIFF_W, wide, wide, wide, wide)
    return pl.pallas_call(
        functools.partial(_odd_in_kernel, mla_scale=(MLA_NOPE + MLA_ROPE) ** -0.5 * LOG2E),
        grid=(B, NTL + 1),
        in_specs=[
            pl.BlockSpec((None, TM, D), tok),
            _mod_spec(NTL),
            _const_spec((1, D)),
            _layer_spec((D, CD_PAD), o),
            pl.BlockSpec((4, TM, LANE), lambda b, t: (0, t, 0)),
            _const_spec((SUBLANE, LANE)),
            _const_spec((1, MLA_Q_RANK)),
            _const_spec((1, MLA_KV_RANK)),
            _const_spec((MLA_Q_RANK, 2 * LANE * MLA_HEADS)),
            _const_spec((MLA_KV_RANK, 2 * LANE * MLA_HEADS)),
            _const_spec((2 * LANE, 2 * LANE)),
            _const_spec((2 * LANE, 2 * LANE)),
            _const_spec((LANE, LANE)),
            _const_spec((LANE, LANE)),
        ],
        out_specs=[pl.BlockSpec((None, TM, n), tok) for n in widths],
        out_shape=[sds(n) for n in widths],
        compiler_params=_cparams(("arbitrary", "arbitrary")),
        name="odd_in_proj",
    )(hs, mod_l, g, w_all, rope, gv, qan, kvan, wuq, wukv, *rmats)


def _exp2_shifted(s):
    return jnp.exp2(s - jnp.max(s, axis=-1, keepdims=True)).astype(BF16)


def _pv_normalised(e, v_aug):
    o = _dot(e, v_aug)
    return o[:, :LANE] / o[:, LANE:]


def _attn_kernel(lam_ref, dq_ref, k1_ref, k2_ref, dv_ref, mq_ref, mk_ref, mv_ref, sub_ref, o_ref,
                 *, out_scale, has_ctx):
    lam = lam_ref[0, 0]

    def body(keys):
        for h in range(DIFF_HEADS):
            sl = slice(h * LANE, (h + 1) * LANE)
            sv = slice(2 * LANE * h, 2 * LANE * (h + 1))
            qh = dq_ref[:, sl]
            o1 = _pv_normalised(_exp2_shifted(_dot_nt(qh, k1_ref[keys, sl])), dv_ref[keys, sv])
            o2 = _pv_normalised(_exp2_shifted(_dot_nt(qh, k2_ref[keys, sl])), dv_ref[keys, sv])
            o = o1 - lam * o2
            o = o * lax.rsqrt(jnp.mean(o * o, axis=-1, keepdims=True) + EPS) * sub_ref[...] * out_scale
            o_ref[:, sl] = o.astype(BF16)
        for h in range(MLA_HEADS):
            sq = slice(2 * LANE * h, 2 * LANE * (h + 1))
            o = _pv_normalised(_exp2_shifted(_dot_nt(mq_ref[:, sq], mk_ref[keys, sq])), mv_ref[keys, sq])
            o_ref[:, DIFF_W + h * LANE:DIFF_W + (h + 1) * LANE] = o.astype(BF16)

    if has_ctx:
        t = pl.program_id(1)
        pl.when(t == 0)(lambda: body(slice(SEQ, T)))
        pl.when(t > 0)(lambda: body(slice(0, T)))
    else:
        body(slice(0, T))


def _attention(lam, dq, k1, k2, dv, mq, mk, mv, sub, out_scale, latent_only):
    has_ctx = not latent_only
    nt = NTS if has_ctx else NTS - 1
    wide = 2 * LANE * MLA_HEADS
    tile = (lambda t: (t + NTS - 1) % NTS) if has_ctx else (lambda t: t)
    qspec = lambda n: pl.BlockSpec((None, TS, n), lambda b, t: (b, tile(t), 0))
    kspec = lambda n: pl.BlockSpec((None, T, n), lambda b, t: (b, 0, 0))
    return pl.pallas_call(
        functools.partial(_attn_kernel, out_scale=out_scale, has_ctx=has_ctx),
        grid=(B, nt),
        in_specs=[
            pl.BlockSpec(memory_space=pltpu.SMEM),
            qspec(DIFF_W), kspec(DIFF_W), kspec(DIFF_W), kspec(wide),
            qspec(wide), kspec(wide), kspec(wide),
            _const_spec((1, LANE)),
        ],
        out_specs=qspec(D),
        out_shape=jax.ShapeDtypeStruct((B, nt * TS, D), BF16),
        compiler_params=_cparams(("arbitrary", "arbitrary")),
        name="attention",
    )(lam, dq, k1, k2, dv, mq, mk, mv, sub)


def _rope_tables():
    n_tok = SEQ
    rows = jnp.repeat(jnp.arange(n_tok // GRID_W, dtype=jnp.int32), GRID_W)
    cols = jnp.tile(jnp.arange(GRID_W, dtype=jnp.int32), n_tok // GRID_W)
    n_freq = ROPE_DIM // 4
    inv = jnp.power(ROPE_BASE, -jnp.arange(n_freq, dtype=F32) / n_freq)
    ang_r = rows.astype(F32)[:, None] * inv
    ang_c = cols.astype(F32)[:, None] * inv
    ang = jnp.concatenate([ang_r, ang_r, ang_c, ang_c], axis=-1)
    sign = jnp.where((jnp.arange(ROPE_DIM) % (ROPE_DIM // 2)) < n_freq, -1.0, 1.0).astype(F32)
    cos = jnp.concatenate([jnp.cos(ang), jnp.ones((CTX, ROPE_DIM), F32)], axis=0)
    sin = jnp.concatenate([jnp.sin(ang) * sign, jnp.zeros((CTX, ROPE_DIM), F32)], axis=0)
    one, zero = jnp.ones_like(cos), jnp.zeros_like(sin)
    cat = lambda a, b: jnp.concatenate([a, b], axis=-1)
    return jnp.stack([cat(cos, cos), cat(sin, sin), cat(cos, one), cat(sin, zero)])


def _averaging_mats():
    idx = np.arange(2 * LANE)
    same64 = (idx[:, None] // DIFF_DIM) == (idx[None, :] // DIFF_DIM)
    rpair = np.where(same64, 1.0 / DIFF_DIM, 0.0)
    rfull = np.full((LANE, LANE), 1.0 / LANE)
    il = np.arange(LANE)
    rhalf = np.where((il[:, None] < MLA_ROPE) & (il[None, :] < MLA_ROPE), 1.0 / MLA_ROPE, 0.0)
    rq = np.zeros((2 * LANE, 2 * LANE))
    rq[:LANE, :LANE] = rfull
    rq[LANE:, LANE:] = rhalf
    return tuple(jnp.asarray(m, BF16) for m in (rpair, rq, rfull, rhalf))


def kernel(x, c, ctx, c_ctx, ada_w, ada_b, norm_mix, norm_ffn, w_out, ffn_w_in, ffn_w_out, ab_w_in, s5_lambda_re, s5_lambda_im, s5_log_step, s5_b_re, s5_b_im, s5_c_re, s5_c_im, s5_d, s5_glu_w, s5_glu_b, hgrn_lb_logits, hgrn_out_norm, cd_w_in, diff_lambda, diff_qk_norm, diff_subln, mla_q_a_norm, mla_kv_a_norm, mla_w_uq, mla_w_ukv, mla_nope_norm, mla_rope_norm):
    assert x.shape == (B, SEQ, D) and ctx.shape == (B, CTX, D)
    mod = _modulation(c, c_ctx, ada_w, ada_b)
    wo_all = w_out.astype(BF16)
    w1_all = ffn_w_in.astype(BF16)
    w2_all = ffn_w_out.astype(BF16)
    wab_all = ab_w_in.astype(BF16)
    wcd_all = jnp.pad(cd_w_in, ((0, 0), (0, 0), (0, CD_PAD - CD_IN))).astype(BF16)
    h_lat, h_ctx, ctx_block = x, ctx, 0
    rope = _rope_tables()
    rmats = _averaging_mats()

    lb_p = jax.nn.softmax(hgrn_lb_logits.astype(F32), axis=0)
    lower_bounds = jnp.cumsum(lb_p, axis=0) - lb_p[0:1]

    for l in range(DEPTH):
        last = l == DEPTH - 1
        g_mix = norm_mix[l].reshape(1, D)
        g_ffn = norm_ffn[l].reshape(1, D)
        if l % 2 == 0:
            e = l // 2
            proj = _even_in(h_lat, h_ctx, ctx_block, mod[l], g_mix, wab_all, e)
            s5w_in, s5w_out, a_r, a_i = _s5_params(s5_lambda_re[e], s5_lambda_im[e], s5_log_step[e],
                                                   s5_b_re[e], s5_b_im[e], s5_c_re[e], s5_c_im[e])
            y_f, y_b = _s5_scan(proj, s5w_in, s5w_out, a_r, a_i)
            lb = lower_bounds[e].reshape(1, HG_W)
            o_f, o_b = _hgrn_scan(proj, lb)
            hs = _post_even(y_f, y_b, o_f, o_b, proj, h_lat, h_ctx, ctx_block, mod[l], g_ffn,
                            s5_d[e].reshape(1, S5_W), s5_glu_w[e].astype(BF16), s5_glu_b[e].reshape(1, S5_W),
                            hgrn_out_norm[e].reshape(1, HG_DIM), wo_all, w1_all, w2_all, l)
        else:
            o = l // 2
            lam_init = 0.8 - 0.6 * math.exp(-0.3 * l)
            lv = diff_lambda[o].astype(F32)
            lam = (jnp.exp(jnp.sum(lv[0] * lv[1])) - jnp.exp(jnp.sum(lv[2] * lv[3])) + lam_init).reshape(1, 1)
            wuq = mla_w_uq[o].reshape(MLA_Q_RANK, MLA_HEADS, MLA_NOPE + MLA_ROPE)
            wuq = jnp.pad(wuq, ((0, 0), (0, 0), (0, 2 * LANE - MLA_NOPE - MLA_ROPE)))
            wuq = wuq.reshape(MLA_Q_RANK, 2 * LANE * MLA_HEADS).astype(BF16)
            pad_r = lambda v: jnp.pad(v, (0, LANE - MLA_ROPE))
            gv = jnp.zeros((SUBLANE, LANE), F32)
            gv = gv.at[0].set(jnp.tile(diff_qk_norm[o, 0], 2)).at[1].set(jnp.tile(diff_qk_norm[o, 1], 2))
            gv = gv.at[2].set(mla_nope_norm[o, 0]).at[3].set(pad_r(mla_rope_norm[o, 0]))
            gv = gv.at[4].set(mla_nope_norm[o, 1]).at[5].set(pad_r(mla_rope_norm[o, 1]))
            parts = _odd_in(hs, mod[l], g_mix, wcd_all, o, rope, gv, mla_q_a_norm[o].reshape(1, MLA_Q_RANK),
                            mla_kv_a_norm[o].reshape(1, MLA_KV_RANK), wuq, mla_w_ukv[o].astype(BF16), rmats)
            mx = _attention(lam, *parts, diff_subln[o].reshape(1, LANE), 1.0 - lam_init, latent_only=last)
            hs = _post_odd(mx, hs, mod[l], g_ffn, wo_all, w1_all, w2_all, l, latent_only=last)
        h_lat, h_ctx, ctx_block = hs, hs, SEQ // CTX
    return hs
```

```python
import functools
import math

import numpy as np
import jax
import jax.numpy as jnp
from jax import lax
from jax.experimental import pallas as pl
from jax.experimental.pallas import tpu as pltpu

F32 = jnp.float32
BF16 = jnp.bfloat16

D = 1024
B = 8
SEQ = 2048
CTX = 256
T = CTX + SEQ
DEPTH = 4
GRID_W = 64
FFN_H = ((8 * D + 3 * 256 - 1) // (3 * 256)) * 256
S5_W = D // 2
S5_GROUP = 16
S5_GROUPS = S5_W // S5_GROUP
S5_STATE = 64
HG_HEADS = 4
HG_DIM = D // 8
HG_W = HG_HEADS * HG_DIM
MAX_EXP_ARG = 60.0
DIFF_HEADS = 4
DIFF_DIM = D // 16
DIFF_W = DIFF_HEADS * 2 * DIFF_DIM
MLA_HEADS = 4
MLA_NOPE = D // 8
MLA_ROPE = D // 16
MLA_V = D // 8
MLA_Q_RANK = 3 * D // 8
MLA_KV_RANK = D // 4
ROPE_DIM = D // 16
ROPE_BASE = 10000.0
EPS = 1e-6
AB_IN = S5_W + 5 * HG_W
CD_IN = 3 * DIFF_W + MLA_Q_RANK + MLA_KV_RANK + MLA_ROPE
CD_PAD = CD_IN + 64

LANE = 128
SUBLANE = 8
TS = CTX
NTS = T // TS
TM = 512
NTL = SEQ // TM
MOD_ROWS = 16
CTX_ROW = B
S5_LC = 512
S5_NC = S5_GROUPS * S5_STATE // S5_LC
S5_UC = S5_LC // S5_STATE * S5_GROUP
S5_SB = 16
HG_C = TS
FFN_BOUNDS = (0, 768, 1536, 2304, FFN_H)
VMEM_LIMIT = 56 * 1024 * 1024
LOG2E = math.log2(math.e)


def _cparams(sem):
    return pltpu.CompilerParams(dimension_semantics=sem, vmem_limit_bytes=VMEM_LIMIT)


def _const_spec(shape):
    n = len(shape)
    return pl.BlockSpec(shape, lambda *_: (0,) * n, pipeline_mode=pl.Buffered(1))


def _layer_spec(shape, l):
    n = len(shape)
    return pl.BlockSpec((None,) + tuple(shape), lambda *_: (l,) + (0,) * n, pipeline_mode=pl.Buffered(1))


def _stream_specs(rows, n_lat, ctx_block):
    return [pl.BlockSpec((None, rows, D), lambda b, t: (b, jnp.minimum(t, n_lat - 1), 0)),
            pl.BlockSpec((None, CTX, D), lambda b, t: (b, ctx_block, 0))]


def _silu(x):
    return x * jax.nn.sigmoid(x)


def _norm_mod(x, g, shift, scale):
    ms = jnp.mean(x * x, axis=-1, keepdims=True)
    return x * lax.rsqrt(ms + EPS) * g * (1.0 + scale) + shift


def _dot(a, b):
    return jnp.dot(a, b, preferred_element_type=F32)


def _dot_nt(a, b):
    return lax.dot_general(a, b, (((1,), (1,)), ((), ())), preferred_element_type=F32)


def _mod_kernel(s_ref, w_ref, b_ref, o_ref):
    s = _silu(s_ref[...])
    o_ref[...] = _dot(s.astype(BF16), w_ref[...].astype(BF16)) + b_ref[...]


def _modulation(c, c_ctx, ada_w, ada_b):
    s = jnp.zeros((MOD_ROWS, D), F32).at[:B].set(c).at[CTX_ROW].set(c_ctx)
    nb = 1536
    out = pl.pallas_call(
        _mod_kernel,
        grid=(DEPTH, 6 * D // nb),
        in_specs=[
            pl.BlockSpec((MOD_ROWS, D), lambda l, n: (0, 0)),
            pl.BlockSpec((None, D, nb), lambda l, n: (l, 0, n)),
            pl.BlockSpec((None, 1, nb), lambda l, n: (l, 0, n)),
        ],
        out_specs=pl.BlockSpec((None, MOD_ROWS, nb), lambda l, n: (l, 0, n)),
        out_shape=jax.ShapeDtypeStruct((DEPTH, MOD_ROWS, 6 * D), F32),
        compiler_params=_cparams(("arbitrary", "arbitrary")),
        name="adaln_mod",
    )(s, ada_w, ada_b.reshape(DEPTH, 1, 6 * D))
    return out.reshape(DEPTH, MOD_ROWS, 6, D)


def _mod_spec(ctx_tile):
    return pl.BlockSpec((None, 6, D), lambda b, t: (jnp.where(t == ctx_tile, CTX_ROW, b), 0, 0))


def _for_tile_rows(has_ctx, body):
    if has_ctx:
        t = pl.program_id(1)
        pl.when(t < NTL)(lambda: body(TM))
        pl.when(t == NTL)(lambda: body(CTX))
    else:
        body(TM)


def _even_in_kernel(xl_ref, xc_ref, mod_ref, g_ref, w_ref, p_ref):
    def body(n):
        x = xl_ref[...] if n == TM else xc_ref[...]
        a = _norm_mod(x, g_ref[...], mod_ref[0:1, :], mod_ref[1:2, :])
        p_ref[0:n] = _dot(a.astype(BF16), w_ref[...])

    _for_tile_rows(True, body)


def _even_in(h_lat, h_ctx, ctx_block, mod_l, g, w_all, e):
    return pl.pallas_call(
        _even_in_kernel,
        grid=(B, NTL + 1),
        in_specs=_stream_specs(TM, NTL, ctx_block) + [
            _mod_spec(NTL),
            _const_spec((1, D)),
            _layer_spec((D, AB_IN), e),
        ],
        out_specs=pl.BlockSpec((None, TM, AB_IN), lambda b, t: (b, t, 0)),
        out_shape=jax.ShapeDtypeStruct((B, T, AB_IN), F32),
        compiler_params=_cparams(("arbitrary", "arbitrary")),
        name="even_in_proj",
    )(h_lat, h_ctx, mod_l, g, w_all)


def _s5_kernel(uf_ref, wf_ref, cf_ref, arf_ref, aif_ref, ur_ref, wr_ref, cr_ref, arr_ref, air_ref,
               yf_ref, yr_ref, uxf_ref, yxf_ref, buf_ref, hf_ref, uxr_ref, yxr_ref, bur_ref, hr_ref):
    @pl.when(pl.program_id(1) == 0)
    def _():
        hf_ref[...] = jnp.zeros_like(hf_ref)
        hr_ref[...] = jnp.zeros_like(hr_ref)

    n_sb = TS // S5_SB
    dirs = (
        (False, uf_ref, wf_ref, cf_ref, arf_ref[...], aif_ref[...], yf_ref, uxf_ref, yxf_ref, buf_ref, hf_ref),
        (True, ur_ref, wr_ref, cr_ref, arr_ref[...], air_ref[...], yr_ref, uxr_ref, yxr_ref, bur_ref, hr_ref),
    )

    def rows_of(sb):
        return slice(sb * S5_SB * B, (sb + 1) * S5_SB * B)

    def sub_block(rev, n):
        return n_sb - 1 - n if rev else n

    def project(d, n):
        rev, _, w_ref, _, _, _, _, ux_ref, _, bu_ref, _ = d
        rows = rows_of(sub_block(rev, n))
        bu_ref[rows, :] = _dot(ux_ref[rows, :].astype(BF16), w_ref[...])

    for _, u_ref, _, _, _, _, _, ux_ref, _, _, _ in dirs:
        for b in range(B):
            ux_ref[pl.ds(b, TS, stride=B), :] = u_ref[b]
    state = [(d[10][0], d[10][1]) for d in dirs]
    for d in dirs:
        project(d, 0)
    for n in range(n_sb):
        if n + 1 < n_sb:
            for d in dirs:
                project(d, n + 1)
        for k in range(S5_SB):
            for i, (rev, _, _, _, ar, ai, _, _, _, bu_ref, _) in enumerate(dirs):
                t = S5_SB - 1 - k if rev else k
                r0 = rows_of(sub_block(rev, n)).start + t * B
                r = slice(r0, r0 + B)
                hr, hi = state[i]
                nr = ar * hr - ai * hi + bu_ref[r, 0:S5_LC]
                ni = ar * hi + ai * hr + bu_ref[r, S5_LC:2 * S5_LC]
                bu_ref[r, 0:S5_LC] = nr
                bu_ref[r, S5_LC:2 * S5_LC] = ni
                state[i] = (nr, ni)
        for rev, _, _, c_ref, _, _, _, _, yx_ref, bu_ref, _ in dirs:
            rows = rows_of(sub_block(rev, n))
            yx_ref[rows, :] = _dot(bu_ref[rows, :].astype(BF16), c_ref[...])
    for i, (_, _, _, _, _, _, y_ref, _, yx_ref, _, h_ref) in enumerate(dirs):
        h_ref[0], h_ref[1] = state[i]
        for b in range(B):
            y_ref[b] = yx_ref[pl.ds(b, TS, stride=B), :]


def _seq_tile(d, i):
    return jnp.where(i == 0, NTS - 1, jnp.where(d == 0, i - 1, NTS - 1 - i))


def _s5_scan(proj, w_in, w_out, a_r, a_i):
    def operands(d):
        par = lambda *shape: pl.BlockSpec((None, None) + shape, lambda c, i: (d, c, 0, 0))
        return [pl.BlockSpec((B, TS, S5_UC), lambda c, i: (0, _seq_tile(d, i), c)),
                par(S5_UC, 2 * S5_LC), par(2 * S5_LC, S5_UC), par(B, S5_LC), par(B, S5_LC)]

    scratch = [pltpu.VMEM((TS * B, S5_UC), F32), pltpu.VMEM((TS * B, S5_UC), F32),
               pltpu.VMEM((TS * B, 2 * S5_LC), F32), pltpu.VMEM((2, B, S5_LC), F32)]
    out = jax.ShapeDtypeStruct((B, T, S5_W), F32)
    return pl.pallas_call(
        _s5_kernel,
        grid=(S5_NC, NTS),
        in_specs=operands(0) + operands(1),
        out_specs=[pl.BlockSpec((B, TS, S5_UC), lambda c, i, d=d: (0, _seq_tile(d, i), c)) for d in (0, 1)],
        out_shape=[out, out],
        scratch_shapes=scratch + scratch,
        compiler_params=_cparams(("arbitrary", "arbitrary")),
        name="s5_scan",
    )(proj, w_in, w_out, a_r, a_i, proj, w_in, w_out, a_r, a_i)


def _s5_params(lam_re, lam_im, log_step, b_re, b_im, c_re, c_im):
    lr = jnp.minimum(lam_re.astype(F32), -1e-4)
    li = lam_im.astype(F32)
    step = jnp.exp(log_step.astype(F32))[..., None]
    mag = jnp.exp(lr * step)
    a_r = mag * jnp.cos(li * step)
    a_i = mag * jnp.sin(li * step)
    den = lr * lr + li * li
    coef_r = ((a_r - 1) * lr + a_i * li) / den
    coef_i = (a_i * lr - (a_r - 1) * li) / den
    br = b_re.astype(F32)
    bi = b_im.astype(F32)
    bb_r = coef_r[..., None] * br - coef_i[..., None] * bi
    bb_i = coef_r[..., None] * bi + coef_i[..., None] * br
    gpc = S5_LC // S5_STATE
    eye = jnp.eye(gpc, dtype=F32)

    def in_blocks(bb):
        bb = bb.reshape(2, S5_NC, gpc, S5_STATE, S5_GROUP)
        return jnp.einsum('dngpc,gh->dngchp', bb, eye).reshape(2, S5_NC, S5_UC, S5_LC)

    def out_blocks(cc):
        cc = cc.reshape(2, S5_NC, gpc, S5_GROUP, S5_STATE)
        return jnp.einsum('dngcp,gh->dngphc', cc, eye).reshape(2, S5_NC, S5_LC, S5_UC)

    w_in = jnp.concatenate([in_blocks(bb_r), in_blocks(bb_i)], axis=-1).astype(BF16)
    w_out = jnp.concatenate([out_blocks(c_re.astype(F32)), -out_blocks(c_im.astype(F32))], axis=-2).astype(BF16)

    def lanes(a):
        return jnp.broadcast_to(a.reshape(2, S5_NC, 1, S5_LC), (2, S5_NC, B, S5_LC))

    return w_in, w_out, lanes(a_r), lanes(a_i)


HG_HALF = HG_C // 2
HG_LEVELS = tuple(2 ** e for e in range(int(math.log2(HG_HALF))))


def _hgrn_maps(rev):
    pos = np.arange(HG_C)
    ph = np.arange(HG_HALF)
    if rev:
        pos = HG_C - 1 - pos
        ph = HG_HALF - 1 - ph
    tri = (pos[None, :] <= pos[:, None]).astype(np.float32)
    pt, ps = ph[:, None], ph[None, :]
    lv = np.full((HG_HALF, HG_HALF), -1, np.int32)
    lv[pt == ps] = len(HG_LEVELS)
    for e, m in enumerate(HG_LEVELS):
        x, y = pt // m, ps // m
        lv[(x == y + 1) & (x % 2 == 1)] = e
    return jnp.asarray(tri, BF16), jnp.asarray(lv)


def _hgrn_chunk(q_ref, f_ref, v_ref, lb, tri_ref, lv_ref, o_ref, s_ref, rev):
    c = HG_C
    fl = f_ref[...]
    qb = _silu(q_ref[...]).astype(BF16)
    v = v_ref[...]
    x2 = fl * LOG2E
    a = jnp.exp2(-jnp.abs(x2))
    s1 = 1.0 + a
    r = 1.0 / s1
    kb = ((1.0 - lb) * jnp.where(x2 >= 0.0, a * r, r)).astype(BF16)
    t2 = jnp.exp2(jnp.minimum(-x2, MAX_EXP_ARG * LOG2E))
    lf = jnp.minimum(x2, 0.0) - jnp.log2(s1) + jnp.log2(1.0 + lb * t2)

    row = lax.broadcasted_iota(jnp.int32, (c, 1), 0)
    pos = (c - 1 - row) if rev else row
    first, second = (slice(HG_HALF, c), slice(0, HG_HALF)) if rev else (slice(0, HG_HALF), slice(HG_HALF, c))
    halves = (first, second)

    def from_earlier(x, j):
        return pltpu.roll(x, (c - j) if rev else j, 0)

    def from_later(x, j):
        return pltpu.roll(x, j if rev else (c - j), 0)

    def head(x, h, rows):
        return x[rows, h * HG_DIM:(h + 1) * HG_DIM]

    hi = lf.astype(BF16)
    r1 = lf - hi.astype(F32)
    mid = r1.astype(BF16)
    lo = (r1 - mid.astype(F32)).astype(BF16)
    tri = tri_ref[...]
    cum = _dot(tri, hi) + _dot(tri, mid) + _dot(tri, lo)

    lv = lv_ref[...]
    diag = len(HG_LEVELS)
    scores = [[jnp.where(lv == diag, _dot_nt(head(qb, h, r), head(kb, h, r)), 0.0) for r in halves]
              for h in range(HG_HEADS)]

    g_end = cum
    for e, m in enumerate(HG_LEVELS):
        qm = qb * jnp.exp2(cum - from_earlier(g_end, m)).astype(BF16)
        km = kb * jnp.exp2(g_end - cum).astype(BF16)
        for h in range(HG_HEADS):
            for j, r in enumerate(halves):
                scores[h][j] = jnp.where(lv == e, _dot_nt(head(qm, h, r), head(km, h, r)), scores[h][j])
        g_end = jnp.where((pos & m) != 0, g_end, from_later(g_end, m))
    qm = qb * jnp.exp2(cum - from_earlier(g_end, HG_HALF)).astype(BF16)
    km = kb * jnp.exp2(g_end - cum).astype(BF16)
    cross = [_dot_nt(head(qm, h, second), head(km, h, first)) for h in range(HG_HEADS)]
    g_end = jnp.where((pos & HG_HALF) != 0, g_end, from_later(g_end, HG_HALF))

    q_in = qb * jnp.exp2(cum).astype(BF16)
    k_out = kb * jnp.exp2(g_end - cum).astype(BF16)
    decay = jnp.exp2(g_end[0:1, :])
    vb = v.astype(BF16)
    for h in range(HG_HEADS):
        sl = slice(h * HG_DIM, (h + 1) * HG_DIM)
        st = s_ref[h]
        inter = _dot_nt(q_in[:, sl], st.astype(BF16))
        v_first, v_second = head(vb, h, first), head(vb, h, second)
        o_ref[first, sl] = inter[first] + _dot(scores[h][0].astype(BF16), v_first)
        o_ref[second, sl] = (inter[second] + _dot(cross[h].astype(BF16), v_first)
                             + _dot(scores[h][1].astype(BF16), v_second))
        s_ref[h] = decay[:, sl] * st + _dot(v[:, sl].T.astype(BF16), k_out[:, sl])


def _hgrn_kernel(qf_ref, ff_ref, vf_ref, qr_ref, fr_ref, vr_ref, lb_ref, trif_ref, lvf_ref, trir_ref, lvr_ref,
                 of_ref, or_ref, sf_ref, sr_ref):
    @pl.when(pl.program_id(1) == 0)
    def _():
        sf_ref[...] = jnp.zeros_like(sf_ref)
        sr_ref[...] = jnp.zeros_like(sr_ref)

    lb = lb_ref[...]
    _hgrn_chunk(qf_ref, ff_ref, vf_ref, lb, trif_ref, lvf_ref, of_ref, sf_ref, rev=False)
    _hgrn_chunk(qr_ref, fr_ref, vr_ref, lb, trir_ref, lvr_ref, or_ref, sr_ref, rev=True)


def _hgrn_scan(hg, lb):
    maps = [_hgrn_maps(rev) for rev in (False, True)]
    blk = lambda d, col: pl.BlockSpec((None, HG_C, HG_W), lambda b, i: (b, _seq_tile(d, i), col))
    state = pltpu.VMEM((HG_HEADS, HG_DIM, HG_DIM), F32)
    out = jax.ShapeDtypeStruct((B, T, HG_W), F32)
    return pl.pallas_call(
        _hgrn_kernel,
        grid=(B, NTS),
        in_specs=[blk(0, 1), blk(0, 2), blk(0, 4), blk(1, 1), blk(1, 3), blk(1, 4),
                  _const_spec((1, HG_W)),
                  _const_spec((HG_C, HG_C)), _const_spec((HG_HALF, HG_HALF)),
                  _const_spec((HG_C, HG_C)), _const_spec((HG_HALF, HG_HALF))],
        out_specs=[blk(0, 0), blk(1, 0)],
        out_shape=[out, out],
        scratch_shapes=[state, state],
        compiler_params=_cparams(("arbitrary", "arbitrary")),
        name="hgrn_scan",
    )(hg, hg, hg, hg, hg, hg, lb, *maps[0], *maps[1])


def _ffn_tail(mix_out, h, mod_ref, g_ref, w_in_ref, w_out_ref):
    h1 = h + mod_ref[2:3, :] * mix_out
    a = _norm_mod(h1, g_ref[...], mod_ref[3:4, :], mod_ref[4:5, :]).astype(BF16)

    def gate_up(c):
        lo, hi = FFN_BOUNDS[c], FFN_BOUNDS[c + 1]
        return _dot(a, w_in_ref[:, lo:hi]), _dot(a, w_in_ref[:, FFN_H + lo:FFN_H + hi])

    ffn = None
    ahead = gate_up(0)
    for c in range(len(FFN_BOUNDS) - 1):
        gate, up = ahead
        if c + 2 < len(FFN_BOUNDS):
            ahead = gate_up(c + 1)
        part = _dot((_silu(gate) * up).astype(BF16), w_out_ref[FFN_BOUNDS[c]:FFN_BOUNDS[c + 1], :])
        ffn = part if ffn is None else ffn + part
    return h1 + mod_ref[5:6, :] * ffn


def _gelu_tanh(x):
    return 0.5 * x * (1.0 + jnp.tanh(math.sqrt(2.0 / math.pi) * (x + 0.044715 * (x * x * x))))


def _post_even_kernel(yf_ref, yb_ref, u_ref, of_ref, ob_ref, gate_ref, hl_ref, hc_ref, mod_ref, g_ref,
                      dsk_ref, gw_ref, gb_ref, on_ref, wo_ref, w_in_ref, w_out_ref, o_ref):
    y = yf_ref[...] + yb_ref[...] + u_ref[...] * dsk_ref[...]
    z = _gelu_tanh(y)
    s5 = z * jax.nn.sigmoid(_dot(z.astype(BF16), gw_ref[...]) + gb_ref[...])
    mix = _dot(s5.astype(BF16), wo_ref[0:S5_W, :])
    o = of_ref[...] + ob_ref[...]
    gate = _silu(gate_ref[...])
    for h in range(HG_HEADS):
        sl = slice(h * HG_DIM, (h + 1) * HG_DIM)
        oh = o[:, sl]
        ms = jnp.mean(oh * oh, axis=-1, keepdims=True)
        hn = oh * lax.rsqrt(ms + EPS) * on_ref[...] * gate[:, sl]
        mix = mix + _dot(hn.astype(BF16), wo_ref[S5_W + h * HG_DIM:S5_W + (h + 1) * HG_DIM, :])
    h = jnp.where(pl.program_id(1) == NTS - 1, hc_ref[...], hl_ref[...])
    o_ref[...] = _ffn_tail(mix, h, mod_ref, g_ref, w_in_ref, w_out_ref)


def _post_even(y_f, y_b, o_f, o_b, proj, h_lat, h_ctx, ctx_block, mod_l, g, dsk, gw, gb, on, wo, w_in, w_out, l):
    tok = lambda b, t: (b, t, 0)
    return pl.pallas_call(
        _post_even_kernel,
        grid=(B, NTS),
        in_specs=[
            pl.BlockSpec((None, TS, S5_W), tok),
            pl.BlockSpec((None, TS, S5_W), tok),
            pl.BlockSpec((None, TS, S5_W), tok),
            pl.BlockSpec((None, TS, HG_W), tok),
            pl.BlockSpec((None, TS, HG_W), tok),
            pl.BlockSpec((None, TS, HG_W), lambda b, t: (b, t, 5)),
        ] + _stream_specs(TS, NTS - 1, ctx_block) + [
            _mod_spec(NTS - 1),
            _const_spec((1, D)),
            _const_spec((1, S5_W)),
            _const_spec((S5_W, S5_W)),
            _const_spec((1, S5_W)),
            _const_spec((1, HG_DIM)),
            _layer_spec((D, D), l),
            _layer_spec((D, 2 * FFN_H), l),
            _layer_spec((FFN_H, D), l),
        ],
        out_specs=pl.BlockSpec((None, TS, D), tok),
        out_shape=jax.ShapeDtypeStruct((B, T, D), F32),
        compiler_params=_cparams(("arbitrary", "arbitrary")),
        name="post_even",
    )(y_f, y_b, proj, o_f, o_b, proj, h_lat, h_ctx, mod_l, g, dsk, gw, gb, on, wo, w_in, w_out)


def _post_odd_kernel(mx_ref, h_ref, mod_ref, g_ref, wo_ref, w_in_ref, w_out_ref, o_ref):
    o_ref[...] = _ffn_tail(_dot(mx_ref[...], wo_ref[...]), h_ref[...], mod_ref, g_ref, w_in_ref, w_out_ref)


def _post_odd(mx, hs, mod_l, g, wo, w_in, w_out, l, latent_only):
    nt = NTS - 1 if latent_only else NTS
    tok = lambda b, t: (b, t, 0)
    return pl.pallas_call(
        _post_odd_kernel,
        grid=(B, nt),
        in_specs=[
            pl.BlockSpec((None, TS, D), tok),
            pl.BlockSpec((None, TS, D), tok),
            _mod_spec(NTS - 1),
            _const_spec((1, D)),
            _layer_spec((D, D), l),
            _layer_spec((D, 2 * FFN_H), l),
            _layer_spec((FFN_H, D), l),
        ],
        out_specs=pl.BlockSpec((None, TS, D), tok),
        out_shape=jax.ShapeDtypeStruct((B, nt * TS, D), F32),
        compiler_params=_cparams(("arbitrary", "arbitrary")),
        name="post_odd",
    )(mx, hs, mod_l, g, wo, w_in, w_out)


def _group_rms(xs, r):
    ms = _dot((xs * xs).astype(BF16), r)
    return xs * lax.rsqrt(ms + EPS)


def _rope(xs, cos, sin_signed, first_half):
    rot = jnp.where(first_half, pltpu.roll(xs, LANE - ROPE_DIM // 4, 1), pltpu.roll(xs, ROPE_DIM // 4, 1))
    return xs * cos + rot * sin_signed


def _odd_in_kernel(x_ref, mod_ref, g_ref, w_ref, rope_ref, gv_ref, qan_ref, kvan_ref, wuq_ref, wukv_ref,
                   rpair_ref, rq_ref, rfull_ref, rhalf_ref,
                   dq_ref, k1_ref, k2_ref, dv_ref, mq_ref, mk_ref, mv_ref, *, mla_scale):
    lane = lax.broadcasted_iota(jnp.int32, (1, LANE), 1)
    first_half = (lane % (ROPE_DIM // 2)) < (ROPE_DIM // 4)
    low = lane < DIFF_DIM
    rpair, rq, rfull, rhalf = rpair_ref[...], rq_ref[...], rfull_ref[...], rhalf_ref[...]
    g_dq, g_dk = gv_ref[0:1, :], gv_ref[1:2, :]
    g_qn, g_qr, g_kn, g_kr = gv_ref[2:3, :], gv_ref[3:4, :], gv_ref[4:5, :], gv_ref[5:6, :]
    diff_scale = DIFF_DIM ** -0.5 * LOG2E
    o_cq = 3 * DIFF_W
    o_ckv = o_cq + MLA_Q_RANK
    o_kr = o_ckv + MLA_KV_RANK

    def body(n):
        rows = slice(0, n)
        a = _norm_mod(x_ref[rows], g_ref[...], mod_ref[0:1, :], mod_ref[1:2, :]).astype(BF16)
        cos_d, sin_d, cos_m, sin_m = rope_ref[0, rows], rope_ref[1, rows], rope_ref[2, rows], rope_ref[3, rows]
        ones = jnp.ones((n, LANE), BF16)

        def proj(c0, c1):
            return _dot(a, w_ref[:, c0:c1])

        def diff_pair(pq, pk, h2):
            qn = _group_rms(pq, rpair)
            kn = _group_rms(pk, rpair)
            for j in range(2):
                sl = slice((h2 + j) * LANE, (h2 + j + 1) * LANE)
                half = slice(j * LANE, (j + 1) * LANE)
                qh = _rope(qn[:, half] * g_dq, cos_d, sin_d, first_half)
                dq_ref[rows, sl] = (qh * diff_scale).astype(BF16)
                kh = _rope(kn[:, half] * g_dk, cos_d, sin_d, first_half)
                k1_ref[rows, sl] = jnp.where(low, kh, 0.0).astype(BF16)
                k2_ref[rows, sl] = jnp.where(low, 0.0, kh).astype(BF16)

        def diff_values(pv):
            for h in range(DIFF_HEADS):
                dv_ref[rows, 2 * h * LANE:(2 * h + 1) * LANE] = pv[:, h * LANE:(h + 1) * LANE].astype(BF16)
                dv_ref[rows, (2 * h + 1) * LANE:(2 * h + 2) * LANE] = ones

        def latent_heads(pm):
            cq = pm[:, 0:MLA_Q_RANK]
            cqn = cq * lax.rsqrt(jnp.mean(cq * cq, axis=-1, keepdims=True) + EPS) * qan_ref[...]
            q = _dot(cqn.astype(BF16), wuq_ref[...])
            ckv = pm[:, MLA_Q_RANK:MLA_Q_RANK + MLA_KV_RANK]
            ckvn = ckv * lax.rsqrt(jnp.mean(ckv * ckv, axis=-1, keepdims=True) + EPS) * kvan_ref[...]
            kv = _dot(ckvn.astype(BF16), wukv_ref[...])
            k_rope = _rope(_group_rms(pm[:, MLA_Q_RANK + MLA_KV_RANK:], rhalf) * g_kr,
                           cos_m, sin_m, first_half).astype(BF16)
            for h in range(MLA_HEADS):
                o = 2 * LANE * h
                qn = _group_rms(q[:, o:o + 2 * LANE], rq)
                q_nope = qn[:, :LANE] * g_qn
                q_rope = _rope(qn[:, LANE:] * g_qr, cos_m, sin_m, first_half)
                mq_ref[rows, o:o + LANE] = (q_nope * mla_scale).astype(BF16)
                mq_ref[rows, o + LANE:o + 2 * LANE] = (q_rope * mla_scale).astype(BF16)
                mk_ref[rows, o:o + LANE] = (_group_rms(kv[:, o:o + LANE], rfull) * g_kn).astype(BF16)
                mk_ref[rows, o + LANE:o + 2 * LANE] = k_rope
                mv_ref[rows, o:o + LANE] = kv[:, o + LANE:o + 2 * LANE].astype(BF16)
                mv_ref[rows, o + LANE:o + 2 * LANE] = ones

        pair = 2 * LANE
        pm = proj(o_cq, CD_PAD)
        pq0, pk0 = proj(0, pair), proj(DIFF_W, DIFF_W + pair)
        latent_heads(pm)
        pq1, pk1 = proj(pair, 2 * pair), proj(DIFF_W + pair, DIFF_W + 2 * pair)
        diff_pair(pq0, pk0, 0)
        pv = proj(2 * DIFF_W, 3 * DIFF_W)
        diff_pair(pq1, pk1, 2)
        diff_values(pv)

    _for_tile_rows(True, body)


def _odd_in(hs, mod_l, g, w_all, o, rope, gv, qan, kvan, wuq, wukv, rmats):
    tok = lambda b, t: (b, t, 0)
    sds = lambda n: jax.ShapeDtypeStruct((B, T, n), BF16)
    wide = 2 * LANE * MLA_HEADS
    widths = (DIFF_W, DIFF_W, DIFF_W, wide, wide, wide, wide)
    return pl.pallas_call(
        functools.partial(_odd_in_kernel, mla_scale=(MLA_NOPE + MLA_ROPE) ** -0.5 * LOG2E),
        grid=(B, NTL + 1),
        in_specs=[
            pl.BlockSpec((None, TM, D), tok),
            _mod_spec(NTL),
            _const_spec((1, D)),
            _layer_spec((D, CD_PAD), o),
            pl.BlockSpec((4, TM, LANE), lambda b, t: (0, t, 0)),
            _const_spec((SUBLANE, LANE)),
            _const_spec((1, MLA_Q_RANK)),
            _const_spec((1, MLA_KV_RANK)),
            _const_spec((MLA_Q_RANK, 2 * LANE * MLA_HEADS)),
            _const_spec((MLA_KV_RANK, 2 * LANE * MLA_HEADS)),
            _const_spec((2 * LANE, 2 * LANE)),
            _const_spec((2 * LANE, 2 * LANE)),
            _const_spec((LANE, LANE)),
            _const_spec((LANE, LANE)),
        ],
        out_specs=[pl.BlockSpec((None, TM, n), tok) for n in widths],
        out_shape=[sds(n) for n in widths],
        compiler_params=_cparams(("arbitrary", "arbitrary")),
        name="odd_in_proj",
    )(hs, mod_l, g, w_all, rope, gv, qan, kvan, wuq, wukv, *rmats)


def _exp2_shifted(s):
    return jnp.exp2(s - jnp.max(s, axis=-1, keepdims=True)).astype(BF16)


def _pv_normalised(e, v_aug):
    o = _dot(e, v_aug)
    return o[:, :LANE] / o[:, LANE:]


def _attn_kernel(lam_ref, dq_ref, k1_ref, k2_ref, dv_ref, mq_ref, mk_ref, mv_ref, sub_ref, o_ref,
                 *, out_scale, has_ctx):
    lam = lam_ref[0, 0]

    def body(keys):
        for h in range(DIFF_HEADS):
            sl = slice(h * LANE, (h + 1) * LANE)
            sv = slice(2 * LANE * h, 2 * LANE * (h + 1))
            qh = dq_ref[:, sl]
            o1 = _pv_normalised(_exp2_shifted(_dot_nt(qh, k1_ref[keys, sl])), dv_ref[keys, sv])
            o2 = _pv_normalised(_exp2_shifted(_dot_nt(qh, k2_ref[keys, sl])), dv_ref[keys, sv])
            o = o1 - lam * o2
            o = o * lax.rsqrt(jnp.mean(o * o, axis=-1, keepdims=True) + EPS) * sub_ref[...] * out_scale
            o_ref[:, sl] = o.astype(BF16)
        for h in range(MLA_HEADS):
            sq = slice(2 * LANE * h, 2 * LANE * (h + 1))
            o = _pv_normalised(_exp2_shifted(_dot_nt(mq_ref[:, sq], mk_ref[keys, sq])), mv_ref[keys, sq])
            o_ref[:, DIFF_W + h * LANE:DIFF_W + (h + 1) * LANE] = o.astype(BF16)

    if has_ctx:
        t = pl.program_id(1)
        pl.when(t == 0)(lambda: body(slice(SEQ, T)))
        pl.when(t > 0)(lambda: body(slice(0, T)))
    else:
        body(slice(0, T))


def _attention(lam, dq, k1, k2, dv, mq, mk, mv, sub, out_scale, latent_only):
    has_ctx = not latent_only
    nt = NTS if has_ctx else NTS - 1
    wide = 2 * LANE * MLA_HEADS
    tile = (lambda t: (t + NTS - 1) % NTS) if has_ctx else (lambda t: t)
    qspec = lambda n: pl.BlockSpec((None, TS, n), lambda b, t: (b, tile(t), 0))
    kspec = lambda n: pl.BlockSpec((None, T, n), lambda b, t: (b, 0, 0))
    return pl.pallas_call(
        functools.partial(_attn_kernel, out_scale=out_scale, has_ctx=has_ctx),
        grid=(B, nt),
        in_specs=[
            pl.BlockSpec(memory_space=pltpu.SMEM),
            qspec(DIFF_W), kspec(DIFF_W), kspec(DIFF_W), kspec(wide),
            qspec(wide), kspec(wide), kspec(wide),
            _const_spec((1, LANE)),
        ],
        out_specs=qspec(D),
        out_shape=jax.ShapeDtypeStruct((B, nt * TS, D), BF16),
        compiler_params=_cparams(("arbitrary", "arbitrary")),
        name="attention",
    )(lam, dq, k1, k2, dv, mq, mk, mv, sub)


def _rope_tables():
    n_tok = SEQ
    rows = jnp.repeat(jnp.arange(n_tok // GRID_W, dtype=jnp.int32), GRID_W)
    cols = jnp.tile(jnp.arange(GRID_W, dtype=jnp.int32), n_tok // GRID_W)
    n_freq = ROPE_DIM // 4
    inv = jnp.power(ROPE_BASE, -jnp.arange(n_freq, dtype=F32) / n_freq)
    ang_r = rows.astype(F32)[:, None] * inv
    ang_c = cols.astype(F32)[:, None] * inv
    ang = jnp.concatenate([ang_r, ang_r, ang_c, ang_c], axis=-1)
    sign = jnp.where((jnp.arange(ROPE_DIM) % (ROPE_DIM // 2)) < n_freq, -1.0, 1.0).astype(F32)
    cos = jnp.concatenate([jnp.cos(ang), jnp.ones((CTX, ROPE_DIM), F32)], axis=0)
    sin = jnp.concatenate([jnp.sin(ang) * sign, jnp.zeros((CTX, ROPE_DIM), F32)], axis=0)
    one, zero = jnp.ones_like(cos), jnp.zeros_like(sin)
    cat = lambda a, b: jnp.concatenate([a, b], axis=-1)
    return jnp.stack([cat(cos, cos), cat(sin, sin), cat(cos, one), cat(sin, zero)])


def _averaging_mats():
    idx = np.arange(2 * LANE)
    same64 = (idx[:, None] // DIFF_DIM) == (idx[None, :] // DIFF_DIM)
    rpair = np.where(same64, 1.0 / DIFF_DIM, 0.0)
    rfull = np.full((LANE, LANE), 1.0 / LANE)
    il = np.arange(LANE)
    rhalf = np.where((il[:, None] < MLA_ROPE) & (il[None, :] < MLA_ROPE), 1.0 / MLA_ROPE, 0.0)
    rq = np.zeros((2 * LANE, 2 * LANE))
    rq[:LANE, :LANE] = rfull
    rq[LANE:, LANE:] = rhalf
    return tuple(jnp.asarray(m, BF16) for m in (rpair, rq, rfull, rhalf))


def kernel(x, c, ctx, c_ctx, ada_w, ada_b, norm_mix, norm_ffn, w_out, ffn_w_in, ffn_w_out, ab_w_in, s5_lambda_re, s5_lambda_im, s5_log_step, s5_b_re, s5_b_im, s5_c_re, s5_c_im, s5_d, s5_glu_w, s5_glu_b, hgrn_lb_logits, hgrn_out_norm, cd_w_in, diff_lambda, diff_qk_norm, diff_subln, mla_q_a_norm, mla_kv_a_norm, mla_w_uq, mla_w_ukv, mla_nope_norm, mla_rope_norm):
    assert x.shape == (B, SEQ, D) and ctx.shape == (B, CTX, D)
    mod = _modulation(c, c_ctx, ada_w, ada_b)
    wo_all = w_out.astype(BF16)
    w1_all = ffn_w_in.astype(BF16)
    w2_all = ffn_w_out.astype(BF16)
    wab_all = ab_w_in.astype(BF16)
    wcd_all = jnp.pad(cd_w_in, ((0, 0), (0, 0), (0, CD_PAD - CD_IN))).astype(BF16)
    h_lat, h_ctx, ctx_block = x, ctx, 0
    rope = _rope_tables()
    rmats = _averaging_mats()

    lb_p = jax.nn.softmax(hgrn_lb_logits.astype(F32), axis=0)
    lower_bounds = jnp.cumsum(lb_p, axis=0) - lb_p[0:1]

    for l in range(DEPTH):
        last = l == DEPTH - 1
        g_mix = norm_mix[l].reshape(1, D)
        g_ffn = norm_ffn[l].reshape(1, D)
        if l % 2 == 0:
            e = l // 2
            proj = _even_in(h_lat, h_ctx, ctx_block, mod[l], g_mix, wab_all, e)
            s5w_in, s5w_out, a_r, a_i = _s5_params(s5_lambda_re[e], s5_lambda_im[e], s5_log_step[e],
                                                   s5_b_re[e], s5_b_im[e], s5_c_re[e], s5_c_im[e])
            y_f, y_b = _s5_scan(proj, s5w_in, s5w_out, a_r, a_i)
            lb = lower_bounds[e].reshape(1, HG_W)
            o_f, o_b = _hgrn_scan(proj, lb)
            hs = _post_even(y_f, y_b, o_f, o_b, proj, h_lat, h_ctx, ctx_block, mod[l], g_ffn,
                            s5_d[e].reshape(1, S5_W), s5_glu_w[e].astype(BF16), s5_glu_b[e].reshape(1, S5_W),
                            hgrn_out_norm[e].reshape(1, HG_DIM), wo_all, w1_all, w2_all, l)
        else:
            o = l // 2
            lam_init = 0.8 - 0.6 * math.exp(-0.3 * l)
            lv = diff_lambda[o].astype(F32)
            lam = (jnp.exp(jnp.sum(lv[0] * lv[1])) - jnp.exp(jnp.sum(lv[2] * lv[3])) + lam_init).reshape(1, 1)
            wuq = mla_w_uq[o].reshape(MLA_Q_RANK, MLA_HEADS, MLA_NOPE + MLA_ROPE)
            wuq = jnp.pad(wuq, ((0, 0), (0, 0), (0, 2 * LANE - MLA_NOPE - MLA_ROPE)))
            wuq = wuq.reshape(MLA_Q_RANK, 2 * LANE * MLA_HEADS).astype(BF16)
            pad_r = lambda v: jnp.pad(v, (0, LANE - MLA_ROPE))
            gv = jnp.zeros((SUBLANE, LANE), F32)
            gv = gv.at[0].set(jnp.tile(diff_qk_norm[o, 0], 2)).at[1].set(jnp.tile(diff_qk_norm[o, 1], 2))
            gv = gv.at[2].set(mla_nope_norm[o, 0]).at[3].set(pad_r(mla_rope_norm[o, 0]))
            gv = gv.at[4].set(mla_nope_norm[o, 1]).at[5].set(pad_r(mla_rope_norm[o, 1]))
            parts = _odd_in(hs, mod[l], g_mix, wcd_all, o, rope, gv, mla_q_a_norm[o].reshape(1, MLA_Q_RANK),
                            mla_kv_a_norm[o].reshape(1, MLA_KV_RANK), wuq, mla_w_ukv[o].astype(BF16), rmats)
            mx = _attention(lam, *parts, diff_subln[o].reshape(1, LANE), 1.0 - lam_init, latent_only=last)
            hs = _post_odd(mx, hs, mod[l], g_ffn, wo_all, w1_all, w2_all, l, latent_only=last)
        h_lat, h_ctx, ctx_block = hs, hs, SEQ // CTX
    return hs
```

```python
import functools
import math

import numpy as np
import jax
import jax.numpy as jnp
from jax import lax
from jax.experimental import pallas as pl
from jax.experimental.pallas import tpu as pltpu

F32 = jnp.float32
BF16 = jnp.bfloat16

D = 1024
B = 8
SEQ = 2048
CTX = 256
T = CTX + SEQ
DEPTH = 4
GRID_W = 64
FFN_H = ((8 * D + 3 * 256 - 1) // (3 * 256)) * 256
S5_W = D // 2
S5_GROUP = 16
S5_GROUPS = S5_W // S5_GROUP
S5_STATE = 64
HG_HEADS = 4
HG_DIM = D // 8
HG_W = HG_HEADS * HG_DIM
MAX_EXP_ARG = 60.0
DIFF_HEADS = 4
DIFF_DIM = D // 16
DIFF_W = DIFF_HEADS * 2 * DIFF_DIM
MLA_HEADS = 4
MLA_NOPE = D // 8
MLA_ROPE = D // 16
MLA_V = D // 8
MLA_Q_RANK = 3 * D // 8
MLA_KV_RANK = D // 4
ROPE_DIM = D // 16
ROPE_BASE = 10000.0
EPS = 1e-6
AB_IN = S5_W + 5 * HG_W
CD_IN = 3 * DIFF_W + MLA_Q_RANK + MLA_KV_RANK + MLA_ROPE
CD_PAD = CD_IN + 64

LANE = 128
SUBLANE = 8
TS = CTX
NTS = T // TS
TM = 512
NTL = SEQ // TM
MOD_ROWS = 16
CTX_ROW = B
S5_LC = 512
S5_NC = S5_GROUPS * S5_STATE // S5_LC
S5_UC = S5_LC // S5_STATE * S5_GROUP
S5_SB = 16
HG_C = TS
FFN_BOUNDS = (0, 768, 1536, 2304, FFN_H)
VMEM_LIMIT = 56 * 1024 * 1024
LOG2E = math.log2(math.e)


def _cparams(sem):
    return pltpu.CompilerParams(dimension_semantics=sem, vmem_limit_bytes=VMEM_LIMIT)


def _const_spec(shape):
    n = len(shape)
    return pl.BlockSpec(shape, lambda *_: (0,) * n, pipeline_mode=pl.Buffered(1))


def _layer_spec(shape, l):
    n = len(shape)
    return pl.BlockSpec((None,) + tuple(shape), lambda *_: (l,) + (0,) * n, pipeline_mode=pl.Buffered(1))


def _stream_specs(rows, n_lat, ctx_block):
    return [pl.BlockSpec((None, rows, D), lambda b, t: (b, jnp.minimum(t, n_lat - 1), 0)),
            pl.BlockSpec((None, CTX, D), lambda b, t: (b, ctx_block, 0))]


def _silu(x):
    return x * jax.nn.sigmoid(x)


def _norm_mod(x, g, shift, scale):
    ms = jnp.mean(x * x, axis=-1, keepdims=True)
    return x * lax.rsqrt(ms + EPS) * g * (1.0 + scale) + shift


def _dot(a, b):
    return jnp.dot(a, b, preferred_element_type=F32)


def _dot_nt(a, b):
    return lax.dot_general(a, b, (((1,), (1,)), ((), ())), preferred_element_type=F32)


def _mod_kernel(s_ref, w_ref, b_ref, o_ref):
    s = _silu(s_ref[...])
    o_ref[...] = _dot(s.astype(BF16), w_ref[...].astype(BF16)) + b_ref[...]


def _modulation(c, c_ctx, ada_w, ada_b):
    s = jnp.zeros((MOD_ROWS, D), F32).at[:B].set(c).at[CTX_ROW].set(c_ctx)
    nb = 1536
    out = pl.pallas_call(
        _mod_kernel,
        grid=(DEPTH, 6 * D // nb),
        in_specs=[
            pl.BlockSpec((MOD_ROWS, D), lambda l, n: (0, 0)),
            pl.BlockSpec((None, D, nb), lambda l, n: (l, 0, n)),
            pl.BlockSpec((None, 1, nb), lambda l, n: (l, 0, n)),
        ],
        out_specs=pl.BlockSpec((None, MOD_ROWS, nb), lambda l, n: (l, 0, n)),
        out_shape=jax.ShapeDtypeStruct((DEPTH, MOD_ROWS, 6 * D), F32),
        compiler_params=_cparams(("arbitrary", "arbitrary")),
        name="adaln_mod",
    )(s, ada_w, ada_b.reshape(DEPTH, 1, 6 * D))
    return out.reshape(DEPTH, MOD_ROWS, 6, D)


def _mod_spec(ctx_tile):
    return pl.BlockSpec((None, 6, D), lambda b, t: (jnp.where(t == ctx_tile, CTX_ROW, b), 0, 0))


def _for_tile_rows(has_ctx, body):
    if has_ctx:
        t = pl.program_id(1)
        pl.when(t < NTL)(lambda: body(TM))
        pl.when(t == NTL)(lambda: body(CTX))
    else:
        body(TM)


def _even_in_kernel(xl_ref, xc_ref, mod_ref, g_ref, w_ref, p_ref):
    def body(n):
        x = xl_ref[...] if n == TM else xc_ref[...]
        a = _norm_mod(x, g_ref[...], mod_ref[0:1, :], mod_ref[1:2, :])
        p_ref[0:n] = _dot(a.astype(BF16), w_ref[...])

    _for_tile_rows(True, body)


def _even_in(h_lat, h_ctx, ctx_block, mod_l, g, w_all, e):
    return pl.pallas_call(
        _even_in_kernel,
        grid=(B, NTL + 1),
        in_specs=_stream_specs(TM, NTL, ctx_block) + [
            _mod_spec(NTL),
            _const_spec((1, D)),
            _layer_spec((D, AB_IN), e),
        ],
        out_specs=pl.BlockSpec((None, TM, AB_IN), lambda b, t: (b, t, 0)),
        out_shape=jax.ShapeDtypeStruct((B, T, AB_IN), F32),
        compiler_params=_cparams(("arbitrary", "arbitrary")),
        name="even_in_proj",
    )(h_lat, h_ctx, mod_l, g, w_all)


def _s5_kernel(uf_ref, wf_ref, cf_ref, arf_ref, aif_ref, ur_ref, wr_ref, cr_ref, arr_ref, air_ref,
               yf_ref, yr_ref, uxf_ref, yxf_ref, buf_ref, hf_ref, uxr_ref, yxr_ref, bur_ref, hr_ref):
    @pl.when(pl.program_id(1) == 0)
    def _():
        hf_ref[...] = jnp.zeros_like(hf_ref)
        hr_ref[...] = jnp.zeros_like(hr_ref)

    n_sb = TS // S5_SB
    dirs = (
        (False, uf_ref, wf_ref, cf_ref, arf_ref[...], aif_ref[...], yf_ref, uxf_ref, yxf_ref, buf_ref, hf_ref),
        (True, ur_ref, wr_ref, cr_ref, arr_ref[...], air_ref[...], yr_ref, uxr_ref, yxr_ref, bur_ref, hr_ref),
    )

    def rows_of(sb):
        return slice(sb * S5_SB * B, (sb + 1) * S5_SB * B)

    def sub_block(rev, n):
        return n_sb - 1 - n if rev else n

    def project(d, n):
        rev, _, w_ref, _, _, _, _, ux_ref, _, bu_ref, _ = d
        rows = rows_of(sub_block(rev, n))
        bu_ref[rows, :] = _dot(ux_ref[rows, :].astype(BF16), w_ref[...])

    for _, u_ref, _, _, _, _, _, ux_ref, _, _, _ in dirs:
        for b in range(B):
            ux_ref[pl.ds(b, TS, stride=B), :] = u_ref[b]
    state = [(d[10][0], d[10][1]) for d in dirs]
    for d in dirs:
        project(d, 0)
    for n in range(n_sb):
        if n + 1 < n_sb:
            for d in dirs:
                project(d, n + 1)
        for k in range(S5_SB):
            for i, (rev, _, _, _, ar, ai, _, _, _, bu_ref, _) in enumerate(dirs):
                t = S5_SB - 1 - k if rev else k
                r0 = rows_of(sub_block(rev, n)).start + t * B
                r = slice(r0, r0 + B)
                hr, hi = state[i]
                nr = ar * hr - ai * hi + bu_ref[r, 0:S5_LC]
                ni = ar * hi + ai * hr + bu_ref[r, S5_LC:2 * S5_LC]
                bu_ref[r, 0:S5_LC] = nr
                bu_ref[r, S5_LC:2 * S5_LC] = ni
                state[i] = (nr, ni)
        for rev, _, _, c_ref, _, _, _, _, yx_ref, bu_ref, _ in dirs:
            rows = rows_of(sub_block(rev, n))
            yx_ref[rows, :] = _dot(bu_ref[rows, :].astype(BF16), c_ref[...])
    for i, (_, _, _, _, _, _, y_ref, _, yx_ref, _, h_ref) in enumerate(dirs):
        h_ref[0], h_ref[1] = state[i]
        for b in range(B):
            y_ref[b] = yx_ref[pl.ds(b, TS, stride=B), :]


def _seq_tile(d, i):
    return jnp.where(i == 0, NTS - 1, jnp.where(d == 0, i - 1, NTS - 1 - i))


def _s5_scan(proj, w_in, w_out, a_r, a_i):
    def operands(d):
        par = lambda *shape: pl.BlockSpec((None, None) + shape, lambda c, i: (d, c, 0, 0))
        return [pl.BlockSpec((B, TS, S5_UC), lambda c, i: (0, _seq_tile(d, i), c)),
                par(S5_UC, 2 * S5_LC), par(2 * S5_LC, S5_UC), par(B, S5_LC), par(B, S5_LC)]

    scratch = [pltpu.VMEM((TS * B, S5_UC), F32), pltpu.VMEM((TS * B, S5_UC), F32),
               pltpu.VMEM((TS * B, 2 * S5_LC), F32), pltpu.VMEM((2, B, S5_LC), F32)]
    out = jax.ShapeDtypeStruct((B, T, S5_W), F32)
    return pl.pallas_call(
        _s5_kernel,
        grid=(S5_NC, NTS),
        in_specs=operands(0) + operands(1),
        out_specs=[pl.BlockSpec((B, TS, S5_UC), lambda c, i, d=d: (0, _seq_tile(d, i), c)) for d in (0, 1)],
        out_shape=[out, out],
        scratch_shapes=scratch + scratch,
        compiler_params=_cparams(("arbitrary", "arbitrary")),
        name="s5_scan",
    )(proj, w_in, w_out, a_r, a_i, proj, w_in, w_out, a_r, a_i)


def _s5_params(lam_re, lam_im, log_step, b_re, b_im, c_re, c_im):
    lr = jnp.minimum(lam_re.astype(F32), -1e-4)
    li = lam_im.astype(F32)
    step = jnp.exp(log_step.astype(F32))[..., None]
    mag = jnp.exp(lr * step)
    a_r = mag * jnp.cos(li * step)
    a_i = mag * jnp.sin(li * step)
    den = lr * lr + li * li
    coef_r = ((a_r - 1) * lr + a_i * li) / den
    coef_i = (a_i * lr - (a_r - 1) * li) / den
    br = b_re.astype(F32)
    bi = b_im.astype(F32)
    bb_r = coef_r[..., None] * br - coef_i[..., None] * bi
    bb_i = coef_r[..., None] * bi + coef_i[..., None] * br
    gpc = S5_LC // S5_STATE
    eye = jnp.eye(gpc, dtype=F32)

    def in_blocks(bb):
        bb = bb.reshape(2, S5_NC, gpc, S5_STATE, S5_GROUP)
        return jnp.einsum('dngpc,gh->dngchp', bb, eye).reshape(2, S5_NC, S5_UC, S5_LC)

    def out_blocks(cc):
        cc = cc.reshape(2, S5_NC, gpc, S5_GROUP, S5_STATE)
        return jnp.einsum('dngcp,gh->dngphc', cc, eye).reshape(2, S5_NC, S5_LC, S5_UC)

    w_in = jnp.concatenate([in_blocks(bb_r), in_blocks(bb_i)], axis=-1).astype(BF16)
    w_out = jnp.concatenate([out_blocks(c_re.astype(F32)), -out_blocks(c_im.astype(F32))], axis=-2).astype(BF16)

    def lanes(a):
        return jnp.broadcast_to(a.reshape(2, S5_NC, 1, S5_LC), (2, S5_NC, B, S5_LC))

    return w_in, w_out, lanes(a_r), lanes(a_i)


HG_HALF = HG_C // 2
HG_LEVELS = tuple(2 ** e for e in range(int(math.log2(HG_HALF))))


def _hgrn_maps(rev):
    pos = np.arange(HG_C)
    ph = np.arange(HG_HALF)
    if rev:
        pos = HG_C - 1 - pos
        ph = HG_HALF - 1 - ph
    tri = (pos[None, :] <= pos[:, None]).astype(np.float32)
    pt, ps = ph[:, None], ph[None, :]
    lv = np.full((HG_HALF, HG_HALF), -1, np.int32)
    lv[pt == ps] = len(HG_LEVELS)
    for e, m in enumerate(HG_LEVELS):
        x, y = pt // m, ps // m
        lv[(x == y + 1) & (x % 2 == 1)] = e
    return jnp.asarray(tri, BF16), jnp.asarray(lv)


def _hgrn_chunk(q_ref, f_ref, v_ref, lb, tri_ref, lv_ref, o_ref, s_ref, rev):
    c = HG_C
    fl = f_ref[...]
    qb = _silu(q_ref[...]).astype(BF16)
    v = v_ref[...]
    x2 = fl * LOG2E
    a = jnp.exp2(-jnp.abs(x2))
    s1 = 1.0 + a
    r = 1.0 / s1
    kb = ((1.0 - lb) * jnp.where(x2 >= 0.0, a * r, r)).astype(BF16)
    t2 = jnp.exp2(jnp.minimum(-x2, MAX_EXP_ARG * LOG2E))
    lf = jnp.minimum(x2, 0.0) - jnp.log2(s1) + jnp.log2(1.0 + lb * t2)

    row = lax.broadcasted_iota(jnp.int32, (c, 1), 0)
    pos = (c - 1 - row) if rev else row
    first, second = (slice(HG_HALF, c), slice(0, HG_HALF)) if rev else (slice(0, HG_HALF), slice(HG_HALF, c))
    halves = (first, second)

    def from_earlier(x, j):
        return pltpu.roll(x, (c - j) if rev else j, 0)

    def from_later(x, j):
        return pltpu.roll(x, j if rev else (c - j), 0)

    def head(x, h, rows):
        return x[rows, h * HG_DIM:(h + 1) * HG_DIM]

    hi = lf.astype(BF16)
    r1 = lf - hi.astype(F32)
    mid = r1.astype(BF16)
    lo = (r1 - mid.astype(F32)).astype(BF16)
    tri = tri_ref[...]
    cum = _dot(tri, hi) + _dot(tri, mid) + _dot(tri, lo)

    lv = lv_ref[...]
    diag = len(HG_LEVELS)
    scores = [[jnp.where(lv == diag, _dot_nt(head(qb, h, r), head(kb, h, r)), 0.0) for r in halves]
              for h in range(HG_HEADS)]

    g_end = cum
    for e, m in enumerate(HG_LEVELS):
        qm = qb * jnp.exp2(cum - from_earlier(g_end, m)).astype(BF16)
        km = kb * jnp.exp2(g_end - cum).astype(BF16)
        for h in range(HG_HEADS):
            for j, r in enumerate(halves):
                scores[h][j] = jnp.where(lv == e, _dot_nt(head(qm, h, r), head(km, h, r)), scores[h][j])
        g_end = jnp.where((pos & m) != 0, g_end, from_later(g_end, m))
    qm = qb * jnp.exp2(cum - from_earlier(g_end, HG_HALF)).astype(BF16)
    km = kb * jnp.exp2(g_end - cum).astype(BF16)
    cross = [_dot_nt(head(qm, h, second), head(km, h, first)) for h in range(HG_HEADS)]
    g_end = jnp.where((pos & HG_HALF) != 0, g_end, from_later(g_end, HG_HALF))

    q_in = qb * jnp.exp2(cum).astype(BF16)
    k_out = kb * jnp.exp2(g_end - cum).astype(BF16)
    decay = jnp.exp2(g_end[0:1, :])
    vb = v.astype(BF16)
    for h in range(HG_HEADS):
        sl = slice(h * HG_DIM, (h + 1) * HG_DIM)
        st = s_ref[h]
        inter = _dot_nt(q_in[:, sl], st.astype(BF16))
        v_first, v_second = head(vb, h, first), head(vb, h, second)
        o_ref[first, sl] = inter[first] + _dot(scores[h][0].astype(BF16), v_first)
        o_ref[second, sl] = (inter[second] + _dot(cross[h].astype(BF16), v_first)
                             + _dot(scores[h][1].astype(BF16), v_second))
        s_ref[h] = decay[:, sl] * st + _dot(v[:, sl].T.astype(BF16), k_out[:, sl])


def _hgrn_kernel(qf_ref, ff_ref, vf_ref, qr_ref, fr_ref, vr_ref, lb_ref, trif_ref, lvf_ref, trir_ref, lvr_ref,
                 of_ref, or_ref, sf_ref, sr_ref):
    @pl.when(pl.program_id(1) == 0)
    def _():
        sf_ref[...] = jnp.zeros_like(sf_ref)
        sr_ref[...] = jnp.zeros_like(sr_ref)

    lb = lb_ref[...]
    _hgrn_chunk(qf_ref, ff_ref, vf_ref, lb, trif_ref, lvf_ref, of_ref, sf_ref, rev=False)
    _hgrn_chunk(qr_ref, fr_ref, vr_ref, lb, trir_ref, lvr_ref, or_ref, sr_ref, rev=True)


def _hgrn_scan(hg, lb):
    maps = [_hgrn_maps(rev) for rev in (False, True)]
    blk = lambda d, col: pl.BlockSpec((None, HG_C, HG_W), lambda b, i: (b, _seq_tile(d, i), col))
    state = pltpu.VMEM((HG_HEADS, HG_DIM, HG_DIM), F32)
    out = jax.ShapeDtypeStruct((B, T, HG_W), F32)
    return pl.pallas_call(
        _hgrn_kernel,
        grid=(B, NTS),
        in_specs=[blk(0, 1), blk(0, 2), blk(0, 4), blk(1, 1), blk(1, 3), blk(1, 4),
                  _const_spec((1, HG_W)),
                  _const_spec((HG_C, HG_C)), _const_spec((HG_HALF, HG_HALF)),
                  _const_spec((HG_C, HG_C)), _const_spec((HG_HALF, HG_HALF))],
        out_specs=[blk(0, 0), blk(1, 0)],
        out_shape=[out, out],
        scratch_shapes=[state, state],
        compiler_params=_cparams(("arbitrary", "arbitrary")),
        name="hgrn_scan",
    )(hg, hg, hg, hg, hg, hg, lb, *maps[0], *maps[1])


def _ffn_tail(mix_out, h, mod_ref, g_ref, w_in_ref, w_out_ref):
    h1 = h + mod_ref[2:3, :] * mix_out
    a = _norm_mod(h1, g_ref[...], mod_ref[3:4, :], mod_ref[4:5, :]).astype(BF16)

    def gate_up(c):
        lo, hi = FFN_BOUNDS[c], FFN_BOUNDS[c + 1]
        return _dot(a, w_in_ref[:, lo:hi]), _dot(a, w_in_ref[:, FFN_H + lo:FFN_H + hi])

    ffn = None
    ahead = gate_up(0)
    for c in range(len(FFN_BOUNDS) - 1):
        gate, up = ahead
        if c + 2 < len(FFN_BOUNDS):
            ahead = gate_up(c + 1)
        part = _dot((_silu(gate) * up).astype(BF16), w_out_ref[FFN_BOUNDS[c]:FFN_BOUNDS[c + 1], :])
        ffn = part if ffn is None else ffn + part
    return h1 + mod_ref[5:6, :] * ffn


def _gelu_tanh(x):
    return 0.5 * x * (1.0 + jnp.tanh(math.sqrt(2.0 / math.pi) * (x + 0.044715 * (x * x * x))))


def _post_even_kernel(yf_ref, yb_ref, u_ref, of_ref, ob_ref, gate_ref, hl_ref, hc_ref, mod_ref, g_ref,
                      dsk_ref, gw_ref, gb_ref, on_ref, wo_ref, w_in_ref, w_out_ref, o_ref):
    def mixer_out(r):
        y = yf_ref[r] + yb_ref[r] + u_ref[r] * dsk_ref[...]
        z = _gelu_tanh(y)
        s5 = z * jax.nn.sigmoid(_dot(z.astype(BF16), gw_ref[...]) + gb_ref[...])
        mix = _dot(s5.astype(BF16), wo_ref[0:S5_W, :])
        o = of_ref[r] + ob_ref[r]
        gate = _silu(gate_ref[r])
        for h in range(HG_HEADS):
            sl = slice(h * HG_DIM, (h + 1) * HG_DIM)
            oh = o[:, sl]
            ms = jnp.mean(oh * oh, axis=-1, keepdims=True)
            hn = oh * lax.rsqrt(ms + EPS) * on_ref[...] * gate[:, sl]
            mix = mix + _dot(hn.astype(BF16), wo_ref[S5_W + h * HG_DIM:S5_W + (h + 1) * HG_DIM, :])
        return mix

    is_ctx = pl.program_id(1) == NTS - 1
    halves = (slice(0, TS // 2), slice(TS // 2, TS))
    mixes = [mixer_out(r) for r in halves]
    for mix, r in zip(mixes, halves):
        h = jnp.where(is_ctx, hc_ref[r], hl_ref[r])
        o_ref[r] = _ffn_tail(mix, h, mod_ref, g_ref, w_in_ref, w_out_ref)


def _post_even(y_f, y_b, o_f, o_b, proj, h_lat, h_ctx, ctx_block, mod_l, g, dsk, gw, gb, on, wo, w_in, w_out, l):
    tok = lambda b, t: (b, t, 0)
    return pl.pallas_call(
        _post_even_kernel,
        grid=(B, NTS),
        in_specs=[
            pl.BlockSpec((None, TS, S5_W), tok),
            pl.BlockSpec((None, TS, S5_W), tok),
            pl.BlockSpec((None, TS, S5_W), tok),
            pl.BlockSpec((None, TS, HG_W), tok),
            pl.BlockSpec((None, TS, HG_W), tok),
            pl.BlockSpec((None, TS, HG_W), lambda b, t: (b, t, 5)),
        ] + _stream_specs(TS, NTS - 1, ctx_block) + [
            _mod_spec(NTS - 1),
            _const_spec((1, D)),
            _const_spec((1, S5_W)),
            _const_spec((S5_W, S5_W)),
            _const_spec((1, S5_W)),
            _const_spec((1, HG_DIM)),
            _layer_spec((D, D), l),
            _layer_spec((D, 2 * FFN_H), l),
            _layer_spec((FFN_H, D), l),
        ],
        out_specs=pl.BlockSpec((None, TS, D), tok),
        out_shape=jax.ShapeDtypeStruct((B, T, D), F32),
        compiler_params=_cparams(("arbitrary", "arbitrary")),
        name="post_even",
    )(y_f, y_b, proj, o_f, o_b, proj, h_lat, h_ctx, mod_l, g, dsk, gw, gb, on, wo, w_in, w_out)


def _post_odd_kernel(mx_ref, h_ref, mod_ref, g_ref, wo_ref, w_in_ref, w_out_ref, o_ref):
    halves = (slice(0, TS // 2), slice(TS // 2, TS))
    mixes = [_dot(mx_ref[r], wo_ref[...]) for r in halves]
    for mix, r in zip(mixes, halves):
        o_ref[r] = _ffn_tail(mix, h_ref[r], mod_ref, g_ref, w_in_ref, w_out_ref)


def _post_odd(mx, hs, mod_l, g, wo, w_in, w_out, l, latent_only):
    nt = NTS - 1 if latent_only else NTS
    tok = lambda b, t: (b, t, 0)
    return pl.pallas_call(
        _post_odd_kernel,
        grid=(B, nt),
        in_specs=[
            pl.BlockSpec((None, TS, D), tok),
            pl.BlockSpec((None, TS, D), tok),
            _mod_spec(NTS - 1),
            _const_spec((1, D)),
            _layer_spec((D, D), l),
            _layer_spec((D, 2 * FFN_H), l),
            _layer_spec((FFN_H, D), l),
        ],
        out_specs=pl.BlockSpec((None, TS, D), tok),
        out_shape=jax.ShapeDtypeStruct((B, nt * TS, D), F32),
        compiler_params=_cparams(("arbitrary", "arbitrary")),
        name="post_odd",
    )(mx, hs, mod_l, g, wo, w_in, w_out)


def _group_rms(xs, r):
    ms = _dot((xs * xs).astype(BF16), r)
    return xs * lax.rsqrt(ms + EPS)


def _rope(xs, cos, sin_signed, first_half):
    rot = jnp.where(first_half, pltpu.roll(xs, LANE - ROPE_DIM // 4, 1), pltpu.roll(xs, ROPE_DIM // 4, 1))
    return xs * cos + rot * sin_signed


def _odd_in_kernel(x_ref, mod_ref, g_ref, w_ref, rope_ref, gv_ref, qan_ref, kvan_ref, wuq_ref, wukv_ref,
                   rpair_ref, rq_ref, rfull_ref, rhalf_ref,
                   dq_ref, k1_ref, k2_ref, dv_ref, mq_ref, mk_ref, mv_ref, *, mla_scale):
    lane = lax.broadcasted_iota(jnp.int32, (1, LANE), 1)
    first_half = (lane % (ROPE_DIM // 2)) < (ROPE_DIM // 4)
    low = lane < DIFF_DIM
    rpair, rq, rfull, rhalf = rpair_ref[...], rq_ref[...], rfull_ref[...], rhalf_ref[...]
    g_dq, g_dk = gv_ref[0:1, :], gv_ref[1:2, :]
    g_qn, g_qr, g_kn, g_kr = gv_ref[2:3, :], gv_ref[3:4, :], gv_ref[4:5, :], gv_ref[5:6, :]
    diff_scale = DIFF_DIM ** -0.5 * LOG2E
    o_cq = 3 * DIFF_W
    o_ckv = o_cq + MLA_Q_RANK
    o_kr = o_ckv + MLA_KV_RANK

    def body(n):
        rows = slice(0, n)
        a = _norm_mod(x_ref[rows], g_ref[...], mod_ref[0:1, :], mod_ref[1:2, :]).astype(BF16)
        cos_d, sin_d, cos_m, sin_m = rope_ref[0, rows], rope_ref[1, rows], rope_ref[2, rows], rope_ref[3, rows]
        ones = jnp.ones((n, LANE), BF16)

        def proj(c0, c1):
            return _dot(a, w_ref[:, c0:c1])

        def diff_pair(pq, pk, h2):
            qn = _group_rms(pq, rpair)
            kn = _group_rms(pk, rpair)
            for j in range(2):
                sl = slice((h2 + j) * LANE, (h2 + j + 1) * LANE)
                half = slice(j * LANE, (j + 1) * LANE)
                qh = _rope(qn[:, half] * g_dq, cos_d, sin_d, first_half)
                dq_ref[rows, sl] = (qh * diff_scale).astype(BF16)
                kh = _rope(kn[:, half] * g_dk, cos_d, sin_d, first_half)
                k1_ref[rows, sl] = jnp.where(low, kh, 0.0).astype(BF16)
                k2_ref[rows, sl] = jnp.where(low, 0.0, kh).astype(BF16)

        def diff_values(pv):
            for h in range(DIFF_HEADS):
                dv_ref[rows, 2 * h * LANE:(2 * h + 1) * LANE] = pv[:, h * LANE:(h + 1) * LANE].astype(BF16)
                dv_ref[rows, (2 * h + 1) * LANE:(2 * h + 2) * LANE] = ones

        def latent_heads(pm):
            cq = pm[:, 0:MLA_Q_RANK]
            cqn = cq * lax.rsqrt(jnp.mean(cq * cq, axis=-1, keepdims=True) + EPS) * qan_ref[...]
            q = _dot(cqn.astype(BF16), wuq_ref[...])
            ckv = pm[:, MLA_Q_RANK:MLA_Q_RANK + MLA_KV_RANK]
            ckvn = ckv * lax.rsqrt(jnp.mean(ckv * ckv, axis=-1, keepdims=True) + EPS) * kvan_ref[...]
            kv = _dot(ckvn.astype(BF16), wukv_ref[...])
            k_rope = _rope(_group_rms(pm[:, MLA_Q_RANK + MLA_KV_RANK:], rhalf) * g_kr,
                           cos_m, sin_m, first_half).astype(BF16)
            for h in range(MLA_HEADS):
                o = 2 * LANE * h
                qn = _group_rms(q[:, o:o + 2 * LANE], rq)
                q_nope = qn[:, :LANE] * g_qn
                q_rope = _rope(qn[:, LANE:] * g_qr, cos_m, sin_m, first_half)
                mq_ref[rows, o:o + LANE] = (q_nope * mla_scale).astype(BF16)
                mq_ref[rows, o + LANE:o + 2 * LANE] = (q_rope * mla_scale).astype(BF16)
                mk_ref[rows, o:o + LANE] = (_group_rms(kv[:, o:o + LANE], rfull) * g_kn).astype(BF16)
                mk_ref[rows, o + LANE:o + 2 * LANE] = k_rope
                mv_ref[rows, o:o + LANE] = kv[:, o + LANE:o + 2 * LANE].astype(BF16)
                mv_ref[rows, o + LANE:o + 2 * LANE] = ones

        pair = 2 * LANE
        pm = proj(o_cq, CD_PAD)
        pq0, pk0 = proj(0, pair), proj(DIFF_W, DIFF_W + pair)
        latent_heads(pm)
        pq1, pk1 = proj(pair, 2 * pair), proj(DIFF_W + pair, DIFF_W + 2 * pair)
        diff_pair(pq0, pk0, 0)
        pv = proj(2 * DIFF_W, 3 * DIFF_W)
        diff_pair(pq1, pk1, 2)
        diff_values(pv)

    _for_tile_rows(True, body)


def _odd_in(hs, mod_l, g, w_all, o, rope, gv, qan, kvan, wuq, wukv, rmats):
    tok = lambda b, t: (b, t, 0)
    sds = lambda n: jax.ShapeDtypeStruct((B, T, n), BF16)
    wide = 2 * LANE * MLA_HEADS
    widths = (DIFF_W, DIFF_W, DIFF_W, wide, wide, wide, wide)
    return pl.pallas_call(
        functools.partial(_odd_in_kernel, mla_scale=(MLA_NOPE + MLA_ROPE) ** -0.5 * LOG2E),
        grid=(B, NTL + 1),
        in_specs=[
            pl.BlockSpec((None, TM, D), tok),
            _mod_spec(NTL),
            _const_spec((1, D)),
            _layer_spec((D, CD_PAD), o),
            pl.BlockSpec((4, TM, LANE), lambda b, t: (0, t, 0)),
            _const_spec((SUBLANE, LANE)),
            _const_spec((1, MLA_Q_RANK)),
            _const_spec((1, MLA_KV_RANK)),
            _const_spec((MLA_Q_RANK, 2 * LANE * MLA_HEADS)),
            _const_spec((MLA_KV_RANK, 2 * LANE * MLA_HEADS)),
            _const_spec((2 * LANE, 2 * LANE)),
            _const_spec((2 * LANE, 2 * LANE)),
            _const_spec((LANE, LANE)),
            _const_spec((LANE, LANE)),
        ],
        out_specs=[pl.BlockSpec((None, TM, n), tok) for n in widths],
        out_shape=[sds(n) for n in widths],
        compiler_params=_cparams(("arbitrary", "arbitrary")),
        name="odd_in_proj",
    )(hs, mod_l, g, w_all, rope, gv, qan, kvan, wuq, wukv, *rmats)


def _exp2_shifted(s):
    return jnp.exp2(s - jnp.max(s, axis=-1, keepdims=True)).astype(BF16)


def _pv_normalised(e, v_aug):
    o = _dot(e, v_aug)
    return o[:, :LANE] / o[:, LANE:]


def _attn_kernel(lam_ref, dq_ref, k1_ref, k2_ref, dv_ref, mq_ref, mk_ref, mv_ref, sub_ref, o_ref,
                 *, out_scale, has_ctx):
    lam = lam_ref[0, 0]

    def body(keys):
        for h in range(DIFF_HEADS):
            sl = slice(h * LANE, (h + 1) * LANE)
            sv = slice(2 * LANE * h, 2 * LANE * (h + 1))
            qh = dq_ref[:, sl]
            o1 = _pv_normalised(_exp2_shifted(_dot_nt(qh, k1_ref[keys, sl])), dv_ref[keys, sv])
            o2 = _pv_normalised(_exp2_shifted(_dot_nt(qh, k2_ref[keys, sl])), dv_ref[keys, sv])
            o = o1 - lam * o2
            o = o * lax.rsqrt(jnp.mean(o * o, axis=-1, keepdims=True) + EPS) * sub_ref[...] * out_scale
            o_ref[:, sl] = o.astype(BF16)
        for h in range(MLA_HEADS):
            sq = slice(2 * LANE * h, 2 * LANE * (h + 1))
            o = _pv_normalised(_exp2_shifted(_dot_nt(mq_ref[:, sq], mk_ref[keys, sq])), mv_ref[keys, sq])
            o_ref[:, DIFF_W + h * LANE:DIFF_W + (h + 1) * LANE] = o.astype(BF16)

    if has_ctx:
        t = pl.program_id(1)
        pl.when(t == 0)(lambda: body(slice(SEQ, T)))
        pl.when(t > 0)(lambda: body(slice(0, T)))
    else:
        body(slice(0, T))


def _attention(lam, dq, k1, k2, dv, mq, mk, mv, sub, out_scale, latent_only):
    has_ctx = not latent_only
    nt = NTS if has_ctx else NTS - 1
    wide = 2 * LANE * MLA_HEADS
    tile = (lambda t: (t + NTS - 1) % NTS) if has_ctx else (lambda t: t)
    qspec = lambda n: pl.BlockSpec((None, TS, n), lambda b, t: (b, tile(t), 0))
    kspec = lambda n: pl.BlockSpec((None, T, n), lambda b, t: (b, 0, 0))
    return pl.pallas_call(
        functools.partial(_attn_kernel, out_scale=out_scale, has_ctx=has_ctx),
        grid=(B, nt),
        in_specs=[
            pl.BlockSpec(memory_space=pltpu.SMEM),
            qspec(DIFF_W), kspec(DIFF_W), kspec(DIFF_W), kspec(wide),
            qspec(wide), kspec(wide), kspec(wide),
            _const_spec((1, LANE)),
        ],
        out_specs=qspec(D),
        out_shape=jax.ShapeDtypeStruct((B, nt * TS, D), BF16),
        compiler_params=_cparams(("arbitrary", "arbitrary")),
        name="attention",
    )(lam, dq, k1, k2, dv, mq, mk, mv, sub)


def _rope_tables():
    n_tok = SEQ
    rows = jnp.repeat(jnp.arange(n_tok // GRID_W, dtype=jnp.int32), GRID_W)
    cols = jnp.tile(jnp.arange(GRID_W, dtype=jnp.int32), n_tok // GRID_W)
    n_freq = ROPE_DIM // 4
    inv = jnp.power(ROPE_BASE, -jnp.arange(n_freq, dtype=F32) / n_freq)
    ang_r = rows.astype(F32)[:, None] * inv
    ang_c = cols.astype(F32)[:, None] * inv
    ang = jnp.concatenate([ang_r, ang_r, ang_c, ang_c], axis=-1)
    sign = jnp.where((jnp.arange(ROPE_DIM) % (ROPE_DIM // 2)) < n_freq, -1.0, 1.0).astype(F32)
    cos = jnp.concatenate([jnp.cos(ang), jnp.ones((CTX, ROPE_DIM), F32)], axis=0)
    sin = jnp.concatenate([jnp.sin(ang) * sign, jnp.zeros((CTX, ROPE_DIM), F32)], axis=0)
    one, zero = jnp.ones_like(cos), jnp.zeros_like(sin)
    cat = lambda a, b: jnp.concatenate([a, b], axis=-1)
    return jnp.stack([cat(cos, cos), cat(sin, sin), cat(cos, one), cat(sin, zero)])


def _averaging_mats():
    idx = np.arange(2 * LANE)
    same64 = (idx[:, None] // DIFF_DIM) == (idx[None, :] // DIFF_DIM)
    rpair = np.where(same64, 1.0 / DIFF_DIM, 0.0)
    rfull = np.full((LANE, LANE), 1.0 / LANE)
    il = np.arange(LANE)
    rhalf = np.where((il[:, None] < MLA_ROPE) & (il[None, :] < MLA_ROPE), 1.0 / MLA_ROPE, 0.0)
    rq = np.zeros((2 * LANE, 2 * LANE))
    rq[:LANE, :LANE] = rfull
    rq[LANE:, LANE:] = rhalf
    return tuple(jnp.asarray(m, BF16) for m in (rpair, rq, rfull, rhalf))


def kernel(x, c, ctx, c_ctx, ada_w, ada_b, norm_mix, norm_ffn, w_out, ffn_w_in, ffn_w_out, ab_w_in, s5_lambda_re, s5_lambda_im, s5_log_step, s5_b_re, s5_b_im, s5_c_re, s5_c_im, s5_d, s5_glu_w, s5_glu_b, hgrn_lb_logits, hgrn_out_norm, cd_w_in, diff_lambda, diff_qk_norm, diff_subln, mla_q_a_norm, mla_kv_a_norm, mla_w_uq, mla_w_ukv, mla_nope_norm, mla_rope_norm):
    assert x.shape == (B, SEQ, D) and ctx.shape == (B, CTX, D)
    mod = _modulation(c, c_ctx, ada_w, ada_b)
    wo_all = w_out.astype(BF16)
    w1_all = ffn_w_in.astype(BF16)
    w2_all = ffn_w_out.astype(BF16)
    wab_all = ab_w_in.astype(BF16)
    wcd_all = jnp.pad(cd_w_in, ((0, 0), (0, 0), (0, CD_PAD - CD_IN))).astype(BF16)
    h_lat, h_ctx, ctx_block = x, ctx, 0
    rope = _rope_tables()
    rmats = _averaging_mats()

    lb_p = jax.nn.softmax(hgrn_lb_logits.astype(F32), axis=0)
    lower_bounds = jnp.cumsum(lb_p, axis=0) - lb_p[0:1]

    for l in range(DEPTH):
        last = l == DEPTH - 1
        g_mix = norm_mix[l].reshape(1, D)
        g_ffn = norm_ffn[l].reshape(1, D)
        if l % 2 == 0:
            e = l // 2
            proj = _even_in(h_lat, h_ctx, ctx_block, mod[l], g_mix, wab_all, e)
            s5w_in, s5w_out, a_r, a_i = _s5_params(s5_lambda_re[e], s5_lambda_im[e], s5_log_step[e],
                                                   s5_b_re[e], s5_b_im[e], s5_c_re[e], s5_c_im[e])
            y_f, y_b = _s5_scan(proj, s5w_in, s5w_out, a_r, a_i)
            lb = lower_bounds[e].reshape(1, HG_W)
            o_f, o_b = _hgrn_scan(proj, lb)
            hs = _post_even(y_f, y_b, o_f, o_b, proj, h_lat, h_ctx, ctx_block, mod[l], g_ffn,
                            s5_d[e].reshape(1, S5_W), s5_glu_w[e].astype(BF16), s5_glu_b[e].reshape(1, S5_W),
                            hgrn_out_norm[e].reshape(1, HG_DIM), wo_all, w1_all, w2_all, l)
        else:
            o = l // 2
            lam_init = 0.8 - 0.6 * math.exp(-0.3 * l)
            lv = diff_lambda[o].astype(F32)
            lam = (jnp.exp(jnp.sum(lv[0] * lv[1])) - jnp.exp(jnp.sum(lv[2] * lv[3])) + lam_init).reshape(1, 1)
            wuq = mla_w_uq[o].reshape(MLA_Q_RANK, MLA_HEADS, MLA_NOPE + MLA_ROPE)
            wuq = jnp.pad(wuq, ((0, 0), (0, 0), (0, 2 * LANE - MLA_NOPE - MLA_ROPE)))
            wuq = wuq.reshape(MLA_Q_RANK, 2 * LANE * MLA_HEADS).astype(BF16)
            pad_r = lambda v: jnp.pad(v, (0, LANE - MLA_ROPE))
            gv = jnp.zeros((SUBLANE, LANE), F32)
            gv = gv.at[0].set(jnp.tile(diff_qk_norm[o, 0], 2)).at[1].set(jnp.tile(diff_qk_norm[o, 1], 2))
            gv = gv.at[2].set(mla_nope_norm[o, 0]).at[3].set(pad_r(mla_rope_norm[o, 0]))
            gv = gv.at[4].set(mla_nope_norm[o, 1]).at[5].set(pad_r(mla_rope_norm[o, 1]))
            parts = _odd_in(hs, mod[l], g_mix, wcd_all, o, rope, gv, mla_q_a_norm[o].reshape(1, MLA_Q_RANK),
                            mla_kv_a_norm[o].reshape(1, MLA_KV_RANK), wuq, mla_w_ukv[o].astype(BF16), rmats)
            mx = _attention(lam, *parts, diff_subln[o].reshape(1, LANE), 1.0 - lam_init, latent_only=last)
            hs = _post_odd(mx, hs, mod[l], g_ffn, wo_all, w1_all, w2_all, l, latent_only=last)
        h_lat, h_ctx, ctx_block = hs, hs, SEQ // CTX
    return hs
```

```python
import functools
import math

import numpy as np
import jax
import jax.numpy as jnp
from jax import lax
from jax.experimental import pallas as pl
from jax.experimental.pallas import tpu as pltpu

F32 = jnp.float32
BF16 = jnp.bfloat16

D = 1024
B = 8
SEQ = 2048
CTX = 256
T = CTX + SEQ
DEPTH = 4
GRID_W = 64
FFN_H = ((8 * D + 3 * 256 - 1) // (3 * 256)) * 256
S5_W = D // 2
S5_GROUP = 16
S5_GROUPS = S5_W // S5_GROUP
S5_STATE = 64
HG_HEADS = 4
HG_DIM = D // 8
HG_W = HG_HEADS * HG_DIM
MAX_EXP_ARG = 60.0
DIFF_HEADS = 4
DIFF_DIM = D // 16
DIFF_W = DIFF_HEADS * 2 * DIFF_DIM
MLA_HEADS = 4
MLA_NOPE = D // 8
MLA_ROPE = D // 16
MLA_V = D // 8
MLA_Q_RANK = 3 * D // 8
MLA_KV_RANK = D // 4
ROPE_DIM = D // 16
ROPE_BASE = 10000.0
EPS = 1e-6
AB_IN = S5_W + 5 * HG_W
CD_IN = 3 * DIFF_W + MLA_Q_RANK + MLA_KV_RANK + MLA_ROPE
CD_PAD = CD_IN + 64

LANE = 128
SUBLANE = 8
TS = CTX
NTS = T // TS
TM = 512
NTL = SEQ // TM
MOD_ROWS = 16
CTX_ROW = B
S5_LC = 512
S5_NC = S5_GROUPS * S5_STATE // S5_LC
S5_UC = S5_LC // S5_STATE * S5_GROUP
S5_SB = 16
HG_C = TS
ATTN_AHEAD = 1
FFN_BOUNDS = (0, 768, 1536, 2304, FFN_H)
VMEM_LIMIT = 56 * 1024 * 1024
LOG2E = math.log2(math.e)


def _cparams(sem):
    return pltpu.CompilerParams(dimension_semantics=sem, vmem_limit_bytes=VMEM_LIMIT)


def _const_spec(shape):
    n = len(shape)
    return pl.BlockSpec(shape, lambda *_: (0,) * n, pipeline_mode=pl.Buffered(1))


def _layer_spec(shape, l):
    n = len(shape)
    return pl.BlockSpec((None,) + tuple(shape), lambda *_: (l,) + (0,) * n, pipeline_mode=pl.Buffered(1))


def _stream_specs(rows, n_lat, ctx_block):
    return [pl.BlockSpec((None, rows, D), lambda b, t: (b, jnp.minimum(t, n_lat - 1), 0)),
            pl.BlockSpec((None, CTX, D), lambda b, t: (b, ctx_block, 0))]


def _silu(x):
    return x * jax.nn.sigmoid(x)


def _norm_mod(x, g, shift, scale):
    ms = jnp.mean(x * x, axis=-1, keepdims=True)
    return x * lax.rsqrt(ms + EPS) * g * (1.0 + scale) + shift


def _dot(a, b):
    return jnp.dot(a, b, preferred_element_type=F32)


def _dot_nt(a, b):
    return lax.dot_general(a, b, (((1,), (1,)), ((), ())), preferred_element_type=F32)


def _mod_kernel(s_ref, w_ref, b_ref, o_ref):
    s = _silu(s_ref[...])
    o_ref[...] = _dot(s.astype(BF16), w_ref[...].astype(BF16)) + b_ref[...]


def _modulation(c, c_ctx, ada_w, ada_b):
    s = jnp.zeros((MOD_ROWS, D), F32).at[:B].set(c).at[CTX_ROW].set(c_ctx)
    nb = 1536
    out = pl.pallas_call(
        _mod_kernel,
        grid=(DEPTH, 6 * D // nb),
        in_specs=[
            pl.BlockSpec((MOD_ROWS, D), lambda l, n: (0, 0)),
            pl.BlockSpec((None, D, nb), lambda l, n: (l, 0, n)),
            pl.BlockSpec((None, 1, nb), lambda l, n: (l, 0, n)),
        ],
        out_specs=pl.BlockSpec((None, MOD_ROWS, nb), lambda l, n: (l, 0, n)),
        out_shape=jax.ShapeDtypeStruct((DEPTH, MOD_ROWS, 6 * D), F32),
        compiler_params=_cparams(("arbitrary", "arbitrary")),
        name="adaln_mod",
    )(s, ada_w, ada_b.reshape(DEPTH, 1, 6 * D))
    return out.reshape(DEPTH, MOD_ROWS, 6, D)


def _mod_spec(ctx_tile):
    return pl.BlockSpec((None, 6, D), lambda b, t: (jnp.where(t == ctx_tile, CTX_ROW, b), 0, 0))


def _for_tile_rows(has_ctx, body):
    if has_ctx:
        t = pl.program_id(1)
        pl.when(t < NTL)(lambda: body(TM))
        pl.when(t == NTL)(lambda: body(CTX))
    else:
        body(TM)


def _even_in_kernel(xl_ref, xc_ref, mod_ref, g_ref, w_ref, p_ref):
    def body(n):
        x = xl_ref[...] if n == TM else xc_ref[...]
        a = _norm_mod(x, g_ref[...], mod_ref[0:1, :], mod_ref[1:2, :])
        p_ref[0:n] = _dot(a.astype(BF16), w_ref[...])

    _for_tile_rows(True, body)


def _even_in(h_lat, h_ctx, ctx_block, mod_l, g, w_all, e):
    return pl.pallas_call(
        _even_in_kernel,
        grid=(B, NTL + 1),
        in_specs=_stream_specs(TM, NTL, ctx_block) + [
            _mod_spec(NTL),
            _const_spec((1, D)),
            _layer_spec((D, AB_IN), e),
        ],
        out_specs=pl.BlockSpec((None, TM, AB_IN), lambda b, t: (b, t, 0)),
        out_shape=jax.ShapeDtypeStruct((B, T, AB_IN), F32),
        compiler_params=_cparams(("arbitrary", "arbitrary")),
        name="even_in_proj",
    )(h_lat, h_ctx, mod_l, g, w_all)


def _s5_kernel(uf_ref, wf_ref, cf_ref, arf_ref, aif_ref, ur_ref, wr_ref, cr_ref, arr_ref, air_ref,
               yf_ref, yr_ref, uxf_ref, yxf_ref, buf_ref, hf_ref, uxr_ref, yxr_ref, bur_ref, hr_ref):
    @pl.when(pl.program_id(1) == 0)
    def _():
        hf_ref[...] = jnp.zeros_like(hf_ref)
        hr_ref[...] = jnp.zeros_like(hr_ref)

    n_sb = TS // S5_SB
    dirs = (
        (False, uf_ref, wf_ref, cf_ref, arf_ref[...], aif_ref[...], yf_ref, uxf_ref, yxf_ref, buf_ref, hf_ref),
        (True, ur_ref, wr_ref, cr_ref, arr_ref[...], air_ref[...], yr_ref, uxr_ref, yxr_ref, bur_ref, hr_ref),
    )

    def rows_of(sb):
        return slice(sb * S5_SB * B, (sb + 1) * S5_SB * B)

    def sub_block(rev, n):
        return n_sb - 1 - n if rev else n

    def project(d, n):
        rev, _, w_ref, _, _, _, _, ux_ref, _, bu_ref, _ = d
        rows = rows_of(sub_block(rev, n))
        bu_ref[rows, :] = _dot(ux_ref[rows, :].astype(BF16), w_ref[...])

    for _, u_ref, _, _, _, _, _, ux_ref, _, _, _ in dirs:
        for b in range(B):
            ux_ref[pl.ds(b, TS, stride=B), :] = u_ref[b]
    state = [(d[10][0], d[10][1]) for d in dirs]
    for d in dirs:
        project(d, 0)
    for n in range(n_sb):
        if n + 1 < n_sb:
            for d in dirs:
                project(d, n + 1)
        for k in range(S5_SB):
            for i, (rev, _, _, _, ar, ai, _, _, _, bu_ref, _) in enumerate(dirs):
                t = S5_SB - 1 - k if rev else k
                r0 = rows_of(sub_block(rev, n)).start + t * B
                r = slice(r0, r0 + B)
                hr, hi = state[i]
                nr = ar * hr - ai * hi + bu_ref[r, 0:S5_LC]
                ni = ar * hi + ai * hr + bu_ref[r, S5_LC:2 * S5_LC]
                bu_ref[r, 0:S5_LC] = nr
                bu_ref[r, S5_LC:2 * S5_LC] = ni
                state[i] = (nr, ni)
        for rev, _, _, c_ref, _, _, _, _, yx_ref, bu_ref, _ in dirs:
            rows = rows_of(sub_block(rev, n))
            yx_ref[rows, :] = _dot(bu_ref[rows, :].astype(BF16), c_ref[...])
    for i, (_, _, _, _, _, _, y_ref, _, yx_ref, _, h_ref) in enumerate(dirs):
        h_ref[0], h_ref[1] = state[i]
        for b in range(B):
            y_ref[b] = yx_ref[pl.ds(b, TS, stride=B), :]


def _seq_tile(d, i):
    return jnp.where(i == 0, NTS - 1, jnp.where(d == 0, i - 1, NTS - 1 - i))


def _s5_scan(proj, w_in, w_out, a_r, a_i):
    def operands(d):
        par = lambda *shape: pl.BlockSpec((None, None) + shape, lambda c, i: (d, c, 0, 0))
        return [pl.BlockSpec((B, TS, S5_UC), lambda c, i: (0, _seq_tile(d, i), c)),
                par(S5_UC, 2 * S5_LC), par(2 * S5_LC, S5_UC), par(B, S5_LC), par(B, S5_LC)]

    scratch = [pltpu.VMEM((TS * B, S5_UC), F32), pltpu.VMEM((TS * B, S5_UC), F32),
               pltpu.VMEM((TS * B, 2 * S5_LC), F32), pltpu.VMEM((2, B, S5_LC), F32)]
    out = jax.ShapeDtypeStruct((B, T, S5_W), F32)
    return pl.pallas_call(
        _s5_kernel,
        grid=(S5_NC, NTS),
        in_specs=operands(0) + operands(1),
        out_specs=[pl.BlockSpec((B, TS, S5_UC), lambda c, i, d=d: (0, _seq_tile(d, i), c)) for d in (0, 1)],
        out_shape=[out, out],
        scratch_shapes=scratch + scratch,
        compiler_params=_cparams(("arbitrary", "arbitrary")),
        name="s5_scan",
    )(proj, w_in, w_out, a_r, a_i, proj, w_in, w_out, a_r, a_i)


def _s5_params(lam_re, lam_im, log_step, b_re, b_im, c_re, c_im):
    lr = jnp.minimum(lam_re.astype(F32), -1e-4)
    li = lam_im.astype(F32)
    step = jnp.exp(log_step.astype(F32))[..., None]
    mag = jnp.exp(lr * step)
    a_r = mag * jnp.cos(li * step)
    a_i = mag * jnp.sin(li * step)
    den = lr * lr + li * li
    coef_r = ((a_r - 1) * lr + a_i * li) / den
    coef_i = (a_i * lr - (a_r - 1) * li) / den
    br = b_re.astype(F32)
    bi = b_im.astype(F32)
    bb_r = coef_r[..., None] * br - coef_i[..., None] * bi
    bb_i = coef_r[..., None] * bi + coef_i[..., None] * br
    gpc = S5_LC // S5_STATE
    eye = jnp.eye(gpc, dtype=F32)

    def in_blocks(bb):
        bb = bb.reshape(2, S5_NC, gpc, S5_STATE, S5_GROUP)
        return jnp.einsum('dngpc,gh->dngchp', bb, eye).reshape(2, S5_NC, S5_UC, S5_LC)

    def out_blocks(cc):
        cc = cc.reshape(2, S5_NC, gpc, S5_GROUP, S5_STATE)
        return jnp.einsum('dngcp,gh->dngphc', cc, eye).reshape(2, S5_NC, S5_LC, S5_UC)

    w_in = jnp.concatenate([in_blocks(bb_r), in_blocks(bb_i)], axis=-1).astype(BF16)
    w_out = jnp.concatenate([out_blocks(c_re.astype(F32)), -out_blocks(c_im.astype(F32))], axis=-2).astype(BF16)

    def lanes(a):
        return jnp.broadcast_to(a.reshape(2, S5_NC, 1, S5_LC), (2, S5_NC, B, S5_LC))

    return w_in, w_out, lanes(a_r), lanes(a_i)


HG_HALF = HG_C // 2
HG_LEVELS = tuple(2 ** e for e in range(int(math.log2(HG_HALF))))


def _hgrn_maps(rev):
    pos = np.arange(HG_C)
    ph = np.arange(HG_HALF)
    if rev:
        pos = HG_C - 1 - pos
        ph = HG_HALF - 1 - ph
    tri = (pos[None, :] <= pos[:, None]).astype(np.float32)
    pt, ps = ph[:, None], ph[None, :]
    lv = np.full((HG_HALF, HG_HALF), -1, np.int32)
    lv[pt == ps] = len(HG_LEVELS)
    for e, m in enumerate(HG_LEVELS):
        x, y = pt // m, ps // m
        lv[(x == y + 1) & (x % 2 == 1)] = e
    return jnp.asarray(tri, BF16), jnp.asarray(lv)


def _hgrn_chunk(q_ref, f_ref, v_ref, lb, tri_ref, lv_ref, o_ref, s_ref, rev):
    c = HG_C
    fl = f_ref[...]
    qb = _silu(q_ref[...]).astype(BF16)
    v = v_ref[...]
    x2 = fl * LOG2E
    a = jnp.exp2(-jnp.abs(x2))
    s1 = 1.0 + a
    r = 1.0 / s1
    kb = ((1.0 - lb) * jnp.where(x2 >= 0.0, a * r, r)).astype(BF16)
    t2 = jnp.exp2(jnp.minimum(-x2, MAX_EXP_ARG * LOG2E))
    lf = jnp.minimum(x2, 0.0) - jnp.log2(s1) + jnp.log2(1.0 + lb * t2)

    row = lax.broadcasted_iota(jnp.int32, (c, 1), 0)
    pos = (c - 1 - row) if rev else row
    first, second = (slice(HG_HALF, c), slice(0, HG_HALF)) if rev else (slice(0, HG_HALF), slice(HG_HALF, c))
    halves = (first, second)

    def from_earlier(x, j):
        return pltpu.roll(x, (c - j) if rev else j, 0)

    def from_later(x, j):
        return pltpu.roll(x, j if rev else (c - j), 0)

    def head(x, h, rows):
        return x[rows, h * HG_DIM:(h + 1) * HG_DIM]

    hi = lf.astype(BF16)
    r1 = lf - hi.astype(F32)
    mid = r1.astype(BF16)
    lo = (r1 - mid.astype(F32)).astype(BF16)
    tri = tri_ref[...]
    cum = _dot(tri, hi) + _dot(tri, mid) + _dot(tri, lo)

    lv = lv_ref[...]
    diag = len(HG_LEVELS)
    scores = [[jnp.where(lv == diag, _dot_nt(head(qb, h, r), head(kb, h, r)), 0.0) for r in halves]
              for h in range(HG_HEADS)]

    g_end = cum
    for e, m in enumerate(HG_LEVELS):
        qm = qb * jnp.exp2(cum - from_earlier(g_end, m)).astype(BF16)
        km = kb * jnp.exp2(g_end - cum).astype(BF16)
        for h in range(HG_HEADS):
            for j, r in enumerate(halves):
                scores[h][j] = jnp.where(lv == e, _dot_nt(head(qm, h, r), head(km, h, r)), scores[h][j])
        g_end = jnp.where((pos & m) != 0, g_end, from_later(g_end, m))
    qm = qb * jnp.exp2(cum - from_earlier(g_end, HG_HALF)).astype(BF16)
    km = kb * jnp.exp2(g_end - cum).astype(BF16)
    cross = [_dot_nt(head(qm, h, second), head(km, h, first)) for h in range(HG_HEADS)]
    g_end = jnp.where((pos & HG_HALF) != 0, g_end, from_later(g_end, HG_HALF))

    q_in = qb * jnp.exp2(cum).astype(BF16)
    k_out = kb * jnp.exp2(g_end - cum).astype(BF16)
    decay = jnp.exp2(g_end[0:1, :])
    vb = v.astype(BF16)
    for h in range(HG_HEADS):
        sl = slice(h * HG_DIM, (h + 1) * HG_DIM)
        st = s_ref[h]
        inter = _dot_nt(q_in[:, sl], st.astype(BF16))
        v_first, v_second = head(vb, h, first), head(vb, h, second)
        o_ref[first, sl] = inter[first] + _dot(scores[h][0].astype(BF16), v_first)
        o_ref[second, sl] = (inter[second] + _dot(cross[h].astype(BF16), v_first)
                             + _dot(scores[h][1].astype(BF16), v_second))
        s_ref[h] = decay[:, sl] * st + _dot(v[:, sl].T.astype(BF16), k_out[:, sl])


def _hgrn_kernel(qf_ref, ff_ref, vf_ref, qr_ref, fr_ref, vr_ref, lb_ref, trif_ref, lvf_ref, trir_ref, lvr_ref,
                 of_ref, or_ref, sf_ref, sr_ref):
    @pl.when(pl.program_id(1) == 0)
    def _():
        sf_ref[...] = jnp.zeros_like(sf_ref)
        sr_ref[...] = jnp.zeros_like(sr_ref)

    lb = lb_ref[...]
    _hgrn_chunk(qf_ref, ff_ref, vf_ref, lb, trif_ref, lvf_ref, of_ref, sf_ref, rev=False)
    _hgrn_chunk(qr_ref, fr_ref, vr_ref, lb, trir_ref, lvr_ref, or_ref, sr_ref, rev=True)


def _hgrn_scan(hg, lb):
    maps = [_hgrn_maps(rev) for rev in (False, True)]
    blk = lambda d, col: pl.BlockSpec((None, HG_C, HG_W), lambda b, i: (b, _seq_tile(d, i), col))
    state = pltpu.VMEM((HG_HEADS, HG_DIM, HG_DIM), F32)
    out = jax.ShapeDtypeStruct((B, T, HG_W), F32)
    return pl.pallas_call(
        _hgrn_kernel,
        grid=(B, NTS),
        in_specs=[blk(0, 1), blk(0, 2), blk(0, 4), blk(1, 1), blk(1, 3), blk(1, 4),
                  _const_spec((1, HG_W)),
                  _const_spec((HG_C, HG_C)), _const_spec((HG_HALF, HG_HALF)),
                  _const_spec((HG_C, HG_C)), _const_spec((HG_HALF, HG_HALF))],
        out_specs=[blk(0, 0), blk(1, 0)],
        out_shape=[out, out],
        scratch_shapes=[state, state],
        compiler_params=_cparams(("arbitrary", "arbitrary")),
        name="hgrn_scan",
    )(hg, hg, hg, hg, hg, hg, lb, *maps[0], *maps[1])


def _ffn_tail(mix_out, h, mod_ref, g_ref, w_in_ref, w_out_ref):
    h1 = h + mod_ref[2:3, :] * mix_out
    a = _norm_mod(h1, g_ref[...], mod_ref[3:4, :], mod_ref[4:5, :]).astype(BF16)

    def gate_up(c):
        lo, hi = FFN_BOUNDS[c], FFN_BOUNDS[c + 1]
        return _dot(a, w_in_ref[:, lo:hi]), _dot(a, w_in_ref[:, FFN_H + lo:FFN_H + hi])

    ffn = None
    ahead = gate_up(0)
    for c in range(len(FFN_BOUNDS) - 1):
        gate, up = ahead
        if c + 2 < len(FFN_BOUNDS):
            ahead = gate_up(c + 1)
        part = _dot((_silu(gate) * up).astype(BF16), w_out_ref[FFN_BOUNDS[c]:FFN_BOUNDS[c + 1], :])
        ffn = part if ffn is None else ffn + part
    return h1 + mod_ref[5:6, :] * ffn


def _gelu_tanh(x):
    return 0.5 * x * (1.0 + jnp.tanh(math.sqrt(2.0 / math.pi) * (x + 0.044715 * (x * x * x))))


def _post_even_kernel(yf_ref, yb_ref, u_ref, of_ref, ob_ref, gate_ref, hl_ref, hc_ref, mod_ref, g_ref,
                      dsk_ref, gw_ref, gb_ref, on_ref, wo_ref, w_in_ref, w_out_ref, o_ref):
    def mixer_out(r):
        y = yf_ref[r] + yb_ref[r] + u_ref[r] * dsk_ref[...]
        z = _gelu_tanh(y)
        s5 = z * jax.nn.sigmoid(_dot(z.astype(BF16), gw_ref[...]) + gb_ref[...])
        mix = _dot(s5.astype(BF16), wo_ref[0:S5_W, :])
        o = of_ref[r] + ob_ref[r]
        gate = _silu(gate_ref[r])
        for h in range(HG_HEADS):
            sl = slice(h * HG_DIM, (h + 1) * HG_DIM)
            oh = o[:, sl]
            ms = jnp.mean(oh * oh, axis=-1, keepdims=True)
            hn = oh * lax.rsqrt(ms + EPS) * on_ref[...] * gate[:, sl]
            mix = mix + _dot(hn.astype(BF16), wo_ref[S5_W + h * HG_DIM:S5_W + (h + 1) * HG_DIM, :])
        return mix

    is_ctx = pl.program_id(1) == NTS - 1
    halves = (slice(0, TS // 2), slice(TS // 2, TS))
    mixes = [mixer_out(r) for r in halves]
    for mix, r in zip(mixes, halves):
        h = jnp.where(is_ctx, hc_ref[r], hl_ref[r])
        o_ref[r] = _ffn_tail(mix, h, mod_ref, g_ref, w_in_ref, w_out_ref)


def _post_even(y_f, y_b, o_f, o_b, proj, h_lat, h_ctx, ctx_block, mod_l, g, dsk, gw, gb, on, wo, w_in, w_out, l):
    tok = lambda b, t: (b, t, 0)
    return pl.pallas_call(
        _post_even_kernel,
        grid=(B, NTS),
        in_specs=[
            pl.BlockSpec((None, TS, S5_W), tok),
            pl.BlockSpec((None, TS, S5_W), tok),
            pl.BlockSpec((None, TS, S5_W), tok),
            pl.BlockSpec((None, TS, HG_W), tok),
            pl.BlockSpec((None, TS, HG_W), tok),
            pl.BlockSpec((None, TS, HG_W), lambda b, t: (b, t, 5)),
        ] + _stream_specs(TS, NTS - 1, ctx_block) + [
            _mod_spec(NTS - 1),
            _const_spec((1, D)),
            _const_spec((1, S5_W)),
            _const_spec((S5_W, S5_W)),
            _const_spec((1, S5_W)),
            _const_spec((1, HG_DIM)),
            _layer_spec((D, D), l),
            _layer_spec((D, 2 * FFN_H), l),
            _layer_spec((FFN_H, D), l),
        ],
        out_specs=pl.BlockSpec((None, TS, D), tok),
        out_shape=jax.ShapeDtypeStruct((B, T, D), F32),
        compiler_params=_cparams(("arbitrary", "arbitrary")),
        name="post_even",
    )(y_f, y_b, proj, o_f, o_b, proj, h_lat, h_ctx, mod_l, g, dsk, gw, gb, on, wo, w_in, w_out)


def _post_odd_kernel(mx_ref, h_ref, mod_ref, g_ref, wo_ref, w_in_ref, w_out_ref, o_ref):
    halves = (slice(0, TS // 2), slice(TS // 2, TS))
    mixes = [_dot(mx_ref[r], wo_ref[...]) for r in halves]
    for mix, r in zip(mixes, halves):
        o_ref[r] = _ffn_tail(mix, h_ref[r], mod_ref, g_ref, w_in_ref, w_out_ref)


def _post_odd(mx, hs, mod_l, g, wo, w_in, w_out, l, latent_only):
    nt = NTS - 1 if latent_only else NTS
    tok = lambda b, t: (b, t, 0)
    return pl.pallas_call(
        _post_odd_kernel,
        grid=(B, nt),
        in_specs=[
            pl.BlockSpec((None, TS, D), tok),
            pl.BlockSpec((None, TS, D), tok),
            _mod_spec(NTS - 1),
            _const_spec((1, D)),
            _layer_spec((D, D), l),
            _layer_spec((D, 2 * FFN_H), l),
            _layer_spec((FFN_H, D), l),
        ],
        out_specs=pl.BlockSpec((None, TS, D), tok),
        out_shape=jax.ShapeDtypeStruct((B, nt * TS, D), F32),
        compiler_params=_cparams(("arbitrary", "arbitrary")),
        name="post_odd",
    )(mx, hs, mod_l, g, wo, w_in, w_out)


def _group_rms(xs, r):
    ms = _dot((xs * xs).astype(BF16), r)
    return xs * lax.rsqrt(ms + EPS)


def _rope(xs, cos, sin_signed, first_half):
    rot = jnp.where(first_half, pltpu.roll(xs, LANE - ROPE_DIM // 4, 1), pltpu.roll(xs, ROPE_DIM // 4, 1))
    return xs * cos + rot * sin_signed


def _odd_in_kernel(x_ref, mod_ref, g_ref, w_ref, rope_ref, gv_ref, qan_ref, kvan_ref, wuq_ref, wukv_ref,
                   rpair_ref, rq_ref, rfull_ref, rhalf_ref,
                   dq_ref, k1_ref, k2_ref, dv_ref, mq_ref, mk_ref, mv_ref, *, mla_scale):
    lane = lax.broadcasted_iota(jnp.int32, (1, LANE), 1)
    first_half = (lane % (ROPE_DIM // 2)) < (ROPE_DIM // 4)
    low = lane < DIFF_DIM
    rpair, rq, rfull, rhalf = rpair_ref[...], rq_ref[...], rfull_ref[...], rhalf_ref[...]
    g_dq, g_dk = gv_ref[0:1, :], gv_ref[1:2, :]
    g_qn, g_qr, g_kn, g_kr = gv_ref[2:3, :], gv_ref[3:4, :], gv_ref[4:5, :], gv_ref[5:6, :]
    diff_scale = DIFF_DIM ** -0.5 * LOG2E
    o_cq = 3 * DIFF_W
    o_ckv = o_cq + MLA_Q_RANK
    o_kr = o_ckv + MLA_KV_RANK

    def body(n):
        rows = slice(0, n)
        a = _norm_mod(x_ref[rows], g_ref[...], mod_ref[0:1, :], mod_ref[1:2, :]).astype(BF16)
        cos_d, sin_d, cos_m, sin_m = rope_ref[0, rows], rope_ref[1, rows], rope_ref[2, rows], rope_ref[3, rows]
        ones = jnp.ones((n, LANE), BF16)

        def proj(c0, c1):
            return _dot(a, w_ref[:, c0:c1])

        def diff_pair(pq, pk, h2):
            qn = _group_rms(pq, rpair)
            kn = _group_rms(pk, rpair)
            for j in range(2):
                sl = slice((h2 + j) * LANE, (h2 + j + 1) * LANE)
                half = slice(j * LANE, (j + 1) * LANE)
                qh = _rope(qn[:, half] * g_dq, cos_d, sin_d, first_half)
                dq_ref[rows, sl] = (qh * diff_scale).astype(BF16)
                kh = _rope(kn[:, half] * g_dk, cos_d, sin_d, first_half)
                k1_ref[rows, sl] = jnp.where(low, kh, 0.0).astype(BF16)
                k2_ref[rows, sl] = jnp.where(low, 0.0, kh).astype(BF16)

        def diff_values(pv):
            for h in range(DIFF_HEADS):
                dv_ref[rows, 2 * h * LANE:(2 * h + 1) * LANE] = pv[:, h * LANE:(h + 1) * LANE].astype(BF16)
                dv_ref[rows, (2 * h + 1) * LANE:(2 * h + 2) * LANE] = ones

        def latent_heads(pm):
            cq = pm[:, 0:MLA_Q_RANK]
            cqn = cq * lax.rsqrt(jnp.mean(cq * cq, axis=-1, keepdims=True) + EPS) * qan_ref[...]
            q = _dot(cqn.astype(BF16), wuq_ref[...])
            ckv = pm[:, MLA_Q_RANK:MLA_Q_RANK + MLA_KV_RANK]
            ckvn = ckv * lax.rsqrt(jnp.mean(ckv * ckv, axis=-1, keepdims=True) + EPS) * kvan_ref[...]
            kv = _dot(ckvn.astype(BF16), wukv_ref[...])
            k_rope = _rope(_group_rms(pm[:, MLA_Q_RANK + MLA_KV_RANK:], rhalf) * g_kr,
                           cos_m, sin_m, first_half).astype(BF16)
            for h in range(MLA_HEADS):
                o = 2 * LANE * h
                qn = _group_rms(q[:, o:o + 2 * LANE], rq)
                q_nope = qn[:, :LANE] * g_qn
                q_rope = _rope(qn[:, LANE:] * g_qr, cos_m, sin_m, first_half)
                mq_ref[rows, o:o + LANE] = (q_nope * mla_scale).astype(BF16)
                mq_ref[rows, o + LANE:o + 2 * LANE] = (q_rope * mla_scale).astype(BF16)
                mk_ref[rows, o:o + LANE] = (_group_rms(kv[:, o:o + LANE], rfull) * g_kn).astype(BF16)
                mk_ref[rows, o + LANE:o + 2 * LANE] = k_rope
                mv_ref[rows, o:o + LANE] = kv[:, o + LANE:o + 2 * LANE].astype(BF16)
                mv_ref[rows, o + LANE:o + 2 * LANE] = ones

        pair = 2 * LANE
        pm = proj(o_cq, CD_PAD)
        pq0, pk0 = proj(0, pair), proj(DIFF_W, DIFF_W + pair)
        latent_heads(pm)
        pq1, pk1 = proj(pair, 2 * pair), proj(DIFF_W + pair, DIFF_W + 2 * pair)
        diff_pair(pq0, pk0, 0)
        pv = proj(2 * DIFF_W, 3 * DIFF_W)
        diff_pair(pq1, pk1, 2)
        diff_values(pv)

    _for_tile_rows(True, body)


def _odd_in(hs, mod_l, g, w_all, o, rope, gv, qan, kvan, wuq, wukv, rmats):
    tok = lambda b, t: (b, t, 0)
    sds = lambda n: jax.ShapeDtypeStruct((B, T, n), BF16)
    wide = 2 * LANE * MLA_HEADS
    widths = (DIFF_W, DIFF_W, DIFF_W, wide, wide, wide, wide)
    return pl.pallas_call(
        functools.partial(_odd_in_kernel, mla_scale=(MLA_NOPE + MLA_ROPE) ** -0.5 * LOG2E),
        grid=(B, NTL + 1),
        in_specs=[
            pl.BlockSpec((None, TM, D), tok),
            _mod_spec(NTL),
            _const_spec((1, D)),
            _layer_spec((D, CD_PAD), o),
            pl.BlockSpec((4, TM, LANE), lambda b, t: (0, t, 0)),
            _const_spec((SUBLANE, LANE)),
            _const_spec((1, MLA_Q_RANK)),
            _const_spec((1, MLA_KV_RANK)),
            _const_spec((MLA_Q_RANK, 2 * LANE * MLA_HEADS)),
            _const_spec((MLA_KV_RANK, 2 * LANE * MLA_HEADS)),
            _const_spec((2 * LANE, 2 * LANE)),
            _const_spec((2 * LANE, 2 * LANE)),
            _const_spec((LANE, LANE)),
            _const_spec((LANE, LANE)),
        ],
        out_specs=[pl.BlockSpec((None, TM, n), tok) for n in widths],
        out_shape=[sds(n) for n in widths],
        compiler_params=_cparams(("arbitrary", "arbitrary")),
        name="odd_in_proj",
    )(hs, mod_l, g, w_all, rope, gv, qan, kvan, wuq, wukv, *rmats)


def _exp2_shifted(s):
    return jnp.exp2(s - jnp.max(s, axis=-1, keepdims=True)).astype(BF16)


def _pv_normalised(e, v_aug):
    o = _dot(e, v_aug)
    return o[:, :LANE] / o[:, LANE:]


def _attn_kernel(lam_ref, dq_ref, k1_ref, k2_ref, dv_ref, mq_ref, mk_ref, mv_ref, sub_ref, o_ref,
                 *, out_scale, has_ctx):
    lam = lam_ref[0, 0]

    def body(keys):
        units = []
        for h in range(DIFF_HEADS):
            sl = slice(h * LANE, (h + 1) * LANE)
            sv = slice(2 * LANE * h, 2 * LANE * (h + 1))
            units += [(dq_ref, sl, k1_ref, dv_ref, sv), (dq_ref, sl, k2_ref, dv_ref, sv)]
        for h in range(MLA_HEADS):
            sq = slice(2 * LANE * h, 2 * LANE * (h + 1))
            units.append((mq_ref, sq, mk_ref, mv_ref, sq))

        def scores(u):
            q_ref, qs, k_ref, _, _ = u
            return _dot_nt(q_ref[:, qs], k_ref[keys, qs])

        outs = []
        ahead = [scores(u) for u in units[:ATTN_AHEAD]]
        for i, u in enumerate(units):
            s = ahead.pop(0)
            if i + ATTN_AHEAD < len(units):
                ahead.append(scores(units[i + ATTN_AHEAD]))
            outs.append(_pv_normalised(_exp2_shifted(s), u[3][keys, u[4]]))
        for h in range(DIFF_HEADS):
            o = outs[2 * h] - lam * outs[2 * h + 1]
            o = o * lax.rsqrt(jnp.mean(o * o, axis=-1, keepdims=True) + EPS) * sub_ref[...] * out_scale
            o_ref[:, h * LANE:(h + 1) * LANE] = o.astype(BF16)
        for h in range(MLA_HEADS):
            o_ref[:, DIFF_W + h * LANE:DIFF_W + (h + 1) * LANE] = outs[2 * DIFF_HEADS + h].astype(BF16)

    if has_ctx:
        t = pl.program_id(1)
        pl.when(t == 0)(lambda: body(slice(SEQ, T)))
        pl.when(t > 0)(lambda: body(slice(0, T)))
    else:
        body(slice(0, T))


def _attention(lam, dq, k1, k2, dv, mq, mk, mv, sub, out_scale, latent_only):
    has_ctx = not latent_only
    nt = NTS if has_ctx else NTS - 1
    wide = 2 * LANE * MLA_HEADS
    tile = (lambda t: (t + NTS - 1) % NTS) if has_ctx else (lambda t: t)
    qspec = lambda n: pl.BlockSpec((None, TS, n), lambda b, t: (b, tile(t), 0))
    kspec = lambda n: pl.BlockSpec((None, T, n), lambda b, t: (b, 0, 0))
    return pl.pallas_call(
        functools.partial(_attn_kernel, out_scale=out_scale, has_ctx=has_ctx),
        grid=(B, nt),
        in_specs=[
            pl.BlockSpec(memory_space=pltpu.SMEM),
            qspec(DIFF_W), kspec(DIFF_W), kspec(DIFF_W), kspec(wide),
            qspec(wide), kspec(wide), kspec(wide),
            _const_spec((1, LANE)),
        ],
        out_specs=qspec(D),
        out_shape=jax.ShapeDtypeStruct((B, nt * TS, D), BF16),
        compiler_params=_cparams(("arbitrary", "arbitrary")),
        name="attention",
    )(lam, dq, k1, k2, dv, mq, mk, mv, sub)


def _rope_tables():
    n_tok = SEQ
    rows = jnp.repeat(jnp.arange(n_tok // GRID_W, dtype=jnp.int32), GRID_W)
    cols = jnp.tile(jnp.arange(GRID_W, dtype=jnp.int32), n_tok // GRID_W)
    n_freq = ROPE_DIM // 4
    inv = jnp.power(ROPE_BASE, -jnp.arange(n_freq, dtype=F32) / n_freq)
    ang_r = rows.astype(F32)[:, None] * inv
    ang_c = cols.astype(F32)[:, None] * inv
    ang = jnp.concatenate([ang_r, ang_r, ang_c, ang_c], axis=-1)
    sign = jnp.where((jnp.arange(ROPE_DIM) % (ROPE_DIM // 2)) < n_freq, -1.0, 1.0).astype(F32)
    cos = jnp.concatenate([jnp.cos(ang), jnp.ones((CTX, ROPE_DIM), F32)], axis=0)
    sin = jnp.concatenate([jnp.sin(ang) * sign, jnp.zeros((CTX, ROPE_DIM), F32)], axis=0)
    one, zero = jnp.ones_like(cos), jnp.zeros_like(sin)
    cat = lambda a, b: jnp.concatenate([a, b], axis=-1)
    return jnp.stack([cat(cos, cos), cat(sin, sin), cat(cos, one), cat(sin, zero)])


def _averaging_mats():
    idx = np.arange(2 * LANE)
    same64 = (idx[:, None] // DIFF_DIM) == (idx[None, :] // DIFF_DIM)
    rpair = np.where(same64, 1.0 / DIFF_DIM, 0.0)
    rfull = np.full((LANE, LANE), 1.0 / LANE)
    il = np.arange(LANE)
    rhalf = np.where((il[:, None] < MLA_ROPE) & (il[None, :] < MLA_ROPE), 1.0 / MLA_ROPE, 0.0)
    rq = np.zeros((2 * LANE, 2 * LANE))
    rq[:LANE, :LANE] = rfull
    rq[LANE:, LANE:] = rhalf
    return tuple(jnp.asarray(m, BF16) for m in (rpair, rq, rfull, rhalf))


def kernel(x, c, ctx, c_ctx, ada_w, ada_b, norm_mix, norm_ffn, w_out, ffn_w_in, ffn_w_out, ab_w_in, s5_lambda_re, s5_lambda_im, s5_log_step, s5_b_re, s5_b_im, s5_c_re, s5_c_im, s5_d, s5_glu_w, s5_glu_b, hgrn_lb_logits, hgrn_out_norm, cd_w_in, diff_lambda, diff_qk_norm, diff_subln, mla_q_a_norm, mla_kv_a_norm, mla_w_uq, mla_w_ukv, mla_nope_norm, mla_rope_norm):
    assert x.shape == (B, SEQ, D) and ctx.shape == (B, CTX, D)
    mod = _modulation(c, c_ctx, ada_w, ada_b)
    wo_all = w_out.astype(BF16)
    w1_all = ffn_w_in.astype(BF16)
    w2_all = ffn_w_out.astype(BF16)
    wab_all = ab_w_in.astype(BF16)
    wcd_all = jnp.pad(cd_w_in, ((0, 0), (0, 0), (0, CD_PAD - CD_IN))).astype(BF16)
    h_lat, h_ctx, ctx_block = x, ctx, 0
    rope = _rope_tables()
    rmats = _averaging_mats()

    lb_p = jax.nn.softmax(hgrn_lb_logits.astype(F32), axis=0)
    lower_bounds = jnp.cumsum(lb_p, axis=0) - lb_p[0:1]

    for l in range(DEPTH):
        last = l == DEPTH - 1
        g_mix = norm_mix[l].reshape(1, D)
        g_ffn = norm_ffn[l].reshape(1, D)
        if l % 2 == 0:
            e = l // 2
            proj = _even_in(h_lat, h_ctx, ctx_block, mod[l], g_mix, wab_all, e)
            s5w_in, s5w_out, a_r, a_i = _s5_params(s5_lambda_re[e], s5_lambda_im[e], s5_log_step[e],
                                                   s5_b_re[e], s5_b_im[e], s5_c_re[e], s5_c_im[e])
            y_f, y_b = _s5_scan(proj, s5w_in, s5w_out, a_r, a_i)
            lb = lower_bounds[e].reshape(1, HG_W)
            o_f, o_b = _hgrn_scan(proj, lb)
            hs = _post_even(y_f, y_b, o_f, o_b, proj, h_lat, h_ctx, ctx_block, mod[l], g_ffn,
                            s5_d[e].reshape(1, S5_W), s5_glu_w[e].astype(BF16), s5_glu_b[e].reshape(1, S5_W),
                            hgrn_out_norm[e].reshape(1, HG_DIM), wo_all, w1_all, w2_all, l)
        else:
            o = l // 2
            lam_init = 0.8 - 0.6 * math.exp(-0.3 * l)
            lv = diff_lambda[o].astype(F32)
            lam = (jnp.exp(jnp.sum(lv[0] * lv[1])) - jnp.exp(jnp.sum(lv[2] * lv[3])) + lam_init).reshape(1, 1)
            wuq = mla_w_uq[o].reshape(MLA_Q_RANK, MLA_HEADS, MLA_NOPE + MLA_ROPE)
            wuq = jnp.pad(wuq, ((0, 0), (0, 0), (0, 2 * LANE - MLA_NOPE - MLA_ROPE)))
            wuq = wuq.reshape(MLA_Q_RANK, 2 * LANE * MLA_HEADS).astype(BF16)
            pad_r = lambda v: jnp.pad(v, (0, LANE - MLA_ROPE))
            gv = jnp.zeros((SUBLANE, LANE), F32)
            gv = gv.at[0].set(jnp.tile(diff_qk_norm[o, 0], 2)).at[1].set(jnp.tile(diff_qk_norm[o, 1], 2))
            gv = gv.at[2].set(mla_nope_norm[o, 0]).at[3].set(pad_r(mla_rope_norm[o, 0]))
            gv = gv.at[4].set(mla_nope_norm[o, 1]).at[5].set(pad_r(mla_rope_norm[o, 1]))
            parts = _odd_in(hs, mod[l], g_mix, wcd_all, o, rope, gv, mla_q_a_norm[o].reshape(1, MLA_Q_RANK),
                            mla_kv_a_norm[o].reshape(1, MLA_KV_RANK), wuq, mla_w_ukv[o].astype(BF16), rmats)
            mx = _attention(lam, *parts, diff_subln[o].reshape(1, LANE), 1.0 - lam_init, latent_only=last)
            hs = _post_odd(mx, hs, mod[l], g_ffn, wo_all, w1_all, w2_all, l, latent_only=last)
        h_lat, h_ctx, ctx_block = hs, hs, SEQ // CTX
    return hs
```

```python
import functools
import math

import numpy as np
import jax
import jax.numpy as jnp
from jax import lax
from jax.experimental import pallas as pl
from jax.experimental.pallas import tpu as pltpu

F32 = jnp.float32
BF16 = jnp.bfloat16

D = 1024
B = 8
SEQ = 2048
CTX = 256
T = CTX + SEQ
DEPTH = 4
GRID_W = 64
FFN_H = ((8 * D + 3 * 256 - 1) // (3 * 256)) * 256
S5_W = D // 2
S5_GROUP = 16
S5_GROUPS = S5_W // S5_GROUP
S5_STATE = 64
HG_HEADS = 4
HG_DIM = D // 8
HG_W = HG_HEADS * HG_DIM
MAX_EXP_ARG = 60.0
DIFF_HEADS = 4
DIFF_DIM = D // 16
DIFF_W = DIFF_HEADS * 2 * DIFF_DIM
MLA_HEADS = 4
MLA_NOPE = D // 8
MLA_ROPE = D // 16
MLA_V = D // 8
MLA_Q_RANK = 3 * D // 8
MLA_KV_RANK = D // 4
ROPE_DIM = D // 16
ROPE_BASE = 10000.0
EPS = 1e-6
AB_IN = S5_W + 5 * HG_W
CD_IN = 3 * DIFF_W + MLA_Q_RANK + MLA_KV_RANK + MLA_ROPE
CD_PAD = CD_IN + 64

LANE = 128
SUBLANE = 8
TS = CTX
NTS = T // TS
TM = 512
NTL = SEQ // TM
MOD_ROWS = 16
CTX_ROW = B
S5_LC = 512
S5_NC = S5_GROUPS * S5_STATE // S5_LC
S5_UC = S5_LC // S5_STATE * S5_GROUP
S5_SB = 16
HG_C = TS
ATTN_AHEAD = 1
FFN_BOUNDS = (0, 768, 1536, 2304, FFN_H)
VMEM_LIMIT = 56 * 1024 * 1024
LOG2E = math.log2(math.e)


def _cparams(sem):
    return pltpu.CompilerParams(dimension_semantics=sem, vmem_limit_bytes=VMEM_LIMIT)


def _const_spec(shape):
    n = len(shape)
    return pl.BlockSpec(shape, lambda *_: (0,) * n, pipeline_mode=pl.Buffered(1))


def _layer_spec(shape, l):
    n = len(shape)
    return pl.BlockSpec((None,) + tuple(shape), lambda *_: (l,) + (0,) * n, pipeline_mode=pl.Buffered(1))


def _stream_specs(rows, n_lat, ctx_block):
    return [pl.BlockSpec((None, rows, D), lambda b, t: (b, jnp.minimum(t, n_lat - 1), 0)),
            pl.BlockSpec((None, CTX, D), lambda b, t: (b, ctx_block, 0))]


def _silu(x):
    return x * jax.nn.sigmoid(x)


def _norm_mod(x, g, shift, scale):
    ms = jnp.mean(x * x, axis=-1, keepdims=True)
    return x * lax.rsqrt(ms + EPS) * g * (1.0 + scale) + shift


def _dot(a, b):
    return jnp.dot(a, b, preferred_element_type=F32)


def _dot_nt(a, b):
    return lax.dot_general(a, b, (((1,), (1,)), ((), ())), preferred_element_type=F32)


def _mod_kernel(s_ref, w_ref, b_ref, o_ref):
    s = _silu(s_ref[...])
    o_ref[...] = _dot(s.astype(BF16), w_ref[...].astype(BF16)) + b_ref[...]


def _modulation(c, c_ctx, ada_w, ada_b):
    s = jnp.zeros((MOD_ROWS, D), F32).at[:B].set(c).at[CTX_ROW].set(c_ctx)
    nb = 1536
    out = pl.pallas_call(
        _mod_kernel,
        grid=(DEPTH, 6 * D // nb),
        in_specs=[
            pl.BlockSpec((MOD_ROWS, D), lambda l, n: (0, 0)),
            pl.BlockSpec((None, D, nb), lambda l, n: (l, 0, n)),
            pl.BlockSpec((None, 1, nb), lambda l, n: (l, 0, n)),
        ],
        out_specs=pl.BlockSpec((None, MOD_ROWS, nb), lambda l, n: (l, 0, n)),
        out_shape=jax.ShapeDtypeStruct((DEPTH, MOD_ROWS, 6 * D), F32),
        compiler_params=_cparams(("arbitrary", "arbitrary")),
        name="adaln_mod",
    )(s, ada_w, ada_b.reshape(DEPTH, 1, 6 * D))
    return out.reshape(DEPTH, MOD_ROWS, 6, D)


def _mod_spec(ctx_tile):
    return pl.BlockSpec((None, 6, D), lambda b, t: (jnp.where(t == ctx_tile, CTX_ROW, b), 0, 0))


def _for_tile_rows(has_ctx, body):
    if has_ctx:
        t = pl.program_id(1)
        pl.when(t < NTL)(lambda: body(TM))
        pl.when(t == NTL)(lambda: body(CTX))
    else:
        body(TM)


def _even_in_kernel(xl_ref, xc_ref, mod_ref, g_ref, w_ref, p_ref):
    def body(n):
        x = xl_ref[...] if n == TM else xc_ref[...]
        a = _norm_mod(x, g_ref[...], mod_ref[0:1, :], mod_ref[1:2, :])
        p_ref[0:n] = _dot(a.astype(BF16), w_ref[...]).astype(BF16)

    _for_tile_rows(True, body)


def _even_in(h_lat, h_ctx, ctx_block, mod_l, g, w_all, e):
    return pl.pallas_call(
        _even_in_kernel,
        grid=(B, NTL + 1),
        in_specs=_stream_specs(TM, NTL, ctx_block) + [
            _mod_spec(NTL),
            _const_spec((1, D)),
            _layer_spec((D, AB_IN), e),
        ],
        out_specs=pl.BlockSpec((None, TM, AB_IN), lambda b, t: (b, t, 0)),
        out_shape=jax.ShapeDtypeStruct((B, T, AB_IN), BF16),
        compiler_params=_cparams(("arbitrary", "arbitrary")),
        name="even_in_proj",
    )(h_lat, h_ctx, mod_l, g, w_all)


def _s5_kernel(uf_ref, wf_ref, cf_ref, arf_ref, aif_ref, ur_ref, wr_ref, cr_ref, arr_ref, air_ref,
               yf_ref, yr_ref, uxf_ref, yxf_ref, buf_ref, hf_ref, uxr_ref, yxr_ref, bur_ref, hr_ref):
    @pl.when(pl.program_id(1) == 0)
    def _():
        hf_ref[...] = jnp.zeros_like(hf_ref)
        hr_ref[...] = jnp.zeros_like(hr_ref)

    n_sb = TS // S5_SB
    dirs = (
        (False, uf_ref, wf_ref, cf_ref, arf_ref[...], aif_ref[...], yf_ref, uxf_ref, yxf_ref, buf_ref, hf_ref),
        (True, ur_ref, wr_ref, cr_ref, arr_ref[...], air_ref[...], yr_ref, uxr_ref, yxr_ref, bur_ref, hr_ref),
    )

    def rows_of(sb):
        return slice(sb * S5_SB * B, (sb + 1) * S5_SB * B)

    def sub_block(rev, n):
        return n_sb - 1 - n if rev else n

    def project(d, n):
        rev, _, w_ref, _, _, _, _, ux_ref, _, bu_ref, _ = d
        rows = rows_of(sub_block(rev, n))
        bu_ref[rows, :] = _dot(ux_ref[rows, :].astype(BF16), w_ref[...])

    for _, u_ref, _, _, _, _, _, ux_ref, _, _, _ in dirs:
        for b in range(B):
            ux_ref[pl.ds(b, TS, stride=B), :] = u_ref[b].astype(F32)
    state = [(d[10][0], d[10][1]) for d in dirs]
    for d in dirs:
        project(d, 0)
    for n in range(n_sb):
        if n + 1 < n_sb:
            for d in dirs:
                project(d, n + 1)
        for k in range(S5_SB):
            for i, (rev, _, _, _, ar, ai, _, _, _, bu_ref, _) in enumerate(dirs):
                t = S5_SB - 1 - k if rev else k
                r0 = rows_of(sub_block(rev, n)).start + t * B
                r = slice(r0, r0 + B)
                hr, hi = state[i]
                nr = ar * hr - ai * hi + bu_ref[r, 0:S5_LC]
                ni = ar * hi + ai * hr + bu_ref[r, S5_LC:2 * S5_LC]
                bu_ref[r, 0:S5_LC] = nr
                bu_ref[r, S5_LC:2 * S5_LC] = ni
                state[i] = (nr, ni)
        for rev, _, _, c_ref, _, _, _, _, yx_ref, bu_ref, _ in dirs:
            rows = rows_of(sub_block(rev, n))
            yx_ref[rows, :] = _dot(bu_ref[rows, :].astype(BF16), c_ref[...])
    for i, (_, _, _, _, _, _, y_ref, _, yx_ref, _, h_ref) in enumerate(dirs):
        h_ref[0], h_ref[1] = state[i]
        for b in range(B):
            y_ref[b] = yx_ref[pl.ds(b, TS, stride=B), :].astype(BF16)


def _seq_tile(d, i):
    return jnp.where(i == 0, NTS - 1, jnp.where(d == 0, i - 1, NTS - 1 - i))


def _s5_scan(proj, w_in, w_out, a_r, a_i):
    def operands(d):
        par = lambda *shape: pl.BlockSpec((None, None) + shape, lambda c, i: (d, c, 0, 0))
        return [pl.BlockSpec((B, TS, S5_UC), lambda c, i: (0, _seq_tile(d, i), c)),
                par(S5_UC, 2 * S5_LC), par(2 * S5_LC, S5_UC), par(B, S5_LC), par(B, S5_LC)]

    scratch = [pltpu.VMEM((TS * B, S5_UC), F32), pltpu.VMEM((TS * B, S5_UC), F32),
               pltpu.VMEM((TS * B, 2 * S5_LC), F32), pltpu.VMEM((2, B, S5_LC), F32)]
    out = jax.ShapeDtypeStruct((B, T, S5_W), BF16)
    return pl.pallas_call(
        _s5_kernel,
        grid=(S5_NC, NTS),
        in_specs=operands(0) + operands(1),
        out_specs=[pl.BlockSpec((B, TS, S5_UC), lambda c, i, d=d: (0, _seq_tile(d, i), c)) for d in (0, 1)],
        out_shape=[out, out],
        scratch_shapes=scratch + scratch,
        compiler_params=_cparams(("arbitrary", "arbitrary")),
        name="s5_scan",
    )(proj, w_in, w_out, a_r, a_i, proj, w_in, w_out, a_r, a_i)


def _s5_params(lam_re, lam_im, log_step, b_re, b_im, c_re, c_im):
    lr = jnp.minimum(lam_re.astype(F32), -1e-4)
    li = lam_im.astype(F32)
    step = jnp.exp(log_step.astype(F32))[..., None]
    mag = jnp.exp(lr * step)
    a_r = mag * jnp.cos(li * step)
    a_i = mag * jnp.sin(li * step)
    den = lr * lr + li * li
    coef_r = ((a_r - 1) * lr + a_i * li) / den
    coef_i = (a_i * lr - (a_r - 1) * li) / den
    br = b_re.astype(F32)
    bi = b_im.astype(F32)
    bb_r = coef_r[..., None] * br - coef_i[..., None] * bi
    bb_i = coef_r[..., None] * bi + coef_i[..., None] * br
    gpc = S5_LC // S5_STATE
    eye = jnp.eye(gpc, dtype=F32)

    def in_blocks(bb):
        bb = bb.reshape(2, S5_NC, gpc, S5_STATE, S5_GROUP)
        return jnp.einsum('dngpc,gh->dngchp', bb, eye).reshape(2, S5_NC, S5_UC, S5_LC)

    def out_blocks(cc):
        cc = cc.reshape(2, S5_NC, gpc, S5_GROUP, S5_STATE)
        return jnp.einsum('dngcp,gh->dngphc', cc, eye).reshape(2, S5_NC, S5_LC, S5_UC)

    w_in = jnp.concatenate([in_blocks(bb_r), in_blocks(bb_i)], axis=-1).astype(BF16)
    w_out = jnp.concatenate([out_blocks(c_re.astype(F32)), -out_blocks(c_im.astype(F32))], axis=-2).astype(BF16)

    def lanes(a):
        return jnp.broadcast_to(a.reshape(2, S5_NC, 1, S5_LC), (2, S5_NC, B, S5_LC))

    return w_in, w_out, lanes(a_r), lanes(a_i)


HG_HALF = HG_C // 2
HG_LEVELS = tuple(2 ** e for e in range(int(math.log2(HG_HALF))))


def _hgrn_maps(rev):
    pos = np.arange(HG_C)
    ph = np.arange(HG_HALF)
    if rev:
        pos = HG_C - 1 - pos
        ph = HG_HALF - 1 - ph
    tri = (pos[None, :] <= pos[:, None]).astype(np.float32)
    pt, ps = ph[:, None], ph[None, :]
    lv = np.full((HG_HALF, HG_HALF), -1, np.int32)
    lv[pt == ps] = len(HG_LEVELS)
    for e, m in enumerate(HG_LEVELS):
        x, y = pt // m, ps // m
        lv[(x == y + 1) & (x % 2 == 1)] = e
    return jnp.asarray(tri, BF16), jnp.asarray(lv)


def _hgrn_chunk(q_ref, f_ref, v_ref, lb, tri_ref, lv_ref, o_ref, s_ref, rev):
    c = HG_C
    fl = f_ref[...].astype(F32)
    qb = _silu(q_ref[...].astype(F32)).astype(BF16)
    v = v_ref[...].astype(F32)
    x2 = fl * LOG2E
    a = jnp.exp2(-jnp.abs(x2))
    s1 = 1.0 + a
    r = 1.0 / s1
    kb = ((1.0 - lb) * jnp.where(x2 >= 0.0, a * r, r)).astype(BF16)
    t2 = jnp.exp2(jnp.minimum(-x2, MAX_EXP_ARG * LOG2E))
    lf = jnp.minimum(x2, 0.0) - jnp.log2(s1) + jnp.log2(1.0 + lb * t2)

    row = lax.broadcasted_iota(jnp.int32, (c, 1), 0)
    pos = (c - 1 - row) if rev else row
    first, second = (slice(HG_HALF, c), slice(0, HG_HALF)) if rev else (slice(0, HG_HALF), slice(HG_HALF, c))
    halves = (first, second)

    def from_earlier(x, j):
        return pltpu.roll(x, (c - j) if rev else j, 0)

    def from_later(x, j):
        return pltpu.roll(x, j if rev else (c - j), 0)

    def head(x, h, rows):
        return x[rows, h * HG_DIM:(h + 1) * HG_DIM]

    hi = lf.astype(BF16)
    r1 = lf - hi.astype(F32)
    mid = r1.astype(BF16)
    lo = (r1 - mid.astype(F32)).astype(BF16)
    tri = tri_ref[...]
    cum = _dot(tri, hi) + _dot(tri, mid) + _dot(tri, lo)

    lv = lv_ref[...]
    diag = len(HG_LEVELS)
    scores = [[jnp.where(lv == diag, _dot_nt(head(qb, h, r), head(kb, h, r)), 0.0) for r in halves]
              for h in range(HG_HEADS)]

    g_end = cum
    for e, m in enumerate(HG_LEVELS):
        qm = qb * jnp.exp2(cum - from_earlier(g_end, m)).astype(BF16)
        km = kb * jnp.exp2(g_end - cum).astype(BF16)
        for h in range(HG_HEADS):
            for j, r in enumerate(halves):
                scores[h][j] = jnp.where(lv == e, _dot_nt(head(qm, h, r), head(km, h, r)), scores[h][j])
        g_end = jnp.where((pos & m) != 0, g_end, from_later(g_end, m))
    qm = qb * jnp.exp2(cum - from_earlier(g_end, HG_HALF)).astype(BF16)
    km = kb * jnp.exp2(g_end - cum).astype(BF16)
    cross = [_dot_nt(head(qm, h, second), head(km, h, first)) for h in range(HG_HEADS)]
    g_end = jnp.where((pos & HG_HALF) != 0, g_end, from_later(g_end, HG_HALF))

    q_in = qb * jnp.exp2(cum).astype(BF16)
    k_out = kb * jnp.exp2(g_end - cum).astype(BF16)
    decay = jnp.exp2(g_end[0:1, :])
    vb = v.astype(BF16)
    for h in range(HG_HEADS):
        sl = slice(h * HG_DIM, (h + 1) * HG_DIM)
        st = s_ref[h]
        inter = _dot_nt(q_in[:, sl], st.astype(BF16))
        v_first, v_second = head(vb, h, first), head(vb, h, second)
        o_ref[first, sl] = (inter[first] + _dot(scores[h][0].astype(BF16), v_first)).astype(BF16)
        o_ref[second, sl] = (inter[second] + _dot(cross[h].astype(BF16), v_first)
                             + _dot(scores[h][1].astype(BF16), v_second)).astype(BF16)
        s_ref[h] = decay[:, sl] * st + _dot(v[:, sl].T.astype(BF16), k_out[:, sl])


def _hgrn_kernel(qf_ref, ff_ref, vf_ref, qr_ref, fr_ref, vr_ref, lb_ref, trif_ref, lvf_ref, trir_ref, lvr_ref,
                 of_ref, or_ref, sf_ref, sr_ref):
    @pl.when(pl.program_id(1) == 0)
    def _():
        sf_ref[...] = jnp.zeros_like(sf_ref)
        sr_ref[...] = jnp.zeros_like(sr_ref)

    lb = lb_ref[...]
    _hgrn_chunk(qf_ref, ff_ref, vf_ref, lb, trif_ref, lvf_ref, of_ref, sf_ref, rev=False)
    _hgrn_chunk(qr_ref, fr_ref, vr_ref, lb, trir_ref, lvr_ref, or_ref, sr_ref, rev=True)


def _hgrn_scan(hg, lb):
    maps = [_hgrn_maps(rev) for rev in (False, True)]
    blk = lambda d, col: pl.BlockSpec((None, HG_C, HG_W), lambda b, i: (b, _seq_tile(d, i), col))
    state = pltpu.VMEM((HG_HEADS, HG_DIM, HG_DIM), F32)
    out = jax.ShapeDtypeStruct((B, T, HG_W), BF16)
    return pl.pallas_call(
        _hgrn_kernel,
        grid=(B, NTS),
        in_specs=[blk(0, 1), blk(0, 2), blk(0, 4), blk(1, 1), blk(1, 3), blk(1, 4),
                  _const_spec((1, HG_W)),
                  _const_spec((HG_C, HG_C)), _const_spec((HG_HALF, HG_HALF)),
                  _const_spec((HG_C, HG_C)), _const_spec((HG_HALF, HG_HALF))],
        out_specs=[blk(0, 0), blk(1, 0)],
        out_shape=[out, out],
        scratch_shapes=[state, state],
        compiler_params=_cparams(("arbitrary", "arbitrary")),
        name="hgrn_scan",
    )(hg, hg, hg, hg, hg, hg, lb, *maps[0], *maps[1])


def _ffn_tail(mix_out, h, mod_ref, g_ref, w_in_ref, w_out_ref):
    h1 = h + mod_ref[2:3, :] * mix_out
    a = _norm_mod(h1, g_ref[...], mod_ref[3:4, :], mod_ref[4:5, :]).astype(BF16)

    def gate_up(c):
        lo, hi = FFN_BOUNDS[c], FFN_BOUNDS[c + 1]
        return _dot(a, w_in_ref[:, lo:hi]), _dot(a, w_in_ref[:, FFN_H + lo:FFN_H + hi])

    ffn = None
    ahead = gate_up(0)
    for c in range(len(FFN_BOUNDS) - 1):
        gate, up = ahead
        if c + 2 < len(FFN_BOUNDS):
            ahead = gate_up(c + 1)
        part = _dot((_silu(gate) * up).astype(BF16), w_out_ref[FFN_BOUNDS[c]:FFN_BOUNDS[c + 1], :])
        ffn = part if ffn is None else ffn + part
    return h1 + mod_ref[5:6, :] * ffn


def _gelu_tanh(x):
    return 0.5 * x * (1.0 + jnp.tanh(math.sqrt(2.0 / math.pi) * (x + 0.044715 * (x * x * x))))


def _post_even_kernel(yf_ref, yb_ref, u_ref, of_ref, ob_ref, gate_ref, hl_ref, hc_ref, mod_ref, g_ref,
                      dsk_ref, gw_ref, gb_ref, on_ref, wo_ref, w_in_ref, w_out_ref, o_ref):
    def mixer_out(r):
        y = yf_ref[r].astype(F32) + yb_ref[r].astype(F32) + u_ref[r].astype(F32) * dsk_ref[...]
        z = _gelu_tanh(y)
        s5 = z * jax.nn.sigmoid(_dot(z.astype(BF16), gw_ref[...]) + gb_ref[...])
        mix = _dot(s5.astype(BF16), wo_ref[0:S5_W, :])
        o = of_ref[r].astype(F32) + ob_ref[r].astype(F32)
        gate = _silu(gate_ref[r].astype(F32))
        for h in range(HG_HEADS):
            sl = slice(h * HG_DIM, (h + 1) * HG_DIM)
            oh = o[:, sl]
            ms = jnp.mean(oh * oh, axis=-1, keepdims=True)
            hn = oh * lax.rsqrt(ms + EPS) * on_ref[...] * gate[:, sl]
            mix = mix + _dot(hn.astype(BF16), wo_ref[S5_W + h * HG_DIM:S5_W + (h + 1) * HG_DIM, :])
        return mix

    is_ctx = pl.program_id(1) == NTS - 1
    halves = (slice(0, TS // 2), slice(TS // 2, TS))
    mixes = [mixer_out(r) for r in halves]
    for mix, r in zip(mixes, halves):
        h = jnp.where(is_ctx, hc_ref[r], hl_ref[r])
        o_ref[r] = _ffn_tail(mix, h, mod_ref, g_ref, w_in_ref, w_out_ref)


def _post_even(y_f, y_b, o_f, o_b, proj, h_lat, h_ctx, ctx_block, mod_l, g, dsk, gw, gb, on, wo, w_in, w_out, l):
    tok = lambda b, t: (b, t, 0)
    return pl.pallas_call(
        _post_even_kernel,
        grid=(B, NTS),
        in_specs=[
            pl.BlockSpec((None, TS, S5_W), tok),
            pl.BlockSpec((None, TS, S5_W), tok),
            pl.BlockSpec((None, TS, S5_W), tok),
            pl.BlockSpec((None, TS, HG_W), tok),
            pl.BlockSpec((None, TS, HG_W), tok),
            pl.BlockSpec((None, TS, HG_W), lambda b, t: (b, t, 5)),
        ] + _stream_specs(TS, NTS - 1, ctx_block) + [
            _mod_spec(NTS - 1),
            _const_spec((1, D)),
            _const_spec((1, S5_W)),
            _const_spec((S5_W, S5_W)),
            _const_spec((1, S5_W)),
            _const_spec((1, HG_DIM)),
            _layer_spec((D, D), l),
            _layer_spec((D, 2 * FFN_H), l),
            _layer_spec((FFN_H, D), l),
        ],
        out_specs=pl.BlockSpec((None, TS, D), tok),
        out_shape=jax.ShapeDtypeStruct((B, T, D), F32),
        compiler_params=_cparams(("arbitrary", "arbitrary")),
        name="post_even",
    )(y_f, y_b, proj, o_f, o_b, proj, h_lat, h_ctx, mod_l, g, dsk, gw, gb, on, wo, w_in, w_out)


def _post_odd_kernel(mx_ref, h_ref, mod_ref, g_ref, wo_ref, w_in_ref, w_out_ref, o_ref):
    halves = (slice(0, TS // 2), slice(TS // 2, TS))
    mixes = [_dot(mx_ref[r], wo_ref[...]) for r in halves]
    for mix, r in zip(mixes, halves):
        o_ref[r] = _ffn_tail(mix, h_ref[r], mod_ref, g_ref, w_in_ref, w_out_ref)


def _post_odd(mx, hs, mod_l, g, wo, w_in, w_out, l, latent_only):
    nt = NTS - 1 if latent_only else NTS
    tok = lambda b, t: (b, t, 0)
    return pl.pallas_call(
        _post_odd_kernel,
        grid=(B, nt),
        in_specs=[
            pl.BlockSpec((None, TS, D), tok),
            pl.BlockSpec((None, TS, D), tok),
            _mod_spec(NTS - 1),
            _const_spec((1, D)),
            _layer_spec((D, D), l),
            _layer_spec((D, 2 * FFN_H), l),
            _layer_spec((FFN_H, D), l),
        ],
        out_specs=pl.BlockSpec((None, TS, D), tok),
        out_shape=jax.ShapeDtypeStruct((B, nt * TS, D), F32),
        compiler_params=_cparams(("arbitrary", "arbitrary")),
        name="post_odd",
    )(mx, hs, mod_l, g, wo, w_in, w_out)


def _group_rms(xs, r):
    ms = _dot((xs * xs).astype(BF16), r)
    return xs * lax.rsqrt(ms + EPS)


def _rope(xs, cos, sin_signed, first_half):
    rot = jnp.where(first_half, pltpu.roll(xs, LANE - ROPE_DIM // 4, 1), pltpu.roll(xs, ROPE_DIM // 4, 1))
    return xs * cos + rot * sin_signed


def _odd_in_kernel(x_ref, mod_ref, g_ref, w_ref, rope_ref, gv_ref, qan_ref, kvan_ref, wuq_ref, wukv_ref,
                   rpair_ref, rq_ref, rfull_ref, rhalf_ref,
                   dq_ref, k1_ref, k2_ref, dv_ref, mq_ref, mk_ref, mv_ref, *, mla_scale):
    lane = lax.broadcasted_iota(jnp.int32, (1, LANE), 1)
    first_half = (lane % (ROPE_DIM // 2)) < (ROPE_DIM // 4)
    low = lane < DIFF_DIM
    rpair, rq, rfull, rhalf = rpair_ref[...], rq_ref[...], rfull_ref[...], rhalf_ref[...]
    g_dq, g_dk = gv_ref[0:1, :], gv_ref[1:2, :]
    g_qn, g_qr, g_kn, g_kr = gv_ref[2:3, :], gv_ref[3:4, :], gv_ref[4:5, :], gv_ref[5:6, :]
    diff_scale = DIFF_DIM ** -0.5 * LOG2E
    o_cq = 3 * DIFF_W
    o_ckv = o_cq + MLA_Q_RANK
    o_kr = o_ckv + MLA_KV_RANK

    def body(n):
        rows = slice(0, n)
        a = _norm_mod(x_ref[rows], g_ref[...], mod_ref[0:1, :], mod_ref[1:2, :]).astype(BF16)
        cos_d, sin_d, cos_m, sin_m = rope_ref[0, rows], rope_ref[1, rows], rope_ref[2, rows], rope_ref[3, rows]
        ones = jnp.ones((n, LANE), BF16)

        def proj(c0, c1):
            return _dot(a, w_ref[:, c0:c1])

        def diff_pair(pq, pk, h2):
            qn = _group_rms(pq, rpair)
            kn = _group_rms(pk, rpair)
            for j in range(2):
                sl = slice((h2 + j) * LANE, (h2 + j + 1) * LANE)
                half = slice(j * LANE, (j + 1) * LANE)
                qh = _rope(qn[:, half] * g_dq, cos_d, sin_d, first_half)
                dq_ref[rows, sl] = (qh * diff_scale).astype(BF16)
                kh = _rope(kn[:, half] * g_dk, cos_d, sin_d, first_half)
                k1_ref[rows, sl] = jnp.where(low, kh, 0.0).astype(BF16)
                k2_ref[rows, sl] = jnp.where(low, 0.0, kh).astype(BF16)

        def diff_values(pv):
            for h in range(DIFF_HEADS):
                dv_ref[rows, 2 * h * LANE:(2 * h + 1) * LANE] = pv[:, h * LANE:(h + 1) * LANE].astype(BF16)
                dv_ref[rows, (2 * h + 1) * LANE:(2 * h + 2) * LANE] = ones

        def latent_heads(pm):
            cq = pm[:, 0:MLA_Q_RANK]
            cqn = cq * lax.rsqrt(jnp.mean(cq * cq, axis=-1, keepdims=True) + EPS) * qan_ref[...]
            q = _dot(cqn.astype(BF16), wuq_ref[...])
            ckv = pm[:, MLA_Q_RANK:MLA_Q_RANK + MLA_KV_RANK]
            ckvn = ckv * lax.rsqrt(jnp.mean(ckv * ckv, axis=-1, keepdims=True) + EPS) * kvan_ref[...]
            kv = _dot(ckvn.astype(BF16), wukv_ref[...])
            k_rope = _rope(_group_rms(pm[:, MLA_Q_RANK + MLA_KV_RANK:], rhalf) * g_kr,
                           cos_m, sin_m, first_half).astype(BF16)
            for h in range(MLA_HEADS):
                o = 2 * LANE * h
                qn = _group_rms(q[:, o:o + 2 * LANE], rq)
                q_nope = qn[:, :LANE] * g_qn
                q_rope = _rope(qn[:, LANE:] * g_qr, cos_m, sin_m, first_half)
                mq_ref[rows, o:o + LANE] = (q_nope * mla_scale).astype(BF16)
                mq_ref[rows, o + LANE:o + 2 * LANE] = (q_rope * mla_scale).astype(BF16)
                mk_ref[rows, o:o + LANE] = (_group_rms(kv[:, o:o + LANE], rfull) * g_kn).astype(BF16)
                mk_ref[rows, o + LANE:o + 2 * LANE] = k_rope
                mv_ref[rows, o:o + LANE] = kv[:, o + LANE:o + 2 * LANE].astype(BF16)
                mv_ref[rows, o + LANE:o + 2 * LANE] = ones

        pair = 2 * LANE
        pm = proj(o_cq, CD_PAD)
        pq0, pk0 = proj(0, pair), proj(DIFF_W, DIFF_W + pair)
        latent_heads(pm)
        pq1, pk1 = proj(pair, 2 * pair), proj(DIFF_W + pair, DIFF_W + 2 * pair)
        diff_pair(pq0, pk0, 0)
        pv = proj(2 * DIFF_W, 3 * DIFF_W)
        diff_pair(pq1, pk1, 2)
        diff_values(pv)

    _for_tile_rows(True, body)


def _odd_in(hs, mod_l, g, w_all, o, rope, gv, qan, kvan, wuq, wukv, rmats):
    tok = lambda b, t: (b, t, 0)
    sds = lambda n: jax.ShapeDtypeStruct((B, T, n), BF16)
    wide = 2 * LANE * MLA_HEADS
    widths = (DIFF_W, DIFF_W, DIFF_W, wide, wide, wide, wide)
    return pl.pallas_call(
        functools.partial(_odd_in_kernel, mla_scale=(MLA_NOPE + MLA_ROPE) ** -0.5 * LOG2E),
        grid=(B, NTL + 1),
        in_specs=[
            pl.BlockSpec((None, TM, D), tok),
            _mod_spec(NTL),
            _const_spec((1, D)),
            _layer_spec((D, CD_PAD), o),
            pl.BlockSpec((4, TM, LANE), lambda b, t: (0, t, 0)),
            _const_spec((SUBLANE, LANE)),
            _const_spec((1, MLA_Q_RANK)),
            _const_spec((1, MLA_KV_RANK)),
            _const_spec((MLA_Q_RANK, 2 * LANE * MLA_HEADS)),
            _const_spec((MLA_KV_RANK, 2 * LANE * MLA_HEADS)),
            _const_spec((2 * LANE, 2 * LANE)),
            _const_spec((2 * LANE, 2 * LANE)),
            _const_spec((LANE, LANE)),
            _const_spec((LANE, LANE)),
        ],
        out_specs=[pl.BlockSpec((None, TM, n), tok) for n in widths],
        out_shape=[sds(n) for n in widths],
        compiler_params=_cparams(("arbitrary", "arbitrary")),
        name="odd_in_proj",
    )(hs, mod_l, g, w_all, rope, gv, qan, kvan, wuq, wukv, *rmats)


def _exp2_shifted(s):
    return jnp.exp2(s - jnp.max(s, axis=-1, keepdims=True)).astype(BF16)


def _pv_normalised(e, v_aug):
    o = _dot(e, v_aug)
    return o[:, :LANE] / o[:, LANE:]


def _attn_kernel(lam_ref, dq_ref, k1_ref, k2_ref, dv_ref, mq_ref, mk_ref, mv_ref, sub_ref, o_ref,
                 *, out_scale, has_ctx):
    lam = lam_ref[0, 0]

    def body(keys):
        units = []
        for h in range(DIFF_HEADS):
            sl = slice(h * LANE, (h + 1) * LANE)
            sv = slice(2 * LANE * h, 2 * LANE * (h + 1))
            units += [(dq_ref, sl, k1_ref, dv_ref, sv), (dq_ref, sl, k2_ref, dv_ref, sv)]
        for h in range(MLA_HEADS):
            sq = slice(2 * LANE * h, 2 * LANE * (h + 1))
            units.append((mq_ref, sq, mk_ref, mv_ref, sq))

        def scores(u):
            q_ref, qs, k_ref, _, _ = u
            return _dot_nt(q_ref[:, qs], k_ref[keys, qs])

        outs = []
        ahead = [scores(u) for u in units[:ATTN_AHEAD]]
        for i, u in enumerate(units):
            s = ahead.pop(0)
            if i + ATTN_AHEAD < len(units):
                ahead.append(scores(units[i + ATTN_AHEAD]))
            outs.append(_pv_normalised(_exp2_shifted(s), u[3][keys, u[4]]))
        for h in range(DIFF_HEADS):
            o = outs[2 * h] - lam * outs[2 * h + 1]
            o = o * lax.rsqrt(jnp.mean(o * o, axis=-1, keepdims=True) + EPS) * sub_ref[...] * out_scale
            o_ref[:, h * LANE:(h + 1) * LANE] = o.astype(BF16)
        for h in range(MLA_HEADS):
            o_ref[:, DIFF_W + h * LANE:DIFF_W + (h + 1) * LANE] = outs[2 * DIFF_HEADS + h].astype(BF16)

    if has_ctx:
        t = pl.program_id(1)
        pl.when(t == 0)(lambda: body(slice(SEQ, T)))
        pl.when(t > 0)(lambda: body(slice(0, T)))
    else:
        body(slice(0, T))


def _attention(lam, dq, k1, k2, dv, mq, mk, mv, sub, out_scale, latent_only):
    has_ctx = not latent_only
    nt = NTS if has_ctx else NTS - 1
    wide = 2 * LANE * MLA_HEADS
    tile = (lambda t: (t + NTS - 1) % NTS) if has_ctx else (lambda t: t)
    qspec = lambda n: pl.BlockSpec((None, TS, n), lambda b, t: (b, tile(t), 0))
    kspec = lambda n: pl.BlockSpec((None, T, n), lambda b, t: (b, 0, 0))
    return pl.pallas_call(
        functools.partial(_attn_kernel, out_scale=out_scale, has_ctx=has_ctx),
        grid=(B, nt),
        in_specs=[
            pl.BlockSpec(memory_space=pltpu.SMEM),
            qspec(DIFF_W), kspec(DIFF_W), kspec(DIFF_W), kspec(wide),
            qspec(wide), kspec(wide), kspec(wide),
            _const_spec((1, LANE)),
        ],
        out_specs=qspec(D),
        out_shape=jax.ShapeDtypeStruct((B, nt * TS, D), BF16),
        compiler_params=_cparams(("arbitrary", "arbitrary")),
        name="attention",
    )(lam, dq, k1, k2, dv, mq, mk, mv, sub)


def _rope_tables():
    n_tok = SEQ
    rows = jnp.repeat(jnp.arange(n_tok // GRID_W, dtype=jnp.int32), GRID_W)
    cols = jnp.tile(jnp.arange(GRID_W, dtype=jnp.int32), n_tok // GRID_W)
    n_freq = ROPE_DIM // 4
    inv = jnp.power(ROPE_BASE, -jnp.arange(n_freq, dtype=F32) / n_freq)
    ang_r = rows.astype(F32)[:, None] * inv
    ang_c = cols.astype(F32)[:, None] * inv
    ang = jnp.concatenate([ang_r, ang_r, ang_c, ang_c], axis=-1)
    sign = jnp.where((jnp.arange(ROPE_DIM) % (ROPE_DIM // 2)) < n_freq, -1.0, 1.0).astype(F32)
    cos = jnp.concatenate([jnp.cos(ang), jnp.ones((CTX, ROPE_DIM), F32)], axis=0)
    sin = jnp.concatenate([jnp.sin(ang) * sign, jnp.zeros((CTX, ROPE_DIM), F32)], axis=0)
    one, zero = jnp.ones_like(cos), jnp.zeros_like(sin)
    cat = lambda a, b: jnp.concatenate([a, b], axis=-1)
    return jnp.stack([cat(cos, cos), cat(sin, sin), cat(cos, one), cat(sin, zero)])


def _averaging_mats():
    idx = np.arange(2 * LANE)
    same64 = (idx[:, None] // DIFF_DIM) == (idx[None, :] // DIFF_DIM)
    rpair = np.where(same64, 1.0 / DIFF_DIM, 0.0)
    rfull = np.full((LANE, LANE), 1.0 / LANE)
    il = np.arange(LANE)
    rhalf = np.where((il[:, None] < MLA_ROPE) & (il[None, :] < MLA_ROPE), 1.0 / MLA_ROPE, 0.0)
    rq = np.zeros((2 * LANE, 2 * LANE))
    rq[:LANE, :LANE] = rfull
    rq[LANE:, LANE:] = rhalf
    return tuple(jnp.asarray(m, BF16) for m in (rpair, rq, rfull, rhalf))


def kernel(x, c, ctx, c_ctx, ada_w, ada_b, norm_mix, norm_ffn, w_out, ffn_w_in, ffn_w_out, ab_w_in, s5_lambda_re, s5_lambda_im, s5_log_step, s5_b_re, s5_b_im, s5_c_re, s5_c_im, s5_d, s5_glu_w, s5_glu_b, hgrn_lb_logits, hgrn_out_norm, cd_w_in, diff_lambda, diff_qk_norm, diff_subln, mla_q_a_norm, mla_kv_a_norm, mla_w_uq, mla_w_ukv, mla_nope_norm, mla_rope_norm):
    assert x.shape == (B, SEQ, D) and ctx.shape == (B, CTX, D)
    mod = _modulation(c, c_ctx, ada_w, ada_b)
    wo_all = w_out.astype(BF16)
    w1_all = ffn_w_in.astype(BF16)
    w2_all = ffn_w_out.astype(BF16)
    wab_all = ab_w_in.astype(BF16)
    wcd_all = jnp.pad(cd_w_in, ((0, 0), (0, 0), (0, CD_PAD - CD_IN))).astype(BF16)
    h_lat, h_ctx, ctx_block = x, ctx, 0
    rope = _rope_tables()
    rmats = _averaging_mats()

    lb_p = jax.nn.softmax(hgrn_lb_logits.astype(F32), axis=0)
    lower_bounds = jnp.cumsum(lb_p, axis=0) - lb_p[0:1]

    for l in range(DEPTH):
        last = l == DEPTH - 1
        g_mix = norm_mix[l].reshape(1, D)
        g_ffn = norm_ffn[l].reshape(1, D)
        if l % 2 == 0:
            e = l // 2
            proj = _even_in(h_lat, h_ctx, ctx_block, mod[l], g_mix, wab_all, e)
            s5w_in, s5w_out, a_r, a_i = _s5_params(s5_lambda_re[e], s5_lambda_im[e], s5_log_step[e],
                                                   s5_b_re[e], s5_b_im[e], s5_c_re[e], s5_c_im[e])
            y_f, y_b = _s5_scan(proj, s5w_in, s5w_out, a_r, a_i)
            lb = lower_bounds[e].reshape(1, HG_W)
            o_f, o_b = _hgrn_scan(proj, lb)
            hs = _post_even(y_f, y_b, o_f, o_b, proj, h_lat, h_ctx, ctx_block, mod[l], g_ffn,
                            s5_d[e].reshape(1, S5_W), s5_glu_w[e].astype(BF16), s5_glu_b[e].reshape(1, S5_W),
                            hgrn_out_norm[e].reshape(1, HG_DIM), wo_all, w1_all, w2_all, l)
        else:
            o = l // 2
            lam_init = 0.8 - 0.6 * math.exp(-0.3 * l)
            lv = diff_lambda[o].astype(F32)
            lam = (jnp.exp(jnp.sum(lv[0] * lv[1])) - jnp.exp(jnp.sum(lv[2] * lv[3])) + lam_init).reshape(1, 1)
            wuq = mla_w_uq[o].reshape(MLA_Q_RANK, MLA_HEADS, MLA_NOPE + MLA_ROPE)
            wuq = jnp.pad(wuq, ((0, 0), (0, 0), (0, 2 * LANE - MLA_NOPE - MLA_ROPE)))
            wuq = wuq.reshape(MLA_Q_RANK, 2 * LANE * MLA_HEADS).astype(BF16)
            pad_r = lambda v: jnp.pad(v, (0, LANE - MLA_ROPE))
            gv = jnp.zeros((SUBLANE, LANE), F32)
            gv = gv.at[0].set(jnp.tile(diff_qk_norm[o, 0], 2)).at[1].set(jnp.tile(diff_qk_norm[o, 1], 2))
            gv = gv.at[2].set(mla_nope_norm[o, 0]).at[3].set(pad_r(mla_rope_norm[o, 0]))
            gv = gv.at[4].set(mla_nope_norm[o, 1]).at[5].set(pad_r(mla_rope_norm[o, 1]))
            parts = _odd_in(hs, mod[l], g_mix, wcd_all, o, rope, gv, mla_q_a_norm[o].reshape(1, MLA_Q_RANK),
                            mla_kv_a_norm[o].reshape(1, MLA_KV_RANK), wuq, mla_w_ukv[o].astype(BF16), rmats)
            mx = _attention(lam, *parts, diff_subln[o].reshape(1, LANE), 1.0 - lam_init, latent_only=last)
            hs = _post_odd(mx, hs, mod[l], g_ffn, wo_all, w1_all, w2_all, l, latent_only=last)
        h_lat, h_ctx, ctx_block = hs, hs, SEQ // CTX
    return hs
```
